```python
import jax, jax.numpy as jnp
from jax import lax
import numpy as np

D_MODEL = 1024
BATCH = 8
SEQ = 2048
DEPTH = 2
DEC_BATCH = 32
DEC_SEQ = 8
PAST_LEN = 16384
PAGE_SIZE = 128

HEAD_DIM = 64
H_A = 8
H_B = 8
H_C = D_MODEL // HEAD_DIM
D_A = H_A * HEAD_DIM
D_B = H_B * HEAD_DIM
D_C = H_C * HEAD_DIM
LORA_W = 64
LORA_A = 64
LORA_G = 128
RW_SPLITS = (D_A, 2 * D_A, 3 * D_A, 3 * D_A + LORA_W, 3 * D_A + LORA_W + LORA_A)
RW_COLS = 3 * D_A + LORA_W + LORA_A + LORA_G
FOX_COLS = 3 * D_B + H_B
EVEN_COLS = RW_COLS + FOX_COLS
ODD_COLS = 3 * D_C
D_FF = 4 * D_MODEL
DIL_PAIRS = ((128, 1), (512, 4), (2048, 16))
MAX_WINDOW = max(w for w, _ in DIL_PAIRS)
Q_BLOCK = 128
ROPE_THETA = 10000.0
NORM_EPS = 1e-6
LNX_EPS = 64e-5
ATTN_SCALE = HEAD_DIM ** -0.5
N_EVEN = (DEPTH + 1) // 2
N_ODD = DEPTH // 2
f32 = jnp.float32

kernel_name = "rwkv7_fox_dilated_hybrid_step"


def _rmsnorm(x, g):
    xf = x.astype(f32)
    y = xf * lax.rsqrt(jnp.mean(xf * xf, axis=-1, keepdims=True) + NORM_EPS)
    return (y * g).astype(x.dtype)


def _sublayer(x, c, w_ada, b_ada, g_pre, g_post, fn):
    mod = (jax.nn.silu(c) @ w_ada + b_ada)[:, None, :]
    shift, scale, gate = jnp.split(mod, 3, axis=-1)
    h = _rmsnorm(x, g_pre) * (1 + scale) + shift
    y, aux = fn(h)
    return x + gate * _rmsnorm(y, g_post), aux


def _rope(x, pos):
    half = x.shape[-1] // 2
    inv = ROPE_THETA ** (-jnp.arange(half, dtype=f32) / half)
    ang = pos.astype(f32)[:, None] * inv[None, :]
    cos = jnp.cos(ang)[None, :, None, :]
    sin = jnp.sin(ang)[None, :, None, :]
    x1 = x[..., :half].astype(f32)
    x2 = x[..., half:].astype(f32)
    return jnp.concatenate([x1 * cos - x2 * sin, x2 * cos + x1 * sin], axis=-1).astype(x.dtype)


def _gather_pages(pool, page_table):
    g = pool[page_table]
    return g.reshape((g.shape[0], g.shape[1] * g.shape[2]) + pool.shape[2:])


def _wkv_scan(s0, r, w, k, v, kk, a):
    def step(s, inp):
        r_t, w_t, k_t, v_t, kk_t, a_t = inp
        s_kk = jnp.einsum('bhvk,bhk->bhv', s, kk_t)
        s = s * w_t[:, :, None, :] - s_kk[..., None] * (kk_t * a_t)[:, :, None, :] + v_t[..., None] * k_t[:, :, None, :]
        return s, jnp.einsum('bhvk,bhk->bhv', s, r_t)
    xs = tuple(jnp.moveaxis(t.astype(f32), 1, 0) for t in (r, w, k, v, kk, a))
    s, ys = lax.scan(step, s0.astype(f32), xs)
    return s, jnp.moveaxis(ys, 0, 1)


def _rwkv7(p, shift0, wkv0, mu, w0, w2, a0, a2, g2, k_k, k_a, r_k, lnx_w, lnx_b):
    B, T, _ = p.shape
    p_prev = jnp.concatenate([shift0[:, None, :].astype(p.dtype), p[:, :-1]], axis=1)
    p = p + mu * (p_prev - p)
    r, k, v, dw, da, dg = jnp.split(p, RW_SPLITS, axis=-1)
    w_log = -jax.nn.softplus(-(w0 + jnp.tanh(dw) @ w2).astype(f32)) - 0.5
    decay = jnp.exp(-jnp.exp(w_log))
    a = jax.nn.sigmoid(a0 + da @ a2)
    g = jax.nn.sigmoid(dg) @ g2
    heads = lambda t: t.reshape(B, T, H_A, HEAD_DIM)
    kk = heads(k * k_k).astype(f32)
    kk = kk / jnp.maximum(jnp.linalg.norm(kk, axis=-1, keepdims=True), 1e-12)
    k = k * (1 + (a - 1) * k_a)
    r, k, v, a, decay = (heads(t) for t in (r, k, v, a, decay))
    wkv, y = _wkv_scan(wkv0, r, decay, k, v, kk, a)
    mean = jnp.mean(y, axis=-1, keepdims=True)
    var = jnp.mean(jnp.square(y - mean), axis=-1, keepdims=True)
    y = ((y - mean) * lax.rsqrt(var + LNX_EPS)).reshape(B, T, D_A) * lnx_w + lnx_b
    bonus = (jnp.sum(r * k * r_k, axis=-1, keepdims=True) * v).reshape(B, T, D_A)
    return ((y + bonus) * g).astype(p.dtype), wkv


def _fox_attend(q, k, v, cum_q, cum_k, q_off):
    B, T, H, D = q.shape
    n_keys = k.shape[1]
    qb = min(Q_BLOCK, T)
    nb = T // qb
    cum_k_t = jnp.transpose(cum_k, (0, 2, 1))
    k_idx = jnp.arange(n_keys)

    def block(args):
        q_i, cq_i, pos_i = args
        s = jnp.einsum('bqhd,bkhd->bhqk', q_i, k).astype(f32) * ATTN_SCALE
        s = s + jnp.transpose(cq_i, (0, 2, 1))[..., None] - cum_k_t[:, :, None, :]
        s = jnp.where(k_idx[None, None, None, :] <= pos_i[None, None, :, None], s, -jnp.inf)
        pr = jax.nn.softmax(s, axis=-1)
        return jnp.einsum('bhqk,bkhd->bqhd', pr.astype(v.dtype), v)

    qs = jnp.moveaxis(q.reshape(B, nb, qb, H, D), 1, 0)
    cqs = jnp.moveaxis(cum_q.reshape(B, nb, qb, H), 1, 0)
    poss = (q_off + jnp.arange(T)).reshape(nb, qb)
    out = lax.map(block, (qs, cqs, poss))
    return jnp.moveaxis(out, 0, 1).reshape(B, T, H, D)


def _fox(p, b_f, past):
    B, T, _ = p.shape
    q, k, v, f_logit = jnp.split(p, [D_B, 2 * D_B, 3 * D_B], axis=-1)
    q, k, v = (t.reshape(B, T, H_B, HEAD_DIM) for t in (q, k, v))
    logf = jax.nn.log_sigmoid((f_logit + b_f).astype(f32))
    if past is None:
        k_all, v_all, lf_all = k, v, logf
    else:
        pk, pv, plf = past
        k_all = jnp.concatenate([pk.astype(k.dtype), k], axis=1)
        v_all = jnp.concatenate([pv.astype(v.dtype), v], axis=1)
        lf_all = jnp.concatenate([plf.astype(f32), logf], axis=1)
    cum = jnp.cumsum(lf_all, axis=1)
    n_keys = k_all.shape[1]
    y = _fox_attend(q, k_all, v_all, cum[:, n_keys - T:], cum, n_keys - T)
    return y.reshape(B, T, D_B), k, v, logf


def _even_mixer(h, w_in, w_out, rw, b_f, shift0, wkv0, past):
    proj = h @ w_in
    y_a, wkv = _rwkv7(proj[..., :RW_COLS], shift0, wkv0, *rw)
    y_b, k_b, v_b, logf = _fox(proj[..., RW_COLS:], b_f, past)
    y = jnp.concatenate([y_a, y_b.astype(y_a.dtype)], axis=-1) @ w_out
    return y, (k_b, v_b, logf, wkv, proj[:, -1, :RW_COLS])


def _band_stats(q, k, v, span):
    N, L, H, D = q.shape
    nb = L // Q_BLOCK
    qb = q.reshape(N, nb, Q_BLOCK, H, D)
    kc = k.reshape(N, nb, Q_BLOCK, H, D)
    vc = v.reshape(N, nb, Q_BLOCK, H, D)
    pad = ((0, 0), (1, 0), (0, 0), (0, 0), (0, 0))
    kb = jnp.concatenate([jnp.pad(kc, pad)[:, :-1], kc], axis=2)
    vb = jnp.concatenate([jnp.pad(vc, pad)[:, :-1], vc], axis=2)
    s = jnp.einsum('nbqhd,nbkhd->nbhqk', qb, kb).astype(f32) * ATTN_SCALE
    u = jnp.arange(Q_BLOCK)[:, None]
    w = jnp.arange(2 * Q_BLOCK)[None, :]
    dist = u - w + Q_BLOCK
    ok = (dist >= 0) & (dist <= span)
    ok = ok[None] & ((jnp.arange(nb)[:, None, None] > 0) | (w >= Q_BLOCK)[None])
    s = jnp.where(ok[None, :, None], s, -jnp.inf)
    m = jnp.max(s, axis=-1)
    pr = jnp.exp(s - m[..., None])
    den = jnp.sum(pr, axis=-1)
    o = jnp.einsum('nbhqk,nbkhd->nbqhd', pr, vb.astype(f32)) / jnp.swapaxes(den, 2, 3)[..., None]
    m = jnp.swapaxes(m, 2, 3).reshape(N, L, H)
    den = jnp.swapaxes(den, 2, 3).reshape(N, L, H)
    return m, den, o.reshape(N, L, H, D)


def _dilated_band(q, k, v, d, span):
    B, L, H, D = q.shape
    ls = L // d
    pad = (-ls) % Q_BLOCK

    def to_sub(t):
        t = jnp.transpose(t.reshape(B, ls, d, H, D), (0, 2, 1, 3, 4)).reshape(B * d, ls, H, D)
        return jnp.pad(t, ((0, 0), (0, pad), (0, 0), (0, 0)))

    def from_sub(t):
        t = t[:, :ls].reshape((B, d, ls) + t.shape[2:])
        return jnp.moveaxis(t, 1, 2).reshape((B, L) + t.shape[3:])

    m, den, o = _band_stats(to_sub(q), to_sub(k), to_sub(v), span)
    return from_sub(m), from_sub(den), from_sub(o)


def _dilated_gather(q, k_all, v_all, q_idx, d, span):
    idx = q_idx[:, None] - d * jnp.arange(span + 1)[None, :]
    valid = idx >= 0
    idx = jnp.maximum(idx, 0)
    kg = k_all[:, idx]
    vg = v_all[:, idx]
    s = jnp.einsum('nqhd,nqkhd->nqhk', q, kg).astype(f32) * ATTN_SCALE
    s = jnp.where(valid[None, :, None, :], s, -jnp.inf)
    m = jnp.max(s, axis=-1)
    pr = jnp.exp(s - m[..., None])
    den = jnp.sum(pr, axis=-1)
    o = jnp.einsum('nqhk,nqkhd->nqhd', pr, vg.astype(f32)) / den[..., None]
    return m, den, o


def _mix_branches(stats):
    m = jnp.stack([st[0] for st in stats])
    den = jnp.stack([st[1] for st in stats])
    o = jnp.stack([st[2] for st in stats])
    wt = den * jnp.exp(m - jnp.max(m, axis=0))
    return jnp.sum(wt[..., None] * o, axis=0) / jnp.sum(wt, axis=0)[..., None]


def _odd_mixer(h, w_in, w_out, pos0, past_k, past_v):
    B, T, _ = h.shape
    q, k, v = (t.reshape(B, T, H_C, HEAD_DIM) for t in jnp.split(h @ w_in, 3, axis=-1))
    pos = pos0 + jnp.arange(T)
    q = _rope(q, pos)
    k = _rope(k, pos)
    if past_k is None:
        stats = [_dilated_band(q, k, v, d, w // d) for w, d in DIL_PAIRS]
        keep = min(MAX_WINDOW, T)
        new_k, new_v = k[:, T - keep:], v[:, T - keep:]
    else:
        k_all = jnp.concatenate([past_k.astype(k.dtype), k], axis=1)
        v_all = jnp.concatenate([past_v.astype(v.dtype), v], axis=1)
        q_idx = past_k.shape[1] + jnp.arange(T)
        stats = [_dilated_gather(q, k_all, v_all, q_idx, d, w // d) for w, d in DIL_PAIRS]
        new_k, new_v = k, v
    y = _mix_branches(stats).reshape(B, T, D_C).astype(h.dtype) @ w_out
    return y, (new_k, new_v)


def _mlp(h, w_up, w_down):
    return jnp.square(jax.nn.relu(h @ w_up)) @ w_down


def setup_inputs(seed: int = 0) -> dict:
    key = jax.random.key(seed)
    keys = jax.random.split(key, 40)
    it = iter(range(40))
    nk = lambda: keys[next(it)]
    nrm = lambda shape, s=1.0: s * jax.random.normal(nk(), shape, f32)
    n_pages = PAST_LEN // PAGE_SIZE
    n_used = DEC_BATCH * n_pages
    n_phys = n_used + n_used // 4
    cbuf = min(MAX_WINDOW, PAST_LEN)
    d = D_MODEL
    inp = {}
    inp['x_prompt'] = nrm((BATCH, SEQ, d))
    inp['x_sample'] = nrm((DEC_BATCH, DEC_SEQ, d))
    inp['cache_fox_k'] = nrm((N_EVEN, n_phys, PAGE_SIZE, H_B, HEAD_DIM))
    inp['cache_fox_v'] = nrm((N_EVEN, n_phys, PAGE_SIZE, H_B, HEAD_DIM))
    inp['cache_fox_logf'] = jax.nn.log_sigmoid(2.0 + nrm((N_EVEN, n_phys, PAGE_SIZE, H_B)))
    inp['state_rwkv_wkv'] = nrm((N_EVEN, DEC_BATCH, H_A, HEAD_DIM, HEAD_DIM), 0.3)
    inp['state_rwkv_shift'] = nrm((N_EVEN, DEC_BATCH, RW_COLS))
    inp['state_dil_k'] = nrm((N_ODD, DEC_BATCH, cbuf, H_C, HEAD_DIM))
    inp['state_dil_v'] = nrm((N_ODD, DEC_BATCH, cbuf, H_C, HEAD_DIM))
    perm = jax.random.permutation(nk(), n_phys)
    inp['page_table'] = perm[:n_used].reshape(DEC_BATCH, n_pages).astype(jnp.int32)
    inp['c_prompt'] = nrm((BATCH, d))
    inp['c_sample'] = nrm((DEC_BATCH, d))
    inp['ada_w'] = nrm((DEPTH, 2, d, 3 * d), 0.5 * d ** -0.5)
    inp['ada_b'] = nrm((DEPTH, 2, 3 * d), 0.02)
    inp['norm_pre'] = 1.0 + nrm((DEPTH, 2, d), 0.05)
    inp['norm_post'] = 1.0 + nrm((DEPTH, 2, d), 0.05)
    inp['mlp_up'] = nrm((DEPTH, d, D_FF), d ** -0.5)
    inp['mlp_down'] = nrm((DEPTH, D_FF, d), D_FF ** -0.5)
    inp['ev_w_in'] = nrm((N_EVEN, d, EVEN_COLS), d ** -0.5)
    inp['ev_w_out'] = nrm((N_EVEN, D_A + D_B, d), (D_A + D_B) ** -0.5)
    inp['rw_mu'] = jax.random.uniform(nk(), (N_EVEN, RW_COLS), f32)
    inp['rw_w0'] = -1.5 + nrm((N_EVEN, D_A), 0.5)
    inp['rw_w2'] = nrm((N_EVEN, LORA_W, D_A), 0.5 * LORA_W ** -0.5)
    inp['rw_a0'] = nrm((N_EVEN, D_A), 0.1)
    inp['rw_a2'] = nrm((N_EVEN, LORA_A, D_A), LORA_A ** -0.5)
    inp['rw_g2'] = nrm((N_EVEN, LORA_G, D_A), LORA_G ** -0.5)
    inp['rw_k_k'] = 0.85 + nrm((N_EVEN, D_A), 0.05)
    inp['rw_k_a'] = 1.0 + nrm((N_EVEN, D_A), 0.05)
    inp['rw_r_k'] = nrm((N_EVEN, H_A, HEAD_DIM), 0.1)
    inp['rw_lnx_w'] = 1.0 + nrm((N_EVEN, D_A), 0.05)
    inp['rw_lnx_b'] = nrm((N_EVEN, D_A), 0.02)
    inp['fox_b_f'] = 2.0 + nrm((N_EVEN, H_B), 0.5)
    inp['od_w_in'] = nrm((N_ODD, d, ODD_COLS), d ** -0.5)
    inp['od_w_out'] = nrm((N_ODD, D_C, d), D_C ** -0.5)
    return inp


def reference(x_prompt, x_sample, cache_fox_k, cache_fox_v, cache_fox_logf,
              state_rwkv_wkv, state_rwkv_shift, state_dil_k, state_dil_v,
              page_table, c_prompt, c_sample,
              ada_w, ada_b, norm_pre, norm_post, mlp_up, mlp_down,
              ev_w_in, ev_w_out, rw_mu, rw_w0, rw_w2, rw_a0, rw_a2, rw_g2,
              rw_k_k, rw_k_a, rw_r_k, rw_lnx_w, rw_lnx_b, fox_b_f,
              od_w_in, od_w_out):
    past_len = page_table.shape[1] * PAGE_SIZE
    xp, xs = x_prompt, x_sample
    bp, bs = xp.shape[0], xs.shape[0]
    fk_p, fv_p, fl_p, fk_s, fv_s, fl_s = [], [], [], [], [], []
    wkv_p, wkv_s, sh_p, sh_s = [], [], [], []
    dk_p, dv_p, dk_s, dv_s = [], [], [], []
    for l in range(DEPTH):
        i = l // 2
        mix_args = (ada_w[l, 0], ada_b[l, 0], norm_pre[l, 0], norm_post[l, 0])
        if l % 2 == 0:
            rw = (rw_mu[i], rw_w0[i], rw_w2[i], rw_a0[i], rw_a2[i], rw_g2[i],
                  rw_k_k[i], rw_k_a[i], rw_r_k[i], rw_lnx_w[i], rw_lnx_b[i])
            past = (_gather_pages(cache_fox_k[i], page_table),
                    _gather_pages(cache_fox_v[i], page_table),
                    _gather_pages(cache_fox_logf[i], page_table))
            xp, aux_p = _sublayer(xp, c_prompt, *mix_args, lambda h: _even_mixer(
                h, ev_w_in[i], ev_w_out[i], rw, fox_b_f[i],
                jnp.zeros((bp, RW_COLS), h.dtype), jnp.zeros((bp, H_A, HEAD_DIM, HEAD_DIM), f32), None))
            xs, aux_s = _sublayer(xs, c_sample, *mix_args, lambda h: _even_mixer(
                h, ev_w_in[i], ev_w_out[i], rw, fox_b_f[i],
                state_rwkv_shift[i], state_rwkv_wkv[i], past))
            fk_p.append(aux_p[0]); fv_p.append(aux_p[1]); fl_p.append(aux_p[2])
            wkv_p.append(aux_p[3]); sh_p.append(aux_p[4])
            fk_s.append(aux_s[0]); fv_s.append(aux_s[1]); fl_s.append(aux_s[2])
            wkv_s.append(aux_s[3]); sh_s.append(aux_s[4])
        else:
            xp, aux_p = _sublayer(xp, c_prompt, *mix_args, lambda h: _odd_mixer(
                h, od_w_in[i], od_w_out[i], 0, None, None))
            xs, aux_s = _sublayer(xs, c_sample, *mix_args, lambda h: _odd_mixer(
                h, od_w_in[i], od_w_out[i], past_len, state_dil_k[i], state_dil_v[i]))
            dk_p.append(aux_p[0]); dv_p.append(aux_p[1])
            dk_s.append(aux_s[0]); dv_s.append(aux_s[1])
        ffn_args = (ada_w[l, 1], ada_b[l, 1], norm_pre[l, 1], norm_post[l, 1])
        ffn = lambda h: (_mlp(h, mlp_up[l], mlp_down[l]), None)
        xp, _ = _sublayer(xp, c_prompt, *ffn_args, ffn)
        xs, _ = _sublayer(xs, c_sample, *ffn_args, ffn)
    fox_k_p, fox_v_p, fox_logf_p = jnp.stack(fk_p), jnp.stack(fv_p), jnp.stack(fl_p)
    fox_k_s, fox_v_s, fox_logf_s = jnp.stack(fk_s), jnp.stack(fv_s), jnp.stack(fl_s)
    wkv_prompt, wkv_sample = jnp.stack(wkv_p), jnp.stack(wkv_s)
    shift_prompt, shift_sample = jnp.stack(sh_p), jnp.stack(sh_s)
    dil_k_p, dil_v_p = jnp.stack(dk_p), jnp.stack(dv_p)
    dil_k_s, dil_v_s = jnp.stack(dk_s), jnp.stack(dv_s)
    return (xp, xs, fox_k_p, fox_v_p, fox_logf_p, fox_k_s, fox_v_s, fox_logf_s,
            wkv_prompt, wkv_sample, shift_prompt, shift_sample,
            dil_k_p, dil_v_p, dil_k_s, dil_v_s)
```

```python
import functools

import numpy as np
import jax
import jax.numpy as jnp
from jax import lax
from jax.experimental import pallas as pl
from jax.experimental.pallas import tpu as pltpu

f32 = jnp.float32
bf16 = jnp.bfloat16

D_MODEL = 1024
HEAD_DIM = 64
H_A = 8
H_B = 8
H_C = 16
D_A = H_A * HEAD_DIM
D_B = H_B * HEAD_DIM
D_C = H_C * HEAD_DIM
LORA_W = 64
LORA_A = 64
LORA_G = 128
RW_COLS = 3 * D_A + LORA_W + LORA_A + LORA_G
D_FF = 4 * D_MODEL
PAGE_SIZE = 128
DIL_PAIRS = ((128, 1), (512, 4), (2048, 16))
ROPE_THETA = 10000.0
NORM_EPS = 1e-6
LNX_EPS = 64e-5
ATTN_SCALE = HEAD_DIM ** -0.5

LANES = 128
SUBLANES = 8
VMEM_LIMIT = 56 * 1024 * 1024

EV_Q_OFF = 2048
EV_K_OFF = 2560
EV_V_OFF = 3072
EV_F_OFF = 3584
EV_COLS_PAD = 3840


def _cparams(sem):
    return pltpu.CompilerParams(dimension_semantics=sem, vmem_limit_bytes=VMEM_LIMIT)


def _split_hi_lo(x):
    hi = x.astype(bf16)
    lo = (x - hi.astype(f32)).astype(bf16)
    return hi, lo


def _seg_sum(x, bd):
    hi, lo = _split_hi_lo(x)
    return (jnp.dot(hi, bd, preferred_element_type=f32) + jnp.dot(lo, bd, preferred_element_type=f32))


def _rmsnorm_rows(x, g):
    ms = jnp.mean(x * x, axis=-1, keepdims=True)
    return x * lax.rsqrt(ms + NORM_EPS) * g


def _softplus(z):
    return jnp.maximum(z, 0.0) + jnp.log(1.0 + jnp.exp(-jnp.abs(z)))


def _ada_kernel(c_ref, w_ref, b_ref, o_ref):
    c = c_ref[...]
    s = (c * jax.nn.sigmoid(c)).astype(bf16)
    o_ref[0] = jnp.dot(s, w_ref[0].astype(bf16), preferred_element_type=f32) + b_ref[0]


def _ada_mod(c_all, ada_w, ada_b):
    n = c_all.shape[0]
    nl = ada_w.shape[0] * ada_w.shape[1]
    w = ada_w.reshape(nl, D_MODEL, 3 * D_MODEL)
    b = ada_b.reshape(nl, 1, 3 * D_MODEL)
    tn = 1024
    return pl.pallas_call(
        _ada_kernel,
        grid=(nl, 3 * D_MODEL // tn),
        in_specs=[pl.BlockSpec((n, D_MODEL), lambda l, j: (0, 0)),
                  pl.BlockSpec((1, D_MODEL, tn), lambda l, j: (l, 0, j)),
                  pl.BlockSpec((1, 1, tn), lambda l, j: (l, 0, j))],
        out_specs=pl.BlockSpec((1, n, tn), lambda l, j: (l, 0, j)),
        out_shape=jax.ShapeDtypeStruct((nl, n, 3 * D_MODEL), f32),
        compiler_params=_cparams(("parallel", "parallel")),
        name="ada_mod",
    )(c_all, w, b)


def _rope_tile(x, cos, sin_signed):
    lane = lax.broadcasted_iota(jnp.int32, (x.shape[0], LANES), 1)
    first = (lane % HEAD_DIM) < (HEAD_DIM // 2)
    outs = []
    for gidx in range(x.shape[1] // LANES):
        xg = x[:, gidx * LANES:(gidx + 1) * LANES]
        partner = jnp.where(first, pltpu.roll(xg, LANES - HEAD_DIM // 2, 1), pltpu.roll(xg, HEAD_DIM // 2, 1))
        outs.append(xg * cos + partner * sin_signed)
    return jnp.concatenate(outs, axis=1) if len(outs) > 1 else outs[0]


def _norm_mm_kernel(x_ref, sh_ref, sc_ref, g_ref, w_ref, *rest, n_rope_tiles):
    if n_rope_tiles:
        cos_ref, sin_ref, o_ref, h_scr = rest
    else:
        o_ref, h_scr = rest
    j = pl.program_id(2)

    @pl.when(j == 0)
    def _():
        h = _rmsnorm_rows(x_ref[0], g_ref[...]) * (1.0 + sc_ref[0]) + sh_ref[0]
        h_scr[...] = h.astype(bf16)

    acc = jnp.dot(h_scr[...], w_ref[...], preferred_element_type=f32)
    if n_rope_tiles:
        @pl.when(j < n_rope_tiles)
        def _():
            o_ref[0] = _rope_tile(acc, cos_ref[...], sin_ref[...])

        @pl.when(j >= n_rope_tiles)
        def _():
            o_ref[0] = acc
    else:
        o_ref[0] = acc


def _norm_matmul(x, shift, scale, g, w, *, tm, tn, rope=None, n_rope_cols=0):
    B, T, D = x.shape
    N = w.shape[1]
    per_row = shift.shape[1] != 1
    mod_spec = (pl.BlockSpec((1, tm, D), lambda b, i, j: (b, i, 0)) if per_row
                else pl.BlockSpec((1, 1, D), lambda b, i, j: (b, 0, 0)))
    in_specs = [pl.BlockSpec((1, tm, D), lambda b, i, j: (b, i, 0)), mod_spec, mod_spec,
                pl.BlockSpec((1, D), lambda b, i, j: (0, 0)),
                pl.BlockSpec((D, tn), lambda b, i, j: (0, j))]
    args = [x, shift, scale, g, w]
    n_rope_tiles = 0
    if rope is not None:
        cos, sin = rope
        n_rope_tiles = n_rope_cols // tn
        if cos.shape[0] == T:
            rspec = pl.BlockSpec((tm, LANES), lambda b, i, j: (i, 0))
        else:
            nt = T // tm
            rspec = pl.BlockSpec((tm, LANES), lambda b, i, j: (b * nt + i, 0))
        in_specs += [rspec, rspec]
        args += [cos, sin]
    return pl.pallas_call(
        functools.partial(_norm_mm_kernel, n_rope_tiles=n_rope_tiles),
        grid=(B, T // tm, N // tn),
        in_specs=in_specs,
        out_specs=pl.BlockSpec((1, tm, tn), lambda b, i, j: (b, i, j)),
        out_shape=jax.ShapeDtypeStruct((B, T, N), f32),
        scratch_shapes=[pltpu.VMEM((tm, D), bf16)],
        compiler_params=_cparams(("parallel", "parallel", "arbitrary")),
        name="norm_matmul",
    )(*args)


def _mm_post_kernel(*refs, n_in):
    a_refs = refs[:n_in]
    w_refs = refs[n_in:2 * n_in]
    x_ref, gate_ref, g_ref, o_ref = refs[2 * n_in:]
    y = None
    for a_ref, w_ref in zip(a_refs, w_refs):
        part = jnp.dot(a_ref[0].astype(bf16), w_ref[...], preferred_element_type=f32)
        y = part if y is None else y + part
    o_ref[0] = x_ref[0] + gate_ref[0] * _rmsnorm_rows(y, g_ref[...])


def _matmul_post(a_list, w_list, x, gate, g, *, tm):
    B, T, D = x.shape
    per_row = gate.shape[1] != 1
    gate_spec = (pl.BlockSpec((1, tm, D), lambda b, i: (b, i, 0)) if per_row
                 else pl.BlockSpec((1, 1, D), lambda b, i: (b, 0, 0)))
    in_specs = []
    for a in a_list:
        ka = a.shape[2]
        in_specs.append(pl.BlockSpec((1, tm, ka), lambda b, i: (b, i, 0)))
    for w in w_list:
        in_specs.append(pl.BlockSpec(w.shape, lambda b, i: (0, 0)))
    in_specs += [pl.BlockSpec((1, tm, D), lambda b, i: (b, i, 0)), gate_spec,
                 pl.BlockSpec((1, D), lambda b, i: (0, 0))]
    return pl.pallas_call(
        functools.partial(_mm_post_kernel, n_in=len(a_list)),
        grid=(B, T // tm),
        in_specs=in_specs,
        out_specs=pl.BlockSpec((1, tm, D), lambda b, i: (b, i, 0)),
        out_shape=jax.ShapeDtypeStruct((B, T, D), f32),
        compiler_params=_cparams(("parallel", "parallel")),
        name="matmul_post",
    )(*a_list, *w_list, x, gate, g)


def _mlp_kernel(x_ref, sh_ref, sc_ref, gate_ref, gpre_ref, gpost_ref, wu_ref, wd_ref, o_ref, h_scr, acc_scr):
    k = pl.program_id(2)

    @pl.when(k == 0)
    def _():
        h = _rmsnorm_rows(x_ref[0], gpre_ref[...]) * (1.0 + sc_ref[0]) + sh_ref[0]
        h_scr[...] = h.astype(bf16)
        acc_scr[...] = jnp.zeros_like(acc_scr)

    u = jnp.dot(h_scr[...], wu_ref[...], preferred_element_type=f32)
    a = jnp.square(jnp.maximum(u, 0.0)).astype(bf16)
    acc_scr[...] += jnp.dot(a, wd_ref[...], preferred_element_type=f32)

    @pl.when(k == pl.num_programs(2) - 1)
    def _():
        o_ref[0] = x_ref[0] + gate_ref[0] * _rmsnorm_rows(acc_scr[...], gpost_ref[...])


def _mlp(x, shift, scale, gate, g_pre, g_post, w_up, w_down, *, tm, tf):
    B, T, D = x.shape
    per_row = shift.shape[1] != 1
    mod_spec = (pl.BlockSpec((1, tm, D), lambda b, i, k: (b, i, 0)) if per_row
                else pl.BlockSpec((1, 1, D), lambda b, i, k: (b, 0, 0)))
    vec_spec = pl.BlockSpec((1, D), lambda b, i, k: (0, 0))
    return pl.pallas_call(
        _mlp_kernel,
        grid=(B, T // tm, D_FF // tf),
        in_specs=[pl.BlockSpec((1, tm, D), lambda b, i, k: (b, i, 0)), mod_spec, mod_spec, mod_spec,
                  vec_spec, vec_spec,
                  pl.BlockSpec((D, tf), lambda b, i, k: (0, k)),
                  pl.BlockSpec((tf, D), lambda b, i, k: (k, 0))],
        out_specs=pl.BlockSpec((1, tm, D), lambda b, i, k: (b, i, 0)),
        out_shape=jax.ShapeDtypeStruct((B, T, D), f32),
        scratch_shapes=[pltpu.VMEM((tm, D), bf16), pltpu.VMEM((tm, D), f32)],
        compiler_params=_cparams(("parallel", "parallel", "arbitrary")),
        name="mlp",
    )(x, shift, scale, gate, g_pre, g_post, w_up, w_down)


def _rwkv_pre_kernel(p_ref, sh0_ref, mu_ref, w0_ref, w2_ref, a0_ref, a2_ref, g2_ref, kk_w_ref, ka_ref,
                     rk_ref, bd_ref, r_o, w_o, k_o, v_o, kk_o, kka_o, g_o, bon_o, carry):
    ti = pl.program_id(1)

    @pl.when(ti == 0)
    def _():
        carry[...] = sh0_ref[0]

    p = p_ref[0]
    tt = p.shape[0]
    row = lax.broadcasted_iota(jnp.int32, (tt, 1), 0)
    prev = jnp.where(row == 0, carry[...], pltpu.roll(p, 1, 0))
    carry[...] = p[tt - 1:tt, :]
    ps = p + mu_ref[...] * (prev - p)
    r = ps[:, 0:D_A]
    k = ps[:, D_A:2 * D_A]
    v = ps[:, 2 * D_A:3 * D_A]
    o = 3 * D_A
    dw = ps[:, o:o + LORA_W]
    da = ps[:, o + LORA_W:o + LORA_W + LORA_A]
    dg = ps[:, o + LORA_W + LORA_A:o + LORA_W + LORA_A + LORA_G]
    bd = bd_ref[...]
    wl = w0_ref[...] + jnp.dot(jnp.tanh(dw).astype(bf16), w2_ref[...], preferred_element_type=f32)
    w_log = -_softplus(-wl) - 0.5
    decay = jnp.exp(-jnp.exp(w_log))
    a = jax.nn.sigmoid(a0_ref[...] + jnp.dot(da.astype(bf16), a2_ref[...], preferred_element_type=f32))
    g = jnp.dot(jax.nn.sigmoid(dg).astype(bf16), g2_ref[...], preferred_element_type=f32)
    kk = k * kk_w_ref[...]
    nrm = jnp.sqrt(_seg_sum(kk * kk, bd))
    kk = kk / jnp.maximum(nrm, 1e-12)
    k2 = k * (1.0 + (a - 1.0) * ka_ref[...])
    bonus = _seg_sum(r * k2 * rk_ref[...], bd) * v
    r_o[0] = r
    w_o[0] = decay
    k_o[0] = k2
    v_o[0] = v
    kk_o[0] = kk
    kka_o[0] = kk * a
    g_o[0] = g
    bon_o[0] = bonus


def _rwkv_pre(proj, shift0, mu, w0, w2, a0, a2, g2, k_k, k_a, r_k, bd512, *, tt):
    B, T, _ = proj.shape
    vec = lambda n: pl.BlockSpec((1, n), lambda b, i: (0, 0))
    mat = lambda s: pl.BlockSpec(s, lambda b, i: (0, 0))
    out_spec = pl.BlockSpec((1, tt, D_A), lambda b, i: (b, i, 0))
    out_sds = jax.ShapeDtypeStruct((B, T, D_A), f32)
    return pl.pallas_call(
        _rwkv_pre_kernel,
        grid=(B, T // tt),
        in_specs=[pl.BlockSpec((1, tt, RW_COLS), lambda b, i: (b, i, 0)),
                  pl.BlockSpec((1, 1, RW_COLS), lambda b, i: (b, 0, 0)),
                  vec(RW_COLS), vec(D_A), mat((LORA_W, D_A)), vec(D_A), mat((LORA_A, D_A)),
                  mat((LORA_G, D_A)), vec(D_A), vec(D_A), vec(D_A), mat((D_A, D_A))],
        out_specs=[out_spec] * 8,
        out_shape=[out_sds] * 8,
        scratch_shapes=[pltpu.VMEM((1, RW_COLS), f32)],
        compiler_params=_cparams(("parallel", "arbitrary")),
        name="rwkv_pre",
    )(proj, shift0, mu, w0, w2, a0, a2, g2, k_k, k_a, r_k, bd512)


SCAN_NB = 8
PAIRS = H_A // 2


def _wkv_scan_kernel(r_ref, w_ref, k_ref, v_ref, kk_ref, kka_ref, s0_ref, bd_ref, e_ref,
                     y_ref, st_ref, s_scr):
    c = pl.program_id(1)
    n_steps = r_ref.shape[1]
    rows_per_b = PAIRS * HEAD_DIM

    @pl.when(c == 0)
    def _():
        s_scr[...] = s0_ref[...]

    bd = bd_ref[...]
    e4 = e_ref[...]

    def step8(t8, carry):
        t0 = pl.multiple_of(t8 * SUBLANES, SUBLANES)
        for b in range(SCAN_NB):
            blk = {name: ref[b, pl.ds(t0, SUBLANES), :]
                   for name, ref in (("r", r_ref), ("w", w_ref), ("k", k_ref), ("v", v_ref),
                                     ("kk", kk_ref), ("kka", kka_ref))}
            s = s_scr[b * rows_per_b:(b + 1) * rows_per_b, :]
            yrows = []
            for j in range(SUBLANES):
                def rows(name):
                    parts = [jnp.broadcast_to(blk[name][j:j + 1, LANES * p:LANES * (p + 1)], (HEAD_DIM, LANES))
                             for p in range(PAIRS)]
                    return jnp.concatenate(parts, axis=0)

                sk = _seg_sum(s * rows("kk"), bd)
                vcol = _seg_sum(e4 * rows("v"), bd)
                s = s * rows("w") - sk * rows("kka") + vcol * rows("k")
                yrep = _seg_sum(s * rows("r"), bd)
                yrow = jnp.sum((e4 * yrep).reshape(PAIRS, HEAD_DIM, LANES), axis=1)
                yrows.append(jnp.concatenate([yrow[p:p + 1, :] for p in range(PAIRS)], axis=1))
            s_scr[b * rows_per_b:(b + 1) * rows_per_b, :] = s
            y_ref[b, pl.ds(t0, SUBLANES), :] = jnp.concatenate(yrows, axis=0)
        return carry

    lax.fori_loop(0, n_steps // SUBLANES, step8, 0)

    @pl.when(c == pl.num_programs(1) - 1)
    def _():
        st_ref[...] = s_scr[...]


def _wkv_scan(r, w, k, v, kk, kka, s0_pairs, bd128, e4, *, chunk):
    NB, T, _ = r.shape
    rows = SCAN_NB * PAIRS * HEAD_DIM
    xspec = pl.BlockSpec((SCAN_NB, chunk, D_A), lambda g, c: (g, c, 0))
    sspec = pl.BlockSpec((rows, LANES), lambda g, c: (g, 0))
    return pl.pallas_call(
        _wkv_scan_kernel,
        grid=(NB // SCAN_NB, T // chunk),
        in_specs=[xspec] * 6 + [sspec,
                                pl.BlockSpec((LANES, LANES), lambda g, c: (0, 0)),
                                pl.BlockSpec((PAIRS * HEAD_DIM, LANES), lambda g, c: (0, 0))],
        out_specs=[xspec, sspec],
        out_shape=[jax.ShapeDtypeStruct((NB, T, D_A), f32),
                   jax.ShapeDtypeStruct((NB * PAIRS * HEAD_DIM, LANES), f32)],
        scratch_shapes=[pltpu.VMEM((rows, LANES), f32)],
        compiler_params=_cparams(("parallel", "arbitrary")),
        name="wkv_scan",
    )(r, w, k, v, kk, kka, s0_pairs, bd128, e4)


def _rwkv_post_kernel(y_ref, g_ref, bon_ref, lw_ref, lb_ref, bd_ref, o_ref):
    y = y_ref[0]
    bd = bd_ref[...]
    mean = _seg_sum(y, bd) * (1.0 / HEAD_DIM)
    d = y - mean
    var = _seg_sum(d * d, bd) * (1.0 / HEAD_DIM)
    yn = d * lax.rsqrt(var + LNX_EPS) * lw_ref[...] + lb_ref[...]
    o_ref[0] = (yn + bon_ref[0]) * g_ref[0]


def _rwkv_post(y, g, bonus, lnx_w, lnx_b, bd512, *, tt):
    B, T, _ = y.shape
    spec = pl.BlockSpec((1, tt, D_A), lambda b, i: (b, i, 0))
    vec = pl.BlockSpec((1, D_A), lambda b, i: (0, 0))
    return pl.pallas_call(
        _rwkv_post_kernel,
        grid=(B, T // tt),
        in_specs=[spec, spec, spec, vec, vec, pl.BlockSpec((D_A, D_A), lambda b, i: (0, 0))],
        out_specs=spec,
        out_shape=jax.ShapeDtypeStruct((B, T, D_A), f32),
        compiler_params=_cparams(("parallel", "parallel")),
        name="rwkv_post",
    )(y, g, bonus, lnx_w, lnx_b, bd512)


def _fox_gate_kernel(f_ref, b_ref, lf_ref, cum_ref):
    z = f_ref[0] + b_ref[...]
    lf = -_softplus(-z)
    lf_ref[0] = lf
    T = lf.shape[0]
    row = lax.broadcasted_iota(jnp.int32, (T, 1), 0)
    x = lf
    s = 1
    while s < T:
        x = x + jnp.where(row >= s, pltpu.roll(x, s, 0), 0.0)
        s *= 2
    cum_ref[0] = x


def _fox_gate(proj, b_f_pad):
    B, T, _ = proj.shape
    spec = pl.BlockSpec((1, T, LANES), lambda b: (b, 0, 0))
    sds = jax.ShapeDtypeStruct((B, T, LANES), f32)
    return pl.pallas_call(
        _fox_gate_kernel,
        grid=(B,),
        in_specs=[pl.BlockSpec((1, T, LANES), lambda b: (b, 0, EV_F_OFF // LANES)),
                  pl.BlockSpec((1, LANES), lambda b: (0, 0))],
        out_specs=[spec, spec],
        out_shape=[sds, sds],
        compiler_params=_cparams(("parallel",)),
        name="fox_gate",
    )(proj, b_f_pad)


def _flash_kernel(q_ref, k_ref, v_ref, c_ref, *rest, n_heads, tq, tk, has_bias):
    if has_bias:
        cq_ref, ckt_ref, o_ref, m_scr, l_scr, acc_scr = rest
    else:
        o_ref, m_scr, l_scr, acc_scr = rest
    qi = pl.program_id(1)
    m_scr[...] = jnp.full_like(m_scr, -jnp.inf)
    l_scr[...] = jnp.zeros_like(l_scr)
    acc_scr[...] = jnp.zeros_like(acc_scr)
    ratio = tq // tk

    def body(it, carry):
        kj = qi * ratio + (ratio - 1) - it
        dist = it
        cm = c_ref[dist]
        valid = cm > 0.0
        k0 = pl.multiple_of(kj * tk, tk)
        for h in range(n_heads):
            lo, hi = h * HEAD_DIM, (h + 1) * HEAD_DIM
            q_h = (q_ref[0, :, lo:hi] * ATTN_SCALE).astype(bf16)
            k_h = k_ref[0, pl.ds(k0, tk), lo:hi].astype(bf16)
            v_h = v_ref[0, pl.ds(k0, tk), lo:hi].astype(bf16)
            s = lax.dot_general(q_h, k_h, (((1,), (1,)), ((), ())), preferred_element_type=f32)
            if has_bias:
                s = s + cq_ref[0, :, h:h + 1] - ckt_ref[0, h:h + 1, pl.ds(k0, tk)]
            s = jnp.where(valid, s, -jnp.inf)
            m_prev = m_scr[h]
            m_next = jnp.maximum(m_prev, jnp.max(s, axis=1, keepdims=True))
            p = jnp.exp(s - m_next[:, 0:1]) * cm
            alpha = jnp.exp(m_prev - m_next)
            l_scr[h] = alpha * l_scr[h] + jnp.sum(p, axis=1, keepdims=True)
            m_scr[h] = m_next
            pv = jnp.dot(p.astype(bf16), v_h, preferred_element_type=f32)
            acc_scr[:, lo:hi] = acc_scr[:, lo:hi] * alpha[:, 0:HEAD_DIM] + pv
        return carry

    lax.fori_loop(0, (qi + 1) * ratio, body, 0)
    for h in range(n_heads):
        lo, hi = h * HEAD_DIM, (h + 1) * HEAD_DIM
        o_ref[0, :, lo:hi] = acc_scr[:, lo:hi] / l_scr[h][:, 0:HEAD_DIM]


def _flash_attention(q_src, k_src, v_src, q_blk, k_blk, v_blk, ctab, n_heads, *, tq, tk, bias=None):
    B, T, _ = q_src.shape
    hd = n_heads * HEAD_DIM
    in_specs = [pl.BlockSpec((1, tq, hd), lambda b, i: (b, i, q_blk)),
                pl.BlockSpec((1, T, hd), lambda b, i: (b, 0, k_blk)),
                pl.BlockSpec((1, T, hd), lambda b, i: (b, 0, v_blk)),
                pl.BlockSpec(ctab.shape, lambda b, i: (0, 0, 0))]
    args = [q_src, k_src, v_src, ctab]
    if bias is not None:
        cum, cum_t = bias
        in_specs += [pl.BlockSpec((1, tq, LANES), lambda b, i: (b, i, 0)),
                     pl.BlockSpec((1, cum_t.shape[1], T), lambda b, i: (b, 0, 0))]
        args += [cum, cum_t]
    return pl.pallas_call(
        functools.partial(_flash_kernel, n_heads=n_heads, tq=tq, tk=tk, has_bias=bias is not None),
        grid=(B, T // tq),
        in_specs=in_specs,
        out_specs=pl.BlockSpec((1, tq, hd), lambda b, i: (b, i, 0)),
        out_shape=jax.ShapeDtypeStruct((B, T, hd), f32),
        scratch_shapes=[pltpu.VMEM((n_heads, tq, LANES), f32), pltpu.VMEM((n_heads, tq, LANES), f32),
                        pltpu.VMEM((tq, hd), f32)],
        compiler_params=_cparams(("parallel", "arbitrary")),
        name="flash_attention",
    )(*args)


def _online_update(h, s, cm, v_h, m_scr, l_scr, acc_scr, v_transposed=False):
    m_prev = m_scr[h]
    m_next = jnp.maximum(m_prev, jnp.max(s, axis=1, keepdims=True))
    p = jnp.exp(s - m_next[:, 0:1])
    if cm is not None:
        p = p * cm
    alpha = jnp.exp(m_prev - m_next)
    l_scr[h] = alpha * l_scr[h] + jnp.sum(p, axis=1, keepdims=True)
    m_scr[h] = m_next
    if v_transposed:
        pv = _qk(p.astype(bf16), v_h)
    else:
        pv = jnp.dot(p.astype(bf16), v_h, preferred_element_type=f32)
    acc_scr[h] = acc_scr[h] * alpha[:, 0:HEAD_DIM] + pv


def _qk(q_h, k_h):
    return lax.dot_general(q_h, k_h, (((1,), (1,)), ((), ())), preferred_element_type=f32)


FOX_PP = 8


def _fox_decode_kernel(pt_ref, q_ref, kn_ref, vn_ref, cq_ref, cnt_ref, *rest):
    lf_refs = rest[:FOX_PP]
    k_refs = rest[FOX_PP:2 * FOX_PP]
    v_refs = rest[2 * FOX_PP:3 * FOX_PP]
    o_ref, m_scr, l_scr, acc_scr, tot_scr = rest[3 * FOX_PP:]
    j = pl.program_id(1)
    nq = q_ref.shape[1]

    @pl.when(j == 0)
    def _():
        m_scr[...] = jnp.full_like(m_scr, -jnp.inf)
        l_scr[...] = jnp.zeros_like(l_scr)
        acc_scr[...] = jnp.zeros_like(acc_scr)
        tot_scr[...] = jnp.zeros_like(tot_scr)
        qi = lax.broadcasted_iota(jnp.int32, (nq, nq), 0)
        kj = lax.broadcasted_iota(jnp.int32, (nq, nq), 1)
        for h in range(H_B):
            lo, hi = h * HEAD_DIM, (h + 1) * HEAD_DIM
            q_h = (q_ref[0, :, lo:hi] * ATTN_SCALE).astype(bf16)
            s = _qk(q_h, kn_ref[0, :, lo:hi].astype(bf16))
            s = s + cq_ref[0, :, h:h + 1] - cnt_ref[0, h:h + 1, 0:nq]
            s = jnp.where(kj <= qi, s, -jnp.inf)
            _online_update(h, s, None, vn_ref[0, :, lo:hi].astype(bf16), m_scr, l_scr, acc_scr)

    lane = lax.broadcasted_iota(jnp.int32, (H_B, PAGE_SIZE), 1)
    biases = []
    tot = tot_scr[...]
    for i in range(FOX_PP):
        lf = lf_refs[i][0]
        x = lf
        sft = 1
        while sft < PAGE_SIZE:
            x = x + jnp.where(lane >= sft, pltpu.roll(x, sft, 1), 0.0)
            sft *= 2
        page_tot = x[:, PAGE_SIZE - 1:PAGE_SIZE]
        biases.append(page_tot - x + tot)
        tot = tot + page_tot
    tot_scr[...] = tot
    for h in range(H_B):
        lo, hi = h * HEAD_DIM, (h + 1) * HEAD_DIM
        q_h = (q_ref[0, :, lo:hi] * ATTN_SCALE).astype(bf16)
        s_parts = []
        v_parts = []
        for i in range(FOX_PP):
            kt_h = k_refs[i][0, h].astype(bf16)
            s_parts.append(jnp.dot(q_h, kt_h, preferred_element_type=f32) + biases[i][h:h + 1, :])
            v_parts.append(v_refs[i][0, h].astype(bf16))
        s = jnp.concatenate(s_parts, axis=1) + cq_ref[0, :, h:h + 1]
        vt_h = jnp.concatenate(v_parts, axis=1)
        _online_update(h, s, None, vt_h, m_scr, l_scr, acc_scr, v_transposed=True)

    @pl.when(j == pl.num_programs(1) - 1)
    def _():
        for h in range(H_B):
            o_ref[0, :, h * HEAD_DIM:(h + 1) * HEAD_DIM] = acc_scr[h] / l_scr[h][:, 0:HEAD_DIM]


def _fox_decode(page_table, proj, cum_new, cum_new_t, lf_cache_t, k_cache, v_cache):
    NB, nq, _ = proj.shape
    n_pages = page_table.shape[1]
    steps = n_pages // FOX_PP

    def lf_spec(i):
        return pl.BlockSpec((1, H_B, PAGE_SIZE),
                            lambda b, j, pt: (pt[b, n_pages - 1 - (j * FOX_PP + i)], 0, 0))

    def kv_spec(i):
        return pl.BlockSpec((1, H_B, HEAD_DIM, PAGE_SIZE),
                            lambda b, j, pt: (pt[b, n_pages - 1 - (j * FOX_PP + i)], 0, 0, 0))

    in_specs = [pl.BlockSpec((1, nq, D_B), lambda b, j, pt: (b, 0, EV_Q_OFF // D_B)),
                pl.BlockSpec((1, nq, D_B), lambda b, j, pt: (b, 0, EV_K_OFF // D_B)),
                pl.BlockSpec((1, nq, D_B), lambda b, j, pt: (b, 0, EV_V_OFF // D_B)),
                pl.BlockSpec((1, nq, LANES), lambda b, j, pt: (b, 0, 0)),
                pl.BlockSpec((1, SUBLANES, LANES), lambda b, j, pt: (b, 0, 0))]
    args = [proj, proj, proj, cum_new, cum_new_t]
    for i in range(FOX_PP):
        in_specs.append(lf_spec(i))
        args.append(lf_cache_t)
    for cache in (k_cache, v_cache):
        for i in range(FOX_PP):
            in_specs.append(kv_spec(i))
            args.append(cache)
    grid_spec = pltpu.PrefetchScalarGridSpec(
        num_scalar_prefetch=1,
        grid=(NB, steps),
        in_specs=in_specs,
        out_specs=pl.BlockSpec((1, nq, D_B), lambda b, j, pt: (b, 0, 0)),
        scratch_shapes=[pltpu.VMEM((H_B, nq, LANES), f32), pltpu.VMEM((H_B, nq, LANES), f32),
                        pltpu.VMEM((H_B, nq, HEAD_DIM), f32), pltpu.VMEM((H_B, LANES), f32)],
    )
    return pl.pallas_call(
        _fox_decode_kernel,
        grid_spec=grid_spec,
        out_shape=jax.ShapeDtypeStruct((NB, nq, D_B), f32),
        compiler_params=_cparams(("parallel", "arbitrary")),
        name="fox_decode",
    )(page_table, *args)


DIL_KC = 512


def _dil_decode_kernel(q_ref, kn_ref, vn_ref, cnew_ref, cpast_ref, k_ref, v_ref, o_ref, m_scr, l_scr, acc_scr):
    j = pl.program_id(1)
    nq = q_ref.shape[1]

    @pl.when(j == 0)
    def _():
        m_scr[...] = jnp.full_like(m_scr, -jnp.inf)
        l_scr[...] = jnp.zeros_like(l_scr)
        acc_scr[...] = jnp.zeros_like(acc_scr)
        cm = cnew_ref[:, 0:nq]
        valid = cm > 0.0
        for h in range(H_C):
            lo, hi = h * HEAD_DIM, (h + 1) * HEAD_DIM
            q_h = (q_ref[0, :, lo:hi] * ATTN_SCALE).astype(bf16)
            s = jnp.where(valid, _qk(q_h, kn_ref[0, :, lo:hi].astype(bf16)), -jnp.inf)
            _online_update(h, s, cm, vn_ref[0, :, lo:hi].astype(bf16), m_scr, l_scr, acc_scr)

    cm = cpast_ref[...]
    valid = cm > 0.0
    for h in range(H_C):
        lo, hi = h * HEAD_DIM, (h + 1) * HEAD_DIM
        q_h = (q_ref[0, :, lo:hi] * ATTN_SCALE).astype(bf16)
        kt_h = k_ref[0, h].astype(bf16)
        s = jnp.where(valid, jnp.dot(q_h, kt_h, preferred_element_type=f32), -jnp.inf)
        _online_update(h, s, cm, v_ref[0, h].astype(bf16), m_scr, l_scr, acc_scr, v_transposed=True)

    @pl.when(j == pl.num_programs(1) - 1)
    def _():
        for h in range(H_C):
            o_ref[0, :, h * HEAD_DIM:(h + 1) * HEAD_DIM] = acc_scr[h] / l_scr[h][:, 0:HEAD_DIM]


def _dil_decode(qkv, c_new, c_past, k_state, v_state):
    NB, nq, _ = qkv.shape
    L = k_state.shape[3]
    kv_spec = pl.BlockSpec((1, H_C, HEAD_DIM, DIL_KC), lambda b, j: (b, 0, 0, j))
    return pl.pallas_call(
        _dil_decode_kernel,
        grid=(NB, L // DIL_KC),
        in_specs=[pl.BlockSpec((1, nq, D_C), lambda b, j: (b, 0, 0)),
                  pl.BlockSpec((1, nq, D_C), lambda b, j: (b, 0, 1)),
                  pl.BlockSpec((1, nq, D_C), lambda b, j: (b, 0, 2)),
                  pl.BlockSpec(c_new.shape, lambda b, j: (0, 0)),
                  pl.BlockSpec((nq, DIL_KC), lambda b, j: (0, j)),
                  kv_spec, kv_spec],
        out_specs=pl.BlockSpec((1, nq, D_C), lambda b, j: (b, 0, 0)),
        out_shape=jax.ShapeDtypeStruct((NB, nq, D_C), f32),
        scratch_shapes=[pltpu.VMEM((H_C, nq, LANES), f32), pltpu.VMEM((H_C, nq, LANES), f32),
                        pltpu.VMEM((H_C, nq, HEAD_DIM), f32)],
        compiler_params=_cparams(("parallel", "arbitrary")),
        name="dil_decode",
    )(qkv, qkv, qkv, c_new, c_past, k_state, v_state)


def _dil_multiplicity(delta):
    delta = np.asarray(delta)
    c = np.zeros(delta.shape, np.float32)
    for w, d in DIL_PAIRS:
        c += ((delta >= 0) & (delta % d == 0) & (delta <= w)).astype(np.float32)
    return c


def _block_tables(T, tq, tk, mult_fn):
    nd = T // tk
    i = np.arange(tq)[:, None]
    j = np.arange(tk)[None, :]
    ratio = tq // tk
    tabs = [mult_fn(i - j + (it - (ratio - 1)) * tk) for it in range(nd)]
    return jnp.asarray(np.stack(tabs))


def _rope_tables(pos):
    half = HEAD_DIM // 2
    inv = ROPE_THETA ** (-jnp.arange(half, dtype=f32) / half)
    ang = pos.astype(f32)[:, None] * inv[None, :]
    cos = jnp.cos(ang)
    sin = jnp.sin(ang)
    return jnp.tile(cos, (1, 4)), jnp.concatenate([-sin, sin, -sin, sin], axis=1)


def _pack_state(s):
    NB = s.shape[0]
    return s.reshape(NB, PAIRS, 2, HEAD_DIM, HEAD_DIM).transpose(0, 1, 3, 2, 4).reshape(NB * PAIRS * HEAD_DIM, LANES)


def _unpack_state(sp, NB):
    return sp.reshape(NB, PAIRS, HEAD_DIM, 2, HEAD_DIM).transpose(0, 1, 3, 2, 4).reshape(NB, H_A, HEAD_DIM, HEAD_DIM)


def _mods(mod_lj, n_prompt, per_row_repeat):
    mp = mod_lj[:n_prompt][:, None, :]
    ms = jnp.repeat(mod_lj[n_prompt:], per_row_repeat, axis=0)[None]
    sp = [mp[..., i * D_MODEL:(i + 1) * D_MODEL] for i in range(3)]
    ss = [ms[..., i * D_MODEL:(i + 1) * D_MODEL] for i in range(3)]
    return sp, ss


def kernel(x_prompt, x_sample, cache_fox_k, cache_fox_v, cache_fox_logf, state_rwkv_wkv, state_rwkv_shift,
           state_dil_k, state_dil_v, page_table, c_prompt, c_sample, ada_w, ada_b, norm_pre, norm_post,
           mlp_up, mlp_down, ev_w_in, ev_w_out, rw_mu, rw_w0, rw_w2, rw_a0, rw_a2, rw_g2, rw_k_k, rw_k_a,
           rw_r_k, rw_lnx_w, rw_lnx_b, fox_b_f, od_w_in, od_w_out):
    BP, T, D = x_prompt.shape
    BS, TS, _ = x_sample.shape
    depth = ada_w.shape[0]
    past_len = page_table.shape[1] * PAGE_SIZE

    mod = _ada_mod(jnp.concatenate([c_prompt, c_sample], axis=0), ada_w, ada_b)

    idx = np.arange(LANES)
    bd128 = jnp.asarray((idx[:, None] // HEAD_DIM == idx[None, :] // HEAD_DIM).astype(np.float32)).astype(bf16)
    idx5 = np.arange(D_A)
    bd512 = jnp.asarray((idx5[:, None] // HEAD_DIM == idx5[None, :] // HEAD_DIM).astype(np.float32)).astype(bf16)
    e4 = jnp.asarray(np.tile((idx[None, :] % HEAD_DIM == np.arange(HEAD_DIM)[:, None]).astype(np.float32),
                             (PAIRS, 1)))
    TQ = 256
    TK = 256
    causal_tab = _block_tables(T, TQ, TK, lambda dl: (dl >= 0).astype(np.float32))
    dil_tab = _block_tables(T, TQ, TK, _dil_multiplicity)

    xp = x_prompt
    xs = x_sample.reshape(1, BS * TS, D)
    outs = {}
    for l in range(depth):
        i = l // 2
        (sh_p, sc_p, gt_p), (sh_s, sc_s, gt_s) = _mods(mod[2 * l], BP, TS)
        g_pre = norm_pre[l, 0][None]
        g_post = norm_post[l, 0][None]
        if l % 2 == 0:
            w_in = ev_w_in[i]
            zeros = lambda n: jnp.zeros((D, n), f32)
            w_pad = jnp.concatenate([w_in[:, :RW_COLS], zeros(EV_Q_OFF - RW_COLS), w_in[:, RW_COLS:],
                                     zeros(EV_COLS_PAD - EV_F_OFF - H_B)], axis=1).astype(bf16)
            w_out = ev_w_out[i].astype(bf16)
            b_f_pad = jnp.pad(fox_b_f[i], (0, LANES - H_B))[None]
            rw_args = (rw_mu[i][None], rw_w0[i][None], rw_w2[i].astype(bf16), rw_a0[i][None],
                       rw_a2[i].astype(bf16), rw_g2[i].astype(bf16), rw_k_k[i][None], rw_k_a[i][None],
                       rw_r_k[i].reshape(1, D_A), bd512)

            proj_p = _norm_matmul(xp, sh_p, sc_p, g_pre, w_pad, tm=512, tn=256)
            r, w, k2, v, kk, kka, g, bonus = _rwkv_pre(proj_p, jnp.zeros((BP, 1, RW_COLS), f32), *rw_args, tt=256)
            y, st = _wkv_scan(r, w, k2, v, kk, kka, jnp.zeros((BP * PAIRS * HEAD_DIM, LANES), f32), bd128, e4,
                              chunk=64)
            ya = _rwkv_post(y, g, bonus, rw_lnx_w[i][None], rw_lnx_b[i][None], bd512, tt=512)
            lf_p, cum_p = _fox_gate(proj_p, b_f_pad)
            cum_t = jnp.transpose(cum_p[:, :, :SUBLANES], (0, 2, 1))
            yb = _flash_attention(proj_p, proj_p, proj_p, EV_Q_OFF // D_B, EV_K_OFF // D_B, EV_V_OFF // D_B,
                                  causal_tab, H_B, tq=TQ, tk=TK, bias=(cum_p, cum_t))
            xp = _matmul_post([ya, yb], [w_out[:D_A], w_out[D_A:]], xp, gt_p, g_post, tm=512)
            outs.setdefault("fk_p", []).append(proj_p[:, :, EV_K_OFF:EV_K_OFF + D_B].reshape(BP, T, H_B, HEAD_DIM))
            outs.setdefault("fv_p", []).append(proj_p[:, :, EV_V_OFF:EV_V_OFF + D_B].reshape(BP, T, H_B, HEAD_DIM))
            outs.setdefault("fl_p", []).append(lf_p[:, :, :H_B])
            outs.setdefault("wkv_p", []).append(_unpack_state(st, BP))
            outs.setdefault("sh_p", []).append(proj_p[:, T - 1, :RW_COLS])

            proj_s = _norm_matmul(xs, sh_s, sc_s, g_pre, w_pad, tm=BS * TS, tn=256).reshape(BS, TS, EV_COLS_PAD)
            r, w, k2, v, kk, kka, g, bonus = _rwkv_pre(proj_s, state_rwkv_shift[i][:, None, :], *rw_args, tt=TS)
            y, st = _wkv_scan(r, w, k2, v, kk, kka, _pack_state(state_rwkv_wkv[i]), bd128, e4, chunk=TS)
            ya = _rwkv_post(y, g, bonus, rw_lnx_w[i][None], rw_lnx_b[i][None], bd512, tt=TS)
            lf_s, cum_s = _fox_gate(proj_s, b_f_pad)
            cum_s_t = jnp.pad(jnp.transpose(cum_s[:, :, :SUBLANES], (0, 2, 1)), ((0, 0), (0, 0), (0, LANES - TS)))
            lf_cache_t = jnp.transpose(cache_fox_logf[i], (0, 2, 1))
            yb = _fox_decode(page_table, proj_s, cum_s, cum_s_t, lf_cache_t,
                             jnp.transpose(cache_fox_k[i], (0, 2, 3, 1)),
                             jnp.transpose(cache_fox_v[i], (0, 2, 3, 1)))
            xs = _matmul_post([ya.reshape(1, BS * TS, D_A), yb.reshape(1, BS * TS, D_B)],
                              [w_out[:D_A], w_out[D_A:]], xs, gt_s, g_post, tm=BS * TS)
            outs.setdefault("fk_s", []).append(proj_s[:, :, EV_K_OFF:EV_K_OFF + D_B].reshape(BS, TS, H_B, HEAD_DIM))
            outs.setdefault("fv_s", []).append(proj_s[:, :, EV_V_OFF:EV_V_OFF + D_B].reshape(BS, TS, H_B, HEAD_DIM))
            outs.setdefault("fl_s", []).append(lf_s[:, :, :H_B])
            outs.setdefault("wkv_s", []).append(_unpack_state(st, BS))
            outs.setdefault("sh_s", []).append(proj_s[:, TS - 1, :RW_COLS])
        else:
            w_in = od_w_in[i].astype(bf16)
            w_out = od_w_out[i].astype(bf16)
            rope_p = _rope_tables(jnp.arange(T))
            cos_s, sin_s = _rope_tables(past_len + jnp.arange(TS))
            rope_s = (jnp.tile(cos_s, (BS, 1)), jnp.tile(sin_s, (BS, 1)))

            qkv_p = _norm_matmul(xp, sh_p, sc_p, g_pre, w_in, tm=512, tn=256, rope=rope_p, n_rope_cols=2 * D_C)
            o_p = _flash_attention(qkv_p, qkv_p, qkv_p, 0, 1, 2, dil_tab, H_C, tq=TQ, tk=TK)
            xp = _matmul_post([o_p], [w_out], xp, gt_p, g_post, tm=512)
            keep = min(DIL_PAIRS[-1][0], T)
            outs.setdefault("dk_p", []).append(qkv_p[:, T - keep:, D_C:2 * D_C].reshape(BP, keep, H_C, HEAD_DIM))
            outs.setdefault("dv_p", []).append(qkv_p[:, T - keep:, 2 * D_C:].reshape(BP, keep, H_C, HEAD_DIM))

            qkv_s = _norm_matmul(xs, sh_s, sc_s, g_pre, w_in, tm=BS * TS, tn=256, rope=rope_s,
                                 n_rope_cols=2 * D_C).reshape(BS, TS, 3 * D_C)
            L = state_dil_k.shape[2]
            tq_idx = np.arange(TS)[:, None]
            c_new = np.zeros((TS, LANES), np.float32)
            c_new[:, :TS] = _dil_multiplicity(tq_idx - np.arange(TS)[None, :])
            c_past = _dil_multiplicity(L + tq_idx - np.arange(L)[None, :])
            o_s = _dil_decode(qkv_s, jnp.asarray(c_new), jnp.asarray(c_past),
                              jnp.transpose(state_dil_k[i], (0, 2, 3, 1)),
                              jnp.transpose(state_dil_v[i], (0, 2, 3, 1)))
            xs = _matmul_post([o_s.reshape(1, BS * TS, D_C)], [w_out], xs, gt_s, g_post, tm=BS * TS)
            outs.setdefault("dk_s", []).append(qkv_s[:, :, D_C:2 * D_C].reshape(BS, TS, H_C, HEAD_DIM))
            outs.setdefault("dv_s", []).append(qkv_s[:, :, 2 * D_C:].reshape(BS, TS, H_C, HEAD_DIM))

        (sh_p, sc_p, gt_p), (sh_s, sc_s, gt_s) = _mods(mod[2 * l + 1], BP, TS)
        g_pre = norm_pre[l, 1][None]
        g_post = norm_post[l, 1][None]
        w_up = mlp_up[l].astype(bf16)
        w_down = mlp_down[l].astype(bf16)
        xp = _mlp(xp, sh_p, sc_p, gt_p, g_pre, g_post, w_up, w_down, tm=512, tf=512)
        xs = _mlp(xs, sh_s, sc_s, gt_s, g_pre, g_post, w_up, w_down, tm=BS * TS, tf=512)

    st = lambda name: jnp.stack(outs[name])
    return (xp, xs.reshape(BS, TS, D), st("fk_p"), st("fv_p"), st("fl_p"), st("fk_s"), st("fv_s"), st("fl_s"),
            st("wkv_p"), st("wkv_s"), st("sh_p"), st("sh_s"), st("dk_p"), st("dv_p"), st("dk_s"), st("dv_s"))
```

```python
import functools

import numpy as np
import jax
import jax.numpy as jnp
from jax import lax
from jax.experimental import pallas as pl
from jax.experimental.pallas import tpu as pltpu

f32 = jnp.float32
bf16 = jnp.bfloat16

D_MODEL = 1024
HEAD_DIM = 64
H_A = 8
H_B = 8
H_C = 16
D_A = H_A * HEAD_DIM
D_B = H_B * HEAD_DIM
D_C = H_C * HEAD_DIM
LORA_W = 64
LORA_A = 64
LORA_G = 128
RW_COLS = 3 * D_A + LORA_W + LORA_A + LORA_G
D_FF = 4 * D_MODEL
PAGE_SIZE = 128
DIL_PAIRS = ((128, 1), (512, 4), (2048, 16))
ROPE_THETA = 10000.0
NORM_EPS = 1e-6
LNX_EPS = 64e-5
ATTN_SCALE = HEAD_DIM ** -0.5
LOG2E = 1.4426950408889634

LANES = 128
SUBLANES = 8
VMEM_LIMIT = 56 * 1024 * 1024

EV_Q_OFF = 2048
EV_K_OFF = 2560
EV_V_OFF = 3072
EV_F_OFF = 3584
EV_COLS_PAD = 3840


def _cparams(sem):
    return pltpu.CompilerParams(dimension_semantics=sem, vmem_limit_bytes=VMEM_LIMIT)


def _split_hi_lo(x):
    hi = x.astype(bf16)
    lo = (x - hi.astype(f32)).astype(bf16)
    return hi, lo


def _seg_sum(x, bd):
    hi, lo = _split_hi_lo(x)
    return (jnp.dot(hi, bd, preferred_element_type=f32) + jnp.dot(lo, bd, preferred_element_type=f32))


def _rmsnorm_rows(x, g):
    ms = jnp.mean(x * x, axis=-1, keepdims=True)
    return x * lax.rsqrt(ms + NORM_EPS) * g


def _softplus(z):
    return jnp.maximum(z, 0.0) + jnp.log(1.0 + jnp.exp(-jnp.abs(z)))


def _ada_kernel(c_ref, w_ref, b_ref, o_ref):
    c = c_ref[...]
    s = (c * jax.nn.sigmoid(c)).astype(bf16)
    o_ref[0] = jnp.dot(s, w_ref[0].astype(bf16), preferred_element_type=f32) + b_ref[0]


def _ada_mod(c_all, ada_w, ada_b):
    n = c_all.shape[0]
    nl = ada_w.shape[0] * ada_w.shape[1]
    w = ada_w.reshape(nl, D_MODEL, 3 * D_MODEL)
    b = ada_b.reshape(nl, 1, 3 * D_MODEL)
    tn = 1024
    return pl.pallas_call(
        _ada_kernel,
        grid=(nl, 3 * D_MODEL // tn),
        in_specs=[pl.BlockSpec((n, D_MODEL), lambda l, j: (0, 0)),
                  pl.BlockSpec((1, D_MODEL, tn), lambda l, j: (l, 0, j)),
                  pl.BlockSpec((1, 1, tn), lambda l, j: (l, 0, j))],
        out_specs=pl.BlockSpec((1, n, tn), lambda l, j: (l, 0, j)),
        out_shape=jax.ShapeDtypeStruct((nl, n, 3 * D_MODEL), f32),
        compiler_params=_cparams(("parallel", "parallel")),
        name="ada_mod",
    )(c_all, w, b)


def _rope_tile(x, cos, sin_signed):
    lane = lax.broadcasted_iota(jnp.int32, (x.shape[0], LANES), 1)
    first = (lane % HEAD_DIM) < (HEAD_DIM // 2)
    outs = []
    for gidx in range(x.shape[1] // LANES):
        xg = x[:, gidx * LANES:(gidx + 1) * LANES]
        partner = jnp.where(first, pltpu.roll(xg, LANES - HEAD_DIM // 2, 1), pltpu.roll(xg, HEAD_DIM // 2, 1))
        outs.append(xg * cos + partner * sin_signed)
    return jnp.concatenate(outs, axis=1) if len(outs) > 1 else outs[0]


def _norm_mm_kernel(x_ref, sh_ref, sc_ref, g_ref, w_ref, *rest, n_rope_tiles):
    if n_rope_tiles:
        cos_ref, sin_ref, o_ref, h_scr = rest
    else:
        o_ref, h_scr = rest
    j = pl.program_id(2)

    @pl.when(j == 0)
    def _():
        h = _rmsnorm_rows(x_ref[0], g_ref[...]) * (1.0 + sc_ref[0]) + sh_ref[0]
        h_scr[...] = h.astype(bf16)

    acc = jnp.dot(h_scr[...], w_ref[...], preferred_element_type=f32)
    if n_rope_tiles:
        @pl.when(j < n_rope_tiles)
        def _():
            o_ref[0] = _rope_tile(acc, cos_ref[...], sin_ref[...])

        @pl.when(j >= n_rope_tiles)
        def _():
            o_ref[0] = acc
    else:
        o_ref[0] = acc


def _norm_matmul(x, shift, scale, g, w, *, tm, tn, rope=None, n_rope_cols=0):
    B, T, D = x.shape
    N = w.shape[1]
    per_row = shift.shape[1] != 1
    mod_spec = (pl.BlockSpec((1, tm, D), lambda b, i, j: (b, i, 0)) if per_row
                else pl.BlockSpec((1, 1, D), lambda b, i, j: (b, 0, 0)))
    in_specs = [pl.BlockSpec((1, tm, D), lambda b, i, j: (b, i, 0)), mod_spec, mod_spec,
                pl.BlockSpec((1, D), lambda b, i, j: (0, 0)),
                pl.BlockSpec((D, tn), lambda b, i, j: (0, j))]
    args = [x, shift, scale, g, w]
    n_rope_tiles = 0
    if rope is not None:
        cos, sin = rope
        n_rope_tiles = n_rope_cols // tn
        if cos.shape[0] == T:
            rspec = pl.BlockSpec((tm, LANES), lambda b, i, j: (i, 0))
        else:
            nt = T // tm
            rspec = pl.BlockSpec((tm, LANES), lambda b, i, j: (b * nt + i, 0))
        in_specs += [rspec, rspec]
        args += [cos, sin]
    return pl.pallas_call(
        functools.partial(_norm_mm_kernel, n_rope_tiles=n_rope_tiles),
        grid=(B, T // tm, N // tn),
        in_specs=in_specs,
        out_specs=pl.BlockSpec((1, tm, tn), lambda b, i, j: (b, i, j)),
        out_shape=jax.ShapeDtypeStruct((B, T, N), f32),
        scratch_shapes=[pltpu.VMEM((tm, D), bf16)],
        compiler_params=_cparams(("parallel", "parallel", "arbitrary")),
        name="norm_matmul",
    )(*args)


def _mm_post_kernel(*refs, n_in):
    a_refs = refs[:n_in]
    w_refs = refs[n_in:2 * n_in]
    x_ref, gate_ref, g_ref, o_ref = refs[2 * n_in:]
    y = None
    for a_ref, w_ref in zip(a_refs, w_refs):
        part = jnp.dot(a_ref[0].astype(bf16), w_ref[...], preferred_element_type=f32)
        y = part if y is None else y + part
    o_ref[0] = x_ref[0] + gate_ref[0] * _rmsnorm_rows(y, g_ref[...])


def _matmul_post(a_list, w_list, x, gate, g, *, tm):
    B, T, D = x.shape
    per_row = gate.shape[1] != 1
    gate_spec = (pl.BlockSpec((1, tm, D), lambda b, i: (b, i, 0)) if per_row
                 else pl.BlockSpec((1, 1, D), lambda b, i: (b, 0, 0)))
    in_specs = []
    for a in a_list:
        ka = a.shape[2]
        in_specs.append(pl.BlockSpec((1, tm, ka), lambda b, i: (b, i, 0)))
    for w in w_list:
        in_specs.append(pl.BlockSpec(w.shape, lambda b, i: (0, 0)))
    in_specs += [pl.BlockSpec((1, tm, D), lambda b, i: (b, i, 0)), gate_spec,
                 pl.BlockSpec((1, D), lambda b, i: (0, 0))]
    return pl.pallas_call(
        functools.partial(_mm_post_kernel, n_in=len(a_list)),
        grid=(B, T // tm),
        in_specs=in_specs,
        out_specs=pl.BlockSpec((1, tm, D), lambda b, i: (b, i, 0)),
        out_shape=jax.ShapeDtypeStruct((B, T, D), f32),
        compiler_params=_cparams(("parallel", "parallel")),
        name="matmul_post",
    )(*a_list, *w_list, x, gate, g)


def _mlp_kernel(x_ref, sh_ref, sc_ref, gate_ref, gpre_ref, gpost_ref, wu_ref, wd_ref, o_ref, h_scr, acc_scr):
    k = pl.program_id(2)

    @pl.when(k == 0)
    def _():
        h = _rmsnorm_rows(x_ref[0], gpre_ref[...]) * (1.0 + sc_ref[0]) + sh_ref[0]
        h_scr[...] = h.astype(bf16)
        acc_scr[...] = jnp.zeros_like(acc_scr)

    u = jnp.dot(h_scr[...], wu_ref[...], preferred_element_type=f32)
    a = jnp.square(jnp.maximum(u, 0.0)).astype(bf16)
    acc_scr[...] += jnp.dot(a, wd_ref[...], preferred_element_type=f32)

    @pl.when(k == pl.num_programs(2) - 1)
    def _():
        o_ref[0] = x_ref[0] + gate_ref[0] * _rmsnorm_rows(acc_scr[...], gpost_ref[...])


def _mlp(x, shift, scale, gate, g_pre, g_post, w_up, w_down, *, tm, tf):
    B, T, D = x.shape
    per_row = shift.shape[1] != 1
    mod_spec = (pl.BlockSpec((1, tm, D), lambda b, i, k: (b, i, 0)) if per_row
                else pl.BlockSpec((1, 1, D), lambda b, i, k: (b, 0, 0)))
    vec_spec = pl.BlockSpec((1, D), lambda b, i, k: (0, 0))
    return pl.pallas_call(
        _mlp_kernel,
        grid=(B, T // tm, D_FF // tf),
        in_specs=[pl.BlockSpec((1, tm, D), lambda b, i, k: (b, i, 0)), mod_spec, mod_spec, mod_spec,
                  vec_spec, vec_spec,
                  pl.BlockSpec((D, tf), lambda b, i, k: (0, k)),
                  pl.BlockSpec((tf, D), lambda b, i, k: (k, 0))],
        out_specs=pl.BlockSpec((1, tm, D), lambda b, i, k: (b, i, 0)),
        out_shape=jax.ShapeDtypeStruct((B, T, D), f32),
        scratch_shapes=[pltpu.VMEM((tm, D), bf16), pltpu.VMEM((tm, D), f32)],
        compiler_params=_cparams(("parallel", "parallel", "arbitrary")),
        name="mlp",
    )(x, shift, scale, gate, g_pre, g_post, w_up, w_down)


def _rwkv_pre_kernel(p_ref, sh0_ref, mu_ref, w0_ref, w2_ref, a0_ref, a2_ref, g2_ref, kk_w_ref, ka_ref,
                     rk_ref, bd_ref, r_o, w_o, k_o, v_o, kk_o, kka_o, g_o, bon_o, carry):
    ti = pl.program_id(1)

    @pl.when(ti == 0)
    def _():
        carry[...] = sh0_ref[0]

    p = p_ref[0]
    tt = p.shape[0]
    row = lax.broadcasted_iota(jnp.int32, (tt, 1), 0)
    prev = jnp.where(row == 0, carry[...], pltpu.roll(p, 1, 0))
    carry[...] = p[tt - 1:tt, :]
    ps = p + mu_ref[...] * (prev - p)
    r = ps[:, 0:D_A]
    k = ps[:, D_A:2 * D_A]
    v = ps[:, 2 * D_A:3 * D_A]
    o = 3 * D_A
    dw = ps[:, o:o + LORA_W]
    da = ps[:, o + LORA_W:o + LORA_W + LORA_A]
    dg = ps[:, o + LORA_W + LORA_A:o + LORA_W + LORA_A + LORA_G]
    bd = bd_ref[...]
    wl = w0_ref[...] + jnp.dot(jnp.tanh(dw).astype(bf16), w2_ref[...], preferred_element_type=f32)
    w_log = -_softplus(-wl) - 0.5
    decay = jnp.exp(-jnp.exp(w_log))
    a = jax.nn.sigmoid(a0_ref[...] + jnp.dot(da.astype(bf16), a2_ref[...], preferred_element_type=f32))
    g = jnp.dot(jax.nn.sigmoid(dg).astype(bf16), g2_ref[...], preferred_element_type=f32)
    kk = k * kk_w_ref[...]
    nrm = jnp.sqrt(_seg_sum(kk * kk, bd))
    kk = kk / jnp.maximum(nrm, 1e-12)
    k2 = k * (1.0 + (a - 1.0) * ka_ref[...])
    bonus = _seg_sum(r * k2 * rk_ref[...], bd) * v
    r_o[0] = r
    w_o[0] = decay
    k_o[0] = k2
    v_o[0] = v
    kk_o[0] = kk
    kka_o[0] = kk * a
    g_o[0] = g
    bon_o[0] = bonus


def _rwkv_pre(proj, shift0, mu, w0, w2, a0, a2, g2, k_k, k_a, r_k, bd512, *, tt):
    B, T, _ = proj.shape
    vec = lambda n: pl.BlockSpec((1, n), lambda b, i: (0, 0))
    mat = lambda s: pl.BlockSpec(s, lambda b, i: (0, 0))
    out_spec = pl.BlockSpec((1, tt, D_A), lambda b, i: (b, i, 0))
    out_sds = jax.ShapeDtypeStruct((B, T, D_A), f32)
    return pl.pallas_call(
        _rwkv_pre_kernel,
        grid=(B, T // tt),
        in_specs=[pl.BlockSpec((1, tt, RW_COLS), lambda b, i: (b, i, 0)),
                  pl.BlockSpec((1, 1, RW_COLS), lambda b, i: (b, 0, 0)),
                  vec(RW_COLS), vec(D_A), mat((LORA_W, D_A)), vec(D_A), mat((LORA_A, D_A)),
                  mat((LORA_G, D_A)), vec(D_A), vec(D_A), vec(D_A), mat((D_A, D_A))],
        out_specs=[out_spec] * 8,
        out_shape=[out_sds] * 8,
        scratch_shapes=[pltpu.VMEM((1, RW_COLS), f32)],
        compiler_params=_cparams(("parallel", "arbitrary")),
        name="rwkv_pre",
    )(proj, shift0, mu, w0, w2, a0, a2, g2, k_k, k_a, r_k, bd512)


SCAN_NB = 8
PAIRS = H_A // 2


def _wkv_scan_kernel(r_ref, w_ref, k_ref, v_ref, kk_ref, kka_ref, s0_ref, bd_ref, e_ref,
                     y_ref, st_ref, s_scr):
    c = pl.program_id(1)
    n_steps = r_ref.shape[1]
    rows_per_b = PAIRS * HEAD_DIM

    @pl.when(c == 0)
    def _():
        s_scr[...] = s0_ref[...]

    bd = bd_ref[...]
    e4 = e_ref[...]

    def step8(t8, carry):
        t0 = pl.multiple_of(t8 * SUBLANES, SUBLANES)
        names = (("r", r_ref), ("w", w_ref), ("k", k_ref), ("v", v_ref), ("kk", kk_ref), ("kka", kka_ref))
        blks = [{name: ref[b, pl.ds(t0, SUBLANES), :] for name, ref in names} for b in range(SCAN_NB)]
        yrows = [[] for _ in range(SCAN_NB)]
        e_all = jnp.concatenate([e4] * SCAN_NB, axis=0)
        for j in range(SUBLANES):
            def rows(name):
                parts = [jnp.broadcast_to(blks[b][name][j:j + 1, LANES * p:LANES * (p + 1)], (HEAD_DIM, LANES))
                         for b in range(SCAN_NB) for p in range(PAIRS)]
                return jnp.concatenate(parts, axis=0)

            s = s_scr[...]
            sk = _seg_sum(s * rows("kk"), bd)
            vcol = _seg_sum(e_all * rows("v"), bd)
            s = s * rows("w") - sk * rows("kka") + vcol * rows("k")
            s_scr[...] = s
            yrep = _seg_sum(s * rows("r"), bd)
            yrow = jnp.sum((e_all * yrep).reshape(SCAN_NB * PAIRS, HEAD_DIM, LANES), axis=1)
            for b in range(SCAN_NB):
                yrows[b].append(jnp.concatenate([yrow[b * PAIRS + p:b * PAIRS + p + 1, :] for p in range(PAIRS)],
                                                axis=1))
        for b in range(SCAN_NB):
            y_ref[b, pl.ds(t0, SUBLANES), :] = jnp.concatenate(yrows[b], axis=0)
        return carry

    lax.fori_loop(0, n_steps // SUBLANES, step8, 0)

    @pl.when(c == pl.num_programs(1) - 1)
    def _():
        st_ref[...] = s_scr[...]


def _wkv_scan(r, w, k, v, kk, kka, s0_pairs, bd128, e4, *, chunk):
    NB, T, _ = r.shape
    rows = SCAN_NB * PAIRS * HEAD_DIM
    xspec = pl.BlockSpec((SCAN_NB, chunk, D_A), lambda g, c: (g, c, 0))
    sspec = pl.BlockSpec((rows, LANES), lambda g, c: (g, 0))
    return pl.pallas_call(
        _wkv_scan_kernel,
        grid=(NB // SCAN_NB, T // chunk),
        in_specs=[xspec] * 6 + [sspec,
                                pl.BlockSpec((LANES, LANES), lambda g, c: (0, 0)),
                                pl.BlockSpec((PAIRS * HEAD_DIM, LANES), lambda g, c: (0, 0))],
        out_specs=[xspec, sspec],
        out_shape=[jax.ShapeDtypeStruct((NB, T, D_A), f32),
                   jax.ShapeDtypeStruct((NB * PAIRS * HEAD_DIM, LANES), f32)],
        scratch_shapes=[pltpu.VMEM((rows, LANES), f32)],
        compiler_params=_cparams(("parallel", "arbitrary")),
        name="wkv_scan",
    )(r, w, k, v, kk, kka, s0_pairs, bd128, e4)


def _rwkv_post_kernel(y_ref, g_ref, bon_ref, lw_ref, lb_ref, bd_ref, o_ref):
    y = y_ref[0]
    bd = bd_ref[...]
    mean = _seg_sum(y, bd) * (1.0 / HEAD_DIM)
    d = y - mean
    var = _seg_sum(d * d, bd) * (1.0 / HEAD_DIM)
    yn = d * lax.rsqrt(var + LNX_EPS) * lw_ref[...] + lb_ref[...]
    o_ref[0] = (yn + bon_ref[0]) * g_ref[0]


def _rwkv_post(y, g, bonus, lnx_w, lnx_b, bd512, *, tt):
    B, T, _ = y.shape
    spec = pl.BlockSpec((1, tt, D_A), lambda b, i: (b, i, 0))
    vec = pl.BlockSpec((1, D_A), lambda b, i: (0, 0))
    return pl.pallas_call(
        _rwkv_post_kernel,
        grid=(B, T // tt),
        in_specs=[spec, spec, spec, vec, vec, pl.BlockSpec((D_A, D_A), lambda b, i: (0, 0))],
        out_specs=spec,
        out_shape=jax.ShapeDtypeStruct((B, T, D_A), f32),
        compiler_params=_cparams(("parallel", "parallel")),
        name="rwkv_post",
    )(y, g, bonus, lnx_w, lnx_b, bd512)


def _fox_gate_kernel(f_ref, b_ref, lf_ref, cum_ref):
    z = f_ref[0] + b_ref[...]
    lf = -_softplus(-z)
    lf_ref[0] = lf
    T = lf.shape[0]
    row = lax.broadcasted_iota(jnp.int32, (T, 1), 0)
    x = lf
    s = 1
    while s < T:
        x = x + jnp.where(row >= s, pltpu.roll(x, s, 0), 0.0)
        s *= 2
    cum_ref[0] = x


def _fox_gate(proj, b_f_pad):
    B, T, _ = proj.shape
    spec = pl.BlockSpec((1, T, LANES), lambda b: (b, 0, 0))
    sds = jax.ShapeDtypeStruct((B, T, LANES), f32)
    return pl.pallas_call(
        _fox_gate_kernel,
        grid=(B,),
        in_specs=[pl.BlockSpec((1, T, LANES), lambda b: (b, 0, EV_F_OFF // LANES)),
                  pl.BlockSpec((1, LANES), lambda b: (0, 0))],
        out_specs=[spec, spec],
        out_shape=[sds, sds],
        compiler_params=_cparams(("parallel",)),
        name="fox_gate",
    )(proj, b_f_pad)


FLASH_HG = 4


def _flash_kernel(q_ref, k_ref, v_ref, l2c_ref, *rest, n_heads, tq, tk, has_bias, table_everywhere):
    if has_bias:
        cq_ref, ckt_ref, o_ref, m_scr, acc_scr, qb_scr, cq2_scr = rest
    else:
        o_ref, m_scr, acc_scr, qb_scr = rest
    qi = pl.program_id(1)
    m_scr[...] = jnp.full_like(m_scr, -jnp.inf)
    acc_scr[...] = jnp.zeros_like(acc_scr)
    qb_scr[...] = (q_ref[0] * (ATTN_SCALE * LOG2E)).astype(bf16)
    if has_bias:
        for h in range(n_heads):
            cq2_scr[h] = jnp.broadcast_to(cq_ref[0, :, h:h + 1] * LOG2E, (tq, LANES))
    ones = jnp.ones((tk, HEAD_DIM), bf16)
    nt = (((1,), (1,)), ((), ()))

    def block(it, use_table):
        k0 = pl.multiple_of((qi - it) * tk, tk)
        for g0 in range(0, n_heads, FLASH_HG):
            heads = range(g0, g0 + FLASH_HG)
            parts = []
            for h in heads:
                lo, hi = h * HEAD_DIM, (h + 1) * HEAD_DIM
                k_h = k_ref[0, pl.ds(k0, tk), lo:hi].astype(bf16)
                s = lax.dot_general(qb_scr[:, lo:hi], k_h, nt, preferred_element_type=f32)
                if has_bias:
                    s = s - ckt_ref[0, h:h + 1, pl.ds(k0, tk)] * LOG2E
                parts.append(s)
            s3 = jnp.stack(parts)
            if use_table:
                s3 = s3 + l2c_ref[it][None]
            m_prev = m_scr[g0:g0 + FLASH_HG]
            m_cur = jnp.max(s3, axis=2, keepdims=True)
            if has_bias:
                cq2 = cq2_scr[g0:g0 + FLASH_HG]
                m_next = jnp.maximum(m_prev, m_cur + cq2)
                m_sub = m_next - cq2
            else:
                m_next = jnp.maximum(m_prev, m_cur)
                m_sub = m_next
            p3 = jnp.exp2(s3 - jnp.concatenate([m_sub] * (tk // LANES), axis=2)).astype(bf16)
            alpha = jnp.exp2(m_prev - m_next)
            m_scr[g0:g0 + FLASH_HG] = m_next
            for n, h in enumerate(heads):
                lo, hi = h * HEAD_DIM, (h + 1) * HEAD_DIM
                v_ext = jnp.concatenate([v_ref[0, pl.ds(k0, tk), lo:hi].astype(bf16), ones], axis=1)
                acc_scr[h] = acc_scr[h] * alpha[n] + jnp.dot(p3[n], v_ext, preferred_element_type=f32)

    block(0, True)

    def body(it, carry):
        block(it, table_everywhere)
        return carry

    lax.fori_loop(1, qi + 1, body, 0)
    for h in range(n_heads):
        acc = acc_scr[h]
        o_ref[0, :, h * HEAD_DIM:(h + 1) * HEAD_DIM] = acc[:, 0:HEAD_DIM] / acc[:, HEAD_DIM:2 * HEAD_DIM]


def _flash_attention(q_src, k_src, v_src, q_blk, k_blk, v_blk, l2c, n_heads, *, tq, tk, bias=None,
                     table_everywhere):
    assert tq == tk
    B, T, _ = q_src.shape
    hd = n_heads * HEAD_DIM
    in_specs = [pl.BlockSpec((1, tq, hd), lambda b, i: (b, i, q_blk)),
                pl.BlockSpec((1, T, hd), lambda b, i: (b, 0, k_blk)),
                pl.BlockSpec((1, T, hd), lambda b, i: (b, 0, v_blk)),
                pl.BlockSpec(l2c.shape, lambda b, i: (0, 0, 0))]
    args = [q_src, k_src, v_src, l2c]
    if bias is not None:
        cum, cum_t = bias
        in_specs += [pl.BlockSpec((1, tq, LANES), lambda b, i: (b, i, 0)),
                     pl.BlockSpec((1, cum_t.shape[1], T), lambda b, i: (b, 0, 0))]
        args += [cum, cum_t]
    scratch = [pltpu.VMEM((n_heads, tq, LANES), f32), pltpu.VMEM((n_heads, tq, LANES), f32),
               pltpu.VMEM((tq, hd), bf16)]
    if bias is not None:
        scratch.append(pltpu.VMEM((n_heads, tq, LANES), f32))
    return pl.pallas_call(
        functools.partial(_flash_kernel, n_heads=n_heads, tq=tq, tk=tk, has_bias=bias is not None,
                          table_everywhere=table_everywhere),
        grid=(B, T // tq),
        in_specs=in_specs,
        out_specs=pl.BlockSpec((1, tq, hd), lambda b, i: (b, i, 0)),
        out_shape=jax.ShapeDtypeStruct((B, T, hd), f32),
        scratch_shapes=scratch,
        compiler_params=_cparams(("parallel", "arbitrary")),
        name="flash_attention",
    )(*args)


def _qk(q_h, k_h):
    return lax.dot_general(q_h, k_h, (((1,), (1,)), ((), ())), preferred_element_type=f32)


FOX_PP = 16
DIL_KC = 512


def _block_diag_queries(q2, n_heads):
    nq, hd = q2.shape
    qt = jnp.concatenate([q2] * n_heads, axis=0)
    rh = lax.broadcasted_iota(jnp.int32, (n_heads * nq, hd), 0) // nq
    ch = lax.broadcasted_iota(jnp.int32, (n_heads * nq, hd), 1) // HEAD_DIM
    return jnp.where(rh == ch, qt, 0.0).astype(bf16)


def _rows_per_head(x, nq):
    n_heads, n = x.shape
    return jnp.broadcast_to(x[:, None, :], (n_heads, nq, n)).reshape(n_heads * nq, n)


def _decode_first(s, v_new, m_scr, l_scr, accn_scr, acct_scr):
    m = jnp.max(s, axis=1, keepdims=True)
    p = jnp.exp2(s - m)
    m_scr[...] = jnp.broadcast_to(m, m_scr.shape)
    l_scr[...] = jnp.broadcast_to(jnp.sum(p, axis=1, keepdims=True), l_scr.shape)
    accn_scr[...] = jnp.dot(p.astype(bf16), v_new, preferred_element_type=f32)
    acct_scr[...] = jnp.zeros_like(acct_scr)


def _decode_update(s, vt_all, m_scr, l_scr, accn_scr, acct_scr):
    R = s.shape[0]
    hd = vt_all.shape[0]
    m_prev = m_scr[...]
    m_next = jnp.maximum(m_prev, jnp.max(s, axis=1, keepdims=True))
    p = jnp.exp2(s - m_next[:, 0:1])
    alpha = jnp.exp2(m_prev - m_next)
    l_scr[...] = alpha * l_scr[...] + jnp.sum(p, axis=1, keepdims=True)
    m_scr[...] = m_next
    accn_scr[...] = accn_scr[...] * jnp.concatenate([alpha] * (hd // LANES), axis=1)
    new_t = lax.dot_general(vt_all, p.astype(bf16), (((1,), (1,)), ((), ())), preferred_element_type=f32)
    if R < LANES:
        alpha = jnp.concatenate([alpha, jnp.zeros((LANES - R, LANES), f32)], axis=0)
    alpha_row = jnp.transpose(alpha)[0:1, 0:R]
    acct_scr[:, 0:R] = acct_scr[:, 0:R] * alpha_row + new_t


def _decode_finish(o_ref, n_heads, nq, l_scr, accn_scr, acct_scr):
    for h in range(n_heads):
        blk = acct_scr[h * HEAD_DIM:(h + 1) * HEAD_DIM, :]
        sq = jnp.transpose(jnp.concatenate([blk, jnp.zeros((LANES - HEAD_DIM, LANES), f32)], axis=0))
        rows = slice(h * nq, (h + 1) * nq)
        cols = slice(h * HEAD_DIM, (h + 1) * HEAD_DIM)
        num = sq[rows, 0:HEAD_DIM] + accn_scr[rows, cols]
        o_ref[0, :, cols] = num / l_scr[rows, 0:HEAD_DIM]


def _fox_decode2_kernel(pt_ref, q_ref, kn_ref, vn_ref, cqr_ref, cnt_ref, *rest):
    lf_refs = rest[:FOX_PP]
    k_refs = rest[FOX_PP:2 * FOX_PP]
    v_refs = rest[2 * FOX_PP:3 * FOX_PP]
    o_ref, m_scr, l_scr, accn_scr, acct_scr, tot_scr, qbd_scr = rest[3 * FOX_PP:]
    j = pl.program_id(1)
    nq = q_ref.shape[1]
    R = H_B * nq
    cq2 = cqr_ref[0] * LOG2E

    @pl.when(j == 0)
    def _():
        qbd = _block_diag_queries(q_ref[0] * (ATTN_SCALE * LOG2E), H_B)
        qbd_scr[...] = qbd
        tot_scr[...] = jnp.zeros_like(tot_scr)
        s = _qk(qbd, kn_ref[0].astype(bf16))
        s = s + cq2[:, 0:nq] - _rows_per_head(cnt_ref[0][:, 0:nq] * LOG2E, nq)
        qi = lax.broadcasted_iota(jnp.int32, (R, nq), 0) % nq
        kj = lax.broadcasted_iota(jnp.int32, (R, nq), 1)
        s = jnp.where(kj <= qi, s, -jnp.inf)
        _decode_first(s, vn_ref[0].astype(bf16), m_scr, l_scr, accn_scr, acct_scr)

    lane = lax.broadcasted_iota(jnp.int32, (H_B, PAGE_SIZE), 1)
    qbd = qbd_scr[...]
    tot = tot_scr[...]
    s_parts = []
    v_parts = []
    for i in range(FOX_PP):
        x = lf_refs[i][0]
        sft = 1
        while sft < PAGE_SIZE:
            x = x + jnp.where(lane >= sft, pltpu.roll(x, sft, 1), 0.0)
            sft *= 2
        page_tot = x[:, PAGE_SIZE - 1:PAGE_SIZE]
        bias2 = _rows_per_head((page_tot - x + tot) * LOG2E, nq) + cq2
        tot = tot + page_tot
        kt_all = k_refs[i][0].reshape(D_B, PAGE_SIZE).astype(bf16)
        s_parts.append(jnp.dot(qbd, kt_all, preferred_element_type=f32) + bias2)
        v_parts.append(v_refs[i][0].reshape(D_B, PAGE_SIZE).astype(bf16))
    tot_scr[...] = tot
    _decode_update(jnp.concatenate(s_parts, axis=1), jnp.concatenate(v_parts, axis=1),
                   m_scr, l_scr, accn_scr, acct_scr)

    @pl.when(j == pl.num_programs(1) - 1)
    def _():
        _decode_finish(o_ref, H_B, nq, l_scr, accn_scr, acct_scr)


def _fox_decode2(page_table, proj, cum_rows, cum_new_t, lf_cache_t, k_cache, v_cache):
    NB, nq, _ = proj.shape
    n_pages = page_table.shape[1]
    steps = n_pages // FOX_PP
    R = H_B * nq

    def lf_spec(i):
        return pl.BlockSpec((1, H_B, PAGE_SIZE),
                            lambda b, j, pt: (pt[b, n_pages - 1 - (j * FOX_PP + i)], 0, 0))

    def kv_spec(i):
        return pl.BlockSpec((1, H_B, HEAD_DIM, PAGE_SIZE),
                            lambda b, j, pt: (pt[b, n_pages - 1 - (j * FOX_PP + i)], 0, 0, 0))

    in_specs = [pl.BlockSpec((1, nq, D_B), lambda b, j, pt: (b, 0, EV_Q_OFF // D_B)),
                pl.BlockSpec((1, nq, D_B), lambda b, j, pt: (b, 0, EV_K_OFF // D_B)),
                pl.BlockSpec((1, nq, D_B), lambda b, j, pt: (b, 0, EV_V_OFF // D_B)),
                pl.BlockSpec((1, R, LANES), lambda b, j, pt: (b, 0, 0)),
                pl.BlockSpec((1, SUBLANES, LANES), lambda b, j, pt: (b, 0, 0))]
    args = [proj, proj, proj, cum_rows, cum_new_t]
    for i in range(FOX_PP):
        in_specs.append(lf_spec(i))
        args.append(lf_cache_t)
    for cache in (k_cache, v_cache):
        for i in range(FOX_PP):
            in_specs.append(kv_spec(i))
            args.append(cache)
    grid_spec = pltpu.PrefetchScalarGridSpec(
        num_scalar_prefetch=1,
        grid=(NB, steps),
        in_specs=in_specs,
        out_specs=pl.BlockSpec((1, nq, D_B), lambda b, j, pt: (b, 0, 0)),
        scratch_shapes=[pltpu.VMEM((R, LANES), f32), pltpu.VMEM((R, LANES), f32),
                        pltpu.VMEM((R, D_B), f32), pltpu.VMEM((D_B, LANES), f32),
                        pltpu.VMEM((H_B, LANES), f32), pltpu.VMEM((R, D_B), bf16)],
    )
    return pl.pallas_call(
        _fox_decode2_kernel,
        grid_spec=grid_spec,
        out_shape=jax.ShapeDtypeStruct((NB, nq, D_B), f32),
        compiler_params=_cparams(("parallel", "arbitrary")),
        name="fox_decode",
    )(page_table, *args)


def _dil_decode2_kernel(q_ref, kn_ref, vn_ref, l2new_ref, l2past_ref, k_ref, v_ref, o_ref,
                        m_scr, l_scr, accn_scr, acct_scr, qbd_scr):
    j = pl.program_id(1)
    nq = q_ref.shape[1]

    @pl.when(j == 0)
    def _():
        qbd = _block_diag_queries(q_ref[0] * (ATTN_SCALE * LOG2E), H_C)
        qbd_scr[...] = qbd
        s = _qk(qbd, kn_ref[0].astype(bf16)) + jnp.concatenate([l2new_ref[:, 0:nq]] * H_C, axis=0)
        _decode_first(s, vn_ref[0].astype(bf16), m_scr, l_scr, accn_scr, acct_scr)

    kt_all = k_ref[0].reshape(D_C, DIL_KC).astype(bf16)
    s = jnp.dot(qbd_scr[...], kt_all, preferred_element_type=f32)
    s = s + jnp.concatenate([l2past_ref[...]] * H_C, axis=0)
    _decode_update(s, v_ref[0].reshape(D_C, DIL_KC).astype(bf16), m_scr, l_scr, accn_scr, acct_scr)

    @pl.when(j == pl.num_programs(1) - 1)
    def _():
        _decode_finish(o_ref, H_C, nq, l_scr, accn_scr, acct_scr)


def _dil_decode2(qkv, l2_new, l2_past, k_state, v_state):
    NB, nq, _ = qkv.shape
    L = k_state.shape[3]
    R = H_C * nq
    kv_spec = pl.BlockSpec((1, H_C, HEAD_DIM, DIL_KC), lambda b, j: (b, 0, 0, j))
    return pl.pallas_call(
        _dil_decode2_kernel,
        grid=(NB, L // DIL_KC),
        in_specs=[pl.BlockSpec((1, nq, D_C), lambda b, j: (b, 0, 0)),
                  pl.BlockSpec((1, nq, D_C), lambda b, j: (b, 0, 1)),
                  pl.BlockSpec((1, nq, D_C), lambda b, j: (b, 0, 2)),
                  pl.BlockSpec(l2_new.shape, lambda b, j: (0, 0)),
                  pl.BlockSpec((nq, DIL_KC), lambda b, j: (0, j)),
                  kv_spec, kv_spec],
        out_specs=pl.BlockSpec((1, nq, D_C), lambda b, j: (b, 0, 0)),
        out_shape=jax.ShapeDtypeStruct((NB, nq, D_C), f32),
        scratch_shapes=[pltpu.VMEM((R, LANES), f32), pltpu.VMEM((R, LANES), f32),
                        pltpu.VMEM((R, D_C), f32), pltpu.VMEM((D_C, LANES), f32),
                        pltpu.VMEM((R, D_C), bf16)],
        compiler_params=_cparams(("parallel", "arbitrary")),
        name="dil_decode",
    )(qkv, qkv, qkv, l2_new, l2_past, k_state, v_state)


def _dil_multiplicity(delta):
    delta = np.asarray(delta)
    c = np.zeros(delta.shape, np.float32)
    for w, d in DIL_PAIRS:
        c += ((delta >= 0) & (delta % d == 0) & (delta <= w)).astype(np.float32)
    return c


def _log2_block_tables(n_blocks, t, mult_fn):
    i = np.arange(t)[:, None]
    j = np.arange(t)[None, :]
    with np.errstate(divide="ignore"):
        tabs = [np.log2(mult_fn(i - j + d * t)).astype(np.float32) for d in range(n_blocks)]
    return jnp.asarray(np.stack(tabs))


def _rope_tables(pos):
    half = HEAD_DIM // 2
    inv = ROPE_THETA ** (-jnp.arange(half, dtype=f32) / half)
    ang = pos.astype(f32)[:, None] * inv[None, :]
    cos = jnp.cos(ang)
    sin = jnp.sin(ang)
    return jnp.tile(cos, (1, 4)), jnp.concatenate([-sin, sin, -sin, sin], axis=1)


def _pack_state(s):
    NB = s.shape[0]
    return s.reshape(NB, PAIRS, 2, HEAD_DIM, HEAD_DIM).transpose(0, 1, 3, 2, 4).reshape(NB * PAIRS * HEAD_DIM, LANES)


def _unpack_state(sp, NB):
    return sp.reshape(NB, PAIRS, HEAD_DIM, 2, HEAD_DIM).transpose(0, 1, 3, 2, 4).reshape(NB, H_A, HEAD_DIM, HEAD_DIM)


def _mods(mod_lj, n_prompt, per_row_repeat):
    mp = mod_lj[:n_prompt][:, None, :]
    ms = jnp.repeat(mod_lj[n_prompt:], per_row_repeat, axis=0)[None]
    sp = [mp[..., i * D_MODEL:(i + 1) * D_MODEL] for i in range(3)]
    ss = [ms[..., i * D_MODEL:(i + 1) * D_MODEL] for i in range(3)]
    return sp, ss


def kernel(x_prompt, x_sample, cache_fox_k, cache_fox_v, cache_fox_logf, state_rwkv_wkv, state_rwkv_shift,
           state_dil_k, state_dil_v, page_table, c_prompt, c_sample, ada_w, ada_b, norm_pre, norm_post,
           mlp_up, mlp_down, ev_w_in, ev_w_out, rw_mu, rw_w0, rw_w2, rw_a0, rw_a2, rw_g2, rw_k_k, rw_k_a,
           rw_r_k, rw_lnx_w, rw_lnx_b, fox_b_f, od_w_in, od_w_out):
    BP, T, D = x_prompt.shape
    BS, TS, _ = x_sample.shape
    depth = ada_w.shape[0]
    past_len = page_table.shape[1] * PAGE_SIZE

    mod = _ada_mod(jnp.concatenate([c_prompt, c_sample], axis=0), ada_w, ada_b)

    idx = np.arange(LANES)
    bd128 = jnp.asarray((idx[:, None] // HEAD_DIM == idx[None, :] // HEAD_DIM).astype(np.float32)).astype(bf16)
    idx5 = np.arange(D_A)
    bd512 = jnp.asarray((idx5[:, None] // HEAD_DIM == idx5[None, :] // HEAD_DIM).astype(np.float32)).astype(bf16)
    e4 = jnp.asarray(np.tile((idx[None, :] % HEAD_DIM == np.arange(HEAD_DIM)[:, None]).astype(np.float32),
                             (PAIRS, 1)))
    TQ = 256
    causal_tab = _log2_block_tables(1, TQ, lambda dl: (dl >= 0).astype(np.float32))
    dil_tab = _log2_block_tables(T // TQ, TQ, _dil_multiplicity)

    xp = x_prompt
    xs = x_sample.reshape(1, BS * TS, D)
    outs = {}
    for l in range(depth):
        i = l // 2
        (sh_p, sc_p, gt_p), (sh_s, sc_s, gt_s) = _mods(mod[2 * l], BP, TS)
        g_pre = norm_pre[l, 0][None]
        g_post = norm_post[l, 0][None]
        if l % 2 == 0:
            w_in = ev_w_in[i]
            zeros = lambda n: jnp.zeros((D, n), f32)
            w_pad = jnp.concatenate([w_in[:, :RW_COLS], zeros(EV_Q_OFF - RW_COLS), w_in[:, RW_COLS:],
                                     zeros(EV_COLS_PAD - EV_F_OFF - H_B)], axis=1).astype(bf16)
            w_out = ev_w_out[i].astype(bf16)
            b_f_pad = jnp.pad(fox_b_f[i], (0, LANES - H_B))[None]
            rw_args = (rw_mu[i][None], rw_w0[i][None], rw_w2[i].astype(bf16), rw_a0[i][None],
                       rw_a2[i].astype(bf16), rw_g2[i].astype(bf16), rw_k_k[i][None], rw_k_a[i][None],
                       rw_r_k[i].reshape(1, D_A), bd512)

            proj_p = _norm_matmul(xp, sh_p, sc_p, g_pre, w_pad, tm=1024, tn=1280)
            r, w, k2, v, kk, kka, g, bonus = _rwkv_pre(proj_p, jnp.zeros((BP, 1, RW_COLS), f32), *rw_args, tt=256)
            y, st = _wkv_scan(r, w, k2, v, kk, kka, jnp.zeros((BP * PAIRS * HEAD_DIM, LANES), f32), bd128, e4,
                              chunk=64)
            ya = _rwkv_post(y, g, bonus, rw_lnx_w[i][None], rw_lnx_b[i][None], bd512, tt=512)
            lf_p, cum_p = _fox_gate(proj_p, b_f_pad)
            cum_t = jnp.transpose(cum_p[:, :, :SUBLANES], (0, 2, 1))
            yb = _flash_attention(proj_p, proj_p, proj_p, EV_Q_OFF // D_B, EV_K_OFF // D_B, EV_V_OFF // D_B,
                                  causal_tab, H_B, tq=TQ, tk=TQ, bias=(cum_p, cum_t), table_everywhere=False)
            xp = _matmul_post([ya, yb], [w_out[:D_A], w_out[D_A:]], xp, gt_p, g_post, tm=512)
            outs.setdefault("fk_p", []).append(proj_p[:, :, EV_K_OFF:EV_K_OFF + D_B].reshape(BP, T, H_B, HEAD_DIM))
            outs.setdefault("fv_p", []).append(proj_p[:, :, EV_V_OFF:EV_V_OFF + D_B].reshape(BP, T, H_B, HEAD_DIM))
            outs.setdefault("fl_p", []).append(lf_p[:, :, :H_B])
            outs.setdefault("wkv_p", []).append(_unpack_state(st, BP))
            outs.setdefault("sh_p", []).append(proj_p[:, T - 1, :RW_COLS])

            proj_s = _norm_matmul(xs, sh_s, sc_s, g_pre, w_pad, tm=BS * TS, tn=1280).reshape(BS, TS, EV_COLS_PAD)
            r, w, k2, v, kk, kka, g, bonus = _rwkv_pre(proj_s, state_rwkv_shift[i][:, None, :], *rw_args, tt=TS)
            y, st = _wkv_scan(r, w, k2, v, kk, kka, _pack_state(state_rwkv_wkv[i]), bd128, e4, chunk=TS)
            ya = _rwkv_post(y, g, bonus, rw_lnx_w[i][None], rw_lnx_b[i][None], bd512, tt=TS)
            lf_s, cum_s = _fox_gate(proj_s, b_f_pad)
            cum_s_t = jnp.pad(jnp.transpose(cum_s[:, :, :SUBLANES], (0, 2, 1)), ((0, 0), (0, 0), (0, LANES - TS)))
            lf_cache_t = jnp.transpose(cache_fox_logf[i], (0, 2, 1))
            cum_rows = jnp.broadcast_to(
                jnp.transpose(cum_s[:, :, :H_B], (0, 2, 1)).reshape(BS, H_B * TS, 1), (BS, H_B * TS, LANES))
            yb = _fox_decode2(page_table, proj_s, cum_rows, cum_s_t, lf_cache_t,
                              jnp.transpose(cache_fox_k[i], (0, 2, 3, 1)),
                              jnp.transpose(cache_fox_v[i], (0, 2, 3, 1)))
            xs = _matmul_post([ya.reshape(1, BS * TS, D_A), yb.reshape(1, BS * TS, D_B)],
                              [w_out[:D_A], w_out[D_A:]], xs, gt_s, g_post, tm=BS * TS)
            outs.setdefault("fk_s", []).append(proj_s[:, :, EV_K_OFF:EV_K_OFF + D_B].reshape(BS, TS, H_B, HEAD_DIM))
            outs.setdefault("fv_s", []).append(proj_s[:, :, EV_V_OFF:EV_V_OFF + D_B].reshape(BS, TS, H_B, HEAD_DIM))
            outs.setdefault("fl_s", []).append(lf_s[:, :, :H_B])
            outs.setdefault("wkv_s", []).append(_unpack_state(st, BS))
            outs.setdefault("sh_s", []).append(proj_s[:, TS - 1, :RW_COLS])
        else:
            w_in = od_w_in[i].astype(bf16)
            w_out = od_w_out[i].astype(bf16)
            rope_p = _rope_tables(jnp.arange(T))
            cos_s, sin_s = _rope_tables(past_len + jnp.arange(TS))
            rope_s = (jnp.tile(cos_s, (BS, 1)), jnp.tile(sin_s, (BS, 1)))

            qkv_p = _norm_matmul(xp, sh_p, sc_p, g_pre, w_in, tm=1024, tn=D_C, rope=rope_p, n_rope_cols=2 * D_C)
            o_p = _flash_attention(qkv_p, qkv_p, qkv_p, 0, 1, 2, dil_tab, H_C, tq=TQ, tk=TQ,
                                   table_everywhere=True)
            xp = _matmul_post([o_p], [w_out], xp, gt_p, g_post, tm=512)
            keep = min(DIL_PAIRS[-1][0], T)
            outs.setdefault("dk_p", []).append(qkv_p[:, T - keep:, D_C:2 * D_C].reshape(BP, keep, H_C, HEAD_DIM))
            outs.setdefault("dv_p", []).append(qkv_p[:, T - keep:, 2 * D_C:].reshape(BP, keep, H_C, HEAD_DIM))

            qkv_s = _norm_matmul(xs, sh_s, sc_s, g_pre, w_in, tm=BS * TS, tn=D_C, rope=rope_s,
                                 n_rope_cols=2 * D_C).reshape(BS, TS, 3 * D_C)
            L = state_dil_k.shape[2]
            tq_idx = np.arange(TS)[:, None]
            c_new = np.zeros((TS, LANES), np.float32)
            c_new[:, :TS] = _dil_multiplicity(tq_idx - np.arange(TS)[None, :])
            c_past = _dil_multiplicity(L + tq_idx - np.arange(L)[None, :])
            with np.errstate(divide="ignore"):
                l2_new, l2_past = jnp.asarray(np.log2(c_new)), jnp.asarray(np.log2(c_past))
            o_s = _dil_decode2(qkv_s, l2_new, l2_past,
                               jnp.transpose(state_dil_k[i], (0, 2, 3, 1)),
                               jnp.transpose(state_dil_v[i], (0, 2, 3, 1)))
            xs = _matmul_post([o_s.reshape(1, BS * TS, D_C)], [w_out], xs, gt_s, g_post, tm=BS * TS)
            outs.setdefault("dk_s", []).append(qkv_s[:, :, D_C:2 * D_C].reshape(BS, TS, H_C, HEAD_DIM))
            outs.setdefault("dv_s", []).append(qkv_s[:, :, 2 * D_C:].reshape(BS, TS, H_C, HEAD_DIM))

        (sh_p, sc_p, gt_p), (sh_s, sc_s, gt_s) = _mods(mod[2 * l + 1], BP, TS)
        g_pre = norm_pre[l, 1][None]
        g_post = norm_post[l, 1][None]
        w_up = mlp_up[l].astype(bf16)
        w_down = mlp_down[l].astype(bf16)
        xp = _mlp(xp, sh_p, sc_p, gt_p, g_pre, g_post, w_up, w_down, tm=512, tf=512)
        xs = _mlp(xs, sh_s, sc_s, gt_s, g_pre, g_post, w_up, w_down, tm=BS * TS, tf=512)

    st = lambda name: jnp.stack(outs[name])
    return (xp, xs.reshape(BS, TS, D), st("fk_p"), st("fv_p"), st("fl_p"), st("fk_s"), st("fv_s"), st("fl_s"),
            st("wkv_p"), st("wkv_s"), st("sh_p"), st("sh_s"), st("dk_p"), st("dv_p"), st("dk_s"), st("dv_s"))
```

```python
import functools

import numpy as np
import jax
import jax.numpy as jnp
from jax import lax
from jax.experimental import pallas as pl
from jax.experimental.pallas import tpu as pltpu

f32 = jnp.float32
bf16 = jnp.bfloat16

D_MODEL = 1024
HEAD_DIM = 64
H_A = 8
H_B = 8
H_C = 16
D_A = H_A * HEAD_DIM
D_B = H_B * HEAD_DIM
D_C = H_C * HEAD_DIM
LORA_W = 64
LORA_A = 64
LORA_G = 128
RW_COLS = 3 * D_A + LORA_W + LORA_A + LORA_G
D_FF = 4 * D_MODEL
PAGE_SIZE = 128
DIL_PAIRS = ((128, 1), (512, 4), (2048, 16))
ROPE_THETA = 10000.0
NORM_EPS = 1e-6
LNX_EPS = 64e-5
ATTN_SCALE = HEAD_DIM ** -0.5
LOG2E = 1.4426950408889634

LANES = 128
SUBLANES = 8
VMEM_LIMIT = 56 * 1024 * 1024

EV_Q_OFF = 2048
EV_K_OFF = 2560
EV_V_OFF = 3072
EV_F_OFF = 3584
EV_COLS_PAD = 3840


def _cparams(sem):
    return pltpu.CompilerParams(dimension_semantics=sem, vmem_limit_bytes=VMEM_LIMIT)


def _split_hi_lo(x):
    hi = x.astype(bf16)
    lo = (x - hi.astype(f32)).astype(bf16)
    return hi, lo


def _seg_sum(x, bd):
    hi, lo = _split_hi_lo(x)
    return (jnp.dot(hi, bd, preferred_element_type=f32) + jnp.dot(lo, bd, preferred_element_type=f32))


def _seg_sum_paired(x, bd2):
    n = x.shape[0] // 2
    xb = x.astype(bf16)
    out = jnp.dot(jnp.concatenate([xb[:n], xb[n:]], axis=1), bd2, preferred_element_type=f32)
    return jnp.concatenate([out[:, :LANES], out[:, LANES:]], axis=0)


def _rmsnorm_rows(x, g):
    ms = jnp.mean(x * x, axis=-1, keepdims=True)
    return x * lax.rsqrt(ms + NORM_EPS) * g


def _softplus(z):
    return jnp.maximum(z, 0.0) + jnp.log(1.0 + jnp.exp(-jnp.abs(z)))


def _ada_kernel(c_ref, w_ref, b_ref, o_ref):
    c = c_ref[...]
    s = (c * jax.nn.sigmoid(c)).astype(bf16)
    o_ref[0] = jnp.dot(s, w_ref[0].astype(bf16), preferred_element_type=f32) + b_ref[0]


def _ada_mod(c_all, ada_w, ada_b):
    n = c_all.shape[0]
    nl = ada_w.shape[0] * ada_w.shape[1]
    w = ada_w.reshape(nl, D_MODEL, 3 * D_MODEL)
    b = ada_b.reshape(nl, 1, 3 * D_MODEL)
    tn = 1024
    return pl.pallas_call(
        _ada_kernel,
        grid=(nl, 3 * D_MODEL // tn),
        in_specs=[pl.BlockSpec((n, D_MODEL), lambda l, j: (0, 0)),
                  pl.BlockSpec((1, D_MODEL, tn), lambda l, j: (l, 0, j)),
                  pl.BlockSpec((1, 1, tn), lambda l, j: (l, 0, j))],
        out_specs=pl.BlockSpec((1, n, tn), lambda l, j: (l, 0, j)),
        out_shape=jax.ShapeDtypeStruct((nl, n, 3 * D_MODEL), f32),
        compiler_params=_cparams(("parallel", "parallel")),
        name="ada_mod",
    )(c_all, w, b)


def _rope_tile(x, cos, sin_signed):
    lane = lax.broadcasted_iota(jnp.int32, (x.shape[0], LANES), 1)
    first = (lane % HEAD_DIM) < (HEAD_DIM // 2)
    outs = []
    for gidx in range(x.shape[1] // LANES):
        xg = x[:, gidx * LANES:(gidx + 1) * LANES]
        partner = jnp.where(first, pltpu.roll(xg, LANES - HEAD_DIM // 2, 1), pltpu.roll(xg, HEAD_DIM // 2, 1))
        outs.append(xg * cos + partner * sin_signed)
    return jnp.concatenate(outs, axis=1) if len(outs) > 1 else outs[0]


def _norm_mm_kernel(x_ref, sh_ref, sc_ref, g_ref, w_ref, *rest, n_rope_tiles):
    if n_rope_tiles:
        cos_ref, sin_ref, o_ref, h_scr = rest
    else:
        o_ref, h_scr = rest
    j = pl.program_id(2)

    @pl.when(j == 0)
    def _():
        h = _rmsnorm_rows(x_ref[0], g_ref[...]) * (1.0 + sc_ref[0]) + sh_ref[0]
        h_scr[...] = h.astype(bf16)

    acc = jnp.dot(h_scr[...], w_ref[...], preferred_element_type=f32)
    if n_rope_tiles:
        @pl.when(j < n_rope_tiles)
        def _():
            o_ref[0] = _rope_tile(acc, cos_ref[...], sin_ref[...])

        @pl.when(j >= n_rope_tiles)
        def _():
            o_ref[0] = acc
    else:
        o_ref[0] = acc


def _norm_matmul(x, shift, scale, g, w, *, tm, tn, rope=None, n_rope_cols=0):
    B, T, D = x.shape
    N = w.shape[1]
    per_row = shift.shape[1] != 1
    mod_spec = (pl.BlockSpec((1, tm, D), lambda b, i, j: (b, i, 0)) if per_row
                else pl.BlockSpec((1, 1, D), lambda b, i, j: (b, 0, 0)))
    in_specs = [pl.BlockSpec((1, tm, D), lambda b, i, j: (b, i, 0)), mod_spec, mod_spec,
                pl.BlockSpec((1, D), lambda b, i, j: (0, 0)),
                pl.BlockSpec((D, tn), lambda b, i, j: (0, j))]
    args = [x, shift, scale, g, w]
    n_rope_tiles = 0
    if rope is not None:
        cos, sin = rope
        n_rope_tiles = n_rope_cols // tn
        if cos.shape[0] == T:
            rspec = pl.BlockSpec((tm, LANES), lambda b, i, j: (i, 0))
        else:
            nt = T // tm
            rspec = pl.BlockSpec((tm, LANES), lambda b, i, j: (b * nt + i, 0))
        in_specs += [rspec, rspec]
        args += [cos, sin]
    return pl.pallas_call(
        functools.partial(_norm_mm_kernel, n_rope_tiles=n_rope_tiles),
        grid=(B, T // tm, N // tn),
        in_specs=in_specs,
        out_specs=pl.BlockSpec((1, tm, tn), lambda b, i, j: (b, i, j)),
        out_shape=jax.ShapeDtypeStruct((B, T, N), f32),
        scratch_shapes=[pltpu.VMEM((tm, D), bf16)],
        compiler_params=_cparams(("parallel", "parallel", "arbitrary")),
        name="norm_matmul",
    )(*args)


def _mm_post_kernel(*refs, n_in):
    a_refs = refs[:n_in]
    w_refs = refs[n_in:2 * n_in]
    x_ref, gate_ref, g_ref, o_ref = refs[2 * n_in:]
    y = None
    for a_ref, w_ref in zip(a_refs, w_refs):
        part = jnp.dot(a_ref[0].astype(bf16), w_ref[...], preferred_element_type=f32)
        y = part if y is None else y + part
    o_ref[0] = x_ref[0] + gate_ref[0] * _rmsnorm_rows(y, g_ref[...])


def _matmul_post(a_list, w_list, x, gate, g, *, tm):
    B, T, D = x.shape
    per_row = gate.shape[1] != 1
    gate_spec = (pl.BlockSpec((1, tm, D), lambda b, i: (b, i, 0)) if per_row
                 else pl.BlockSpec((1, 1, D), lambda b, i: (b, 0, 0)))
    in_specs = []
    for a in a_list:
        ka = a.shape[2]
        in_specs.append(pl.BlockSpec((1, tm, ka), lambda b, i: (b, i, 0)))
    for w in w_list:
        in_specs.append(pl.BlockSpec(w.shape, lambda b, i: (0, 0)))
    in_specs += [pl.BlockSpec((1, tm, D), lambda b, i: (b, i, 0)), gate_spec,
                 pl.BlockSpec((1, D), lambda b, i: (0, 0))]
    return pl.pallas_call(
        functools.partial(_mm_post_kernel, n_in=len(a_list)),
        grid=(B, T // tm),
        in_specs=in_specs,
        out_specs=pl.BlockSpec((1, tm, D), lambda b, i: (b, i, 0)),
        out_shape=jax.ShapeDtypeStruct((B, T, D), f32),
        compiler_params=_cparams(("parallel", "parallel")),
        name="matmul_post",
    )(*a_list, *w_list, x, gate, g)


def _mlp_kernel(x_ref, sh_ref, sc_ref, gate_ref, gpre_ref, gpost_ref, wu_ref, wd_ref, o_ref, h_scr, acc_scr):
    k = pl.program_id(2)

    @pl.when(k == 0)
    def _():
        h = _rmsnorm_rows(x_ref[0], gpre_ref[...]) * (1.0 + sc_ref[0]) + sh_ref[0]
        h_scr[...] = h.astype(bf16)
        acc_scr[...] = jnp.zeros_like(acc_scr)

    u = jnp.dot(h_scr[...], wu_ref[...], preferred_element_type=f32)
    a = jnp.square(jnp.maximum(u, 0.0)).astype(bf16)
    acc_scr[...] += jnp.dot(a, wd_ref[...], preferred_element_type=f32)

    @pl.when(k == pl.num_programs(2) - 1)
    def _():
        o_ref[0] = x_ref[0] + gate_ref[0] * _rmsnorm_rows(acc_scr[...], gpost_ref[...])


def _mlp(x, shift, scale, gate, g_pre, g_post, w_up, w_down, *, tm, tf):
    B, T, D = x.shape
    per_row = shift.shape[1] != 1
    mod_spec = (pl.BlockSpec((1, tm, D), lambda b, i, k: (b, i, 0)) if per_row
                else pl.BlockSpec((1, 1, D), lambda b, i, k: (b, 0, 0)))
    vec_spec = pl.BlockSpec((1, D), lambda b, i, k: (0, 0))
    return pl.pallas_call(
        _mlp_kernel,
        grid=(B, T // tm, D_FF // tf),
        in_specs=[pl.BlockSpec((1, tm, D), lambda b, i, k: (b, i, 0)), mod_spec, mod_spec, mod_spec,
                  vec_spec, vec_spec,
                  pl.BlockSpec((D, tf), lambda b, i, k: (0, k)),
                  pl.BlockSpec((tf, D), lambda b, i, k: (k, 0))],
        out_specs=pl.BlockSpec((1, tm, D), lambda b, i, k: (b, i, 0)),
        out_shape=jax.ShapeDtypeStruct((B, T, D), f32),
        scratch_shapes=[pltpu.VMEM((tm, D), bf16), pltpu.VMEM((tm, D), f32)],
        compiler_params=_cparams(("parallel", "parallel", "arbitrary")),
        name="mlp",
    )(x, shift, scale, gate, g_pre, g_post, w_up, w_down)


def _rwkv_pre_kernel(p_ref, sh0_ref, mu_ref, w0_ref, w2_ref, a0_ref, a2_ref, g2_ref, kk_w_ref, ka_ref,
                     rk_ref, bd_ref, r_o, w_o, k_o, v_o, kk_o, kka_o, g_o, bon_o, carry):
    ti = pl.program_id(1)

    @pl.when(ti == 0)
    def _():
        carry[...] = sh0_ref[0]

    p = p_ref[0]
    tt = p.shape[0]
    row = lax.broadcasted_iota(jnp.int32, (tt, 1), 0)
    prev = jnp.where(row == 0, carry[...], pltpu.roll(p, 1, 0))
    carry[...] = p[tt - 1:tt, :]
    ps = p + mu_ref[...] * (prev - p)
    r = ps[:, 0:D_A]
    k = ps[:, D_A:2 * D_A]
    v = ps[:, 2 * D_A:3 * D_A]
    o = 3 * D_A
    dw = ps[:, o:o + LORA_W]
    da = ps[:, o + LORA_W:o + LORA_W + LORA_A]
    dg = ps[:, o + LORA_W + LORA_A:o + LORA_W + LORA_A + LORA_G]
    bd = bd_ref[...]
    wl = w0_ref[...] + jnp.dot(jnp.tanh(dw).astype(bf16), w2_ref[...], preferred_element_type=f32)
    w_log = -_softplus(-wl) - 0.5
    decay = jnp.exp(-jnp.exp(w_log))
    a = jax.nn.sigmoid(a0_ref[...] + jnp.dot(da.astype(bf16), a2_ref[...], preferred_element_type=f32))
    g = jnp.dot(jax.nn.sigmoid(dg).astype(bf16), g2_ref[...], preferred_element_type=f32)
    kk = k * kk_w_ref[...]
    nrm = jnp.sqrt(_seg_sum(kk * kk, bd))
    kk = kk / jnp.maximum(nrm, 1e-12)
    k2 = k * (1.0 + (a - 1.0) * ka_ref[...])
    bonus = _seg_sum(r * k2 * rk_ref[...], bd) * v
    r_o[0] = r
    w_o[0] = decay
    k_o[0] = k2
    v_o[0] = v
    kk_o[0] = kk
    kka_o[0] = kk * a
    g_o[0] = g
    bon_o[0] = bonus


def _rwkv_pre(proj, shift0, mu, w0, w2, a0, a2, g2, k_k, k_a, r_k, bd512, *, tt):
    B, T, _ = proj.shape
    vec = lambda n: pl.BlockSpec((1, n), lambda b, i: (0, 0))
    mat = lambda s: pl.BlockSpec(s, lambda b, i: (0, 0))
    out_spec = pl.BlockSpec((1, tt, D_A), lambda b, i: (b, i, 0))
    out_sds = jax.ShapeDtypeStruct((B, T, D_A), f32)
    return pl.pallas_call(
        _rwkv_pre_kernel,
        grid=(B, T // tt),
        in_specs=[pl.BlockSpec((1, tt, RW_COLS), lambda b, i: (b, i, 0)),
                  pl.BlockSpec((1, 1, RW_COLS), lambda b, i: (b, 0, 0)),
                  vec(RW_COLS), vec(D_A), mat((LORA_W, D_A)), vec(D_A), mat((LORA_A, D_A)),
                  mat((LORA_G, D_A)), vec(D_A), vec(D_A), vec(D_A), mat((D_A, D_A))],
        out_specs=[out_spec] * 8,
        out_shape=[out_sds] * 8,
        scratch_shapes=[pltpu.VMEM((1, RW_COLS), f32)],
        compiler_params=_cparams(("parallel", "arbitrary")),
        name="rwkv_pre",
    )(proj, shift0, mu, w0, w2, a0, a2, g2, k_k, k_a, r_k, bd512)


SCAN_NB = 8
PAIRS = H_A // 2


def _wkv_scan_kernel(r_ref, w_ref, k_ref, v_ref, kk_ref, kka_ref, s0_ref, bd_ref, e_ref,
                     y_ref, st_ref, s_scr):
    c = pl.program_id(1)
    n_steps = r_ref.shape[1]
    rows_per_b = PAIRS * HEAD_DIM

    @pl.when(c == 0)
    def _():
        s_scr[...] = s0_ref[...]

    bd = bd_ref[...]
    e4 = e_ref[...]

    def step8(t8, carry):
        t0 = pl.multiple_of(t8 * SUBLANES, SUBLANES)
        names = (("r", r_ref), ("w", w_ref), ("k", k_ref), ("v", v_ref), ("kk", kk_ref), ("kka", kka_ref))
        blks = [{name: ref[b, pl.ds(t0, SUBLANES), :] for name, ref in names} for b in range(SCAN_NB)]
        yrows = [[] for _ in range(SCAN_NB)]
        e_all = jnp.concatenate([e4] * SCAN_NB, axis=0)
        for j in range(SUBLANES):
            def rows(name):
                parts = [jnp.broadcast_to(blks[b][name][j:j + 1, LANES * p:LANES * (p + 1)], (HEAD_DIM, LANES))
                         for b in range(SCAN_NB) for p in range(PAIRS)]
                return jnp.concatenate(parts, axis=0)

            s = s_scr[...]
            sk = _seg_sum_paired(s * rows("kk"), bd)
            vcol = _seg_sum_paired(e_all * rows("v"), bd)
            s = s * rows("w") - sk * rows("kka") + vcol * rows("k")
            s_scr[...] = s
            yrep = _seg_sum_paired(s * rows("r"), bd)
            yrow = jnp.sum((e_all * yrep).reshape(SCAN_NB * PAIRS, HEAD_DIM, LANES), axis=1)
            for b in range(SCAN_NB):
                yrows[b].append(jnp.concatenate([yrow[b * PAIRS + p:b * PAIRS + p + 1, :] for p in range(PAIRS)],
                                                axis=1))
        for b in range(SCAN_NB):
            y_ref[b, pl.ds(t0, SUBLANES), :] = jnp.concatenate(yrows[b], axis=0)
        return carry

    lax.fori_loop(0, n_steps // SUBLANES, step8, 0)

    @pl.when(c == pl.num_programs(1) - 1)
    def _():
        st_ref[...] = s_scr[...]


def _wkv_scan(r, w, k, v, kk, kka, s0_pairs, bd256, e4, *, chunk):
    NB, T, _ = r.shape
    rows = SCAN_NB * PAIRS * HEAD_DIM
    xspec = pl.BlockSpec((SCAN_NB, chunk, D_A), lambda g, c: (g, c, 0))
    sspec = pl.BlockSpec((rows, LANES), lambda g, c: (g, 0))
    return pl.pallas_call(
        _wkv_scan_kernel,
        grid=(NB // SCAN_NB, T // chunk),
        in_specs=[xspec] * 6 + [sspec,
                                pl.BlockSpec(bd256.shape, lambda g, c: (0, 0)),
                                pl.BlockSpec((PAIRS * HEAD_DIM, LANES), lambda g, c: (0, 0))],
        out_specs=[xspec, sspec],
        out_shape=[jax.ShapeDtypeStruct((NB, T, D_A), f32),
                   jax.ShapeDtypeStruct((NB * PAIRS * HEAD_DIM, LANES), f32)],
        scratch_shapes=[pltpu.VMEM((rows, LANES), f32)],
        compiler_params=_cparams(("parallel", "arbitrary")),
        name="wkv_scan",
    )(r, w, k, v, kk, kka, s0_pairs, bd256, e4)


def _rwkv_post_kernel(y_ref, g_ref, bon_ref, lw_ref, lb_ref, bd_ref, o_ref):
    y = y_ref[0]
    bd = bd_ref[...]
    mean = _seg_sum(y, bd) * (1.0 / HEAD_DIM)
    d = y - mean
    var = _seg_sum(d * d, bd) * (1.0 / HEAD_DIM)
    yn = d * lax.rsqrt(var + LNX_EPS) * lw_ref[...] + lb_ref[...]
    o_ref[0] = (yn + bon_ref[0]) * g_ref[0]


def _rwkv_post(y, g, bonus, lnx_w, lnx_b, bd512, *, tt):
    B, T, _ = y.shape
    spec = pl.BlockSpec((1, tt, D_A), lambda b, i: (b, i, 0))
    vec = pl.BlockSpec((1, D_A), lambda b, i: (0, 0))
    return pl.pallas_call(
        _rwkv_post_kernel,
        grid=(B, T // tt),
        in_specs=[spec, spec, spec, vec, vec, pl.BlockSpec((D_A, D_A), lambda b, i: (0, 0))],
        out_specs=spec,
        out_shape=jax.ShapeDtypeStruct((B, T, D_A), f32),
        compiler_params=_cparams(("parallel", "parallel")),
        name="rwkv_post",
    )(y, g, bonus, lnx_w, lnx_b, bd512)


def _fox_gate_kernel(f_ref, b_ref, lf_ref, cum_ref):
    z = f_ref[0] + b_ref[...]
    lf = -_softplus(-z)
    lf_ref[0] = lf
    T = lf.shape[0]
    row = lax.broadcasted_iota(jnp.int32, (T, 1), 0)
    x = lf
    s = 1
    while s < T:
        x = x + jnp.where(row >= s, pltpu.roll(x, s, 0), 0.0)
        s *= 2
    cum_ref[0] = x


def _fox_gate(proj, b_f_pad):
    B, T, _ = proj.shape
    spec = pl.BlockSpec((1, T, LANES), lambda b: (b, 0, 0))
    sds = jax.ShapeDtypeStruct((B, T, LANES), f32)
    return pl.pallas_call(
        _fox_gate_kernel,
        grid=(B,),
        in_specs=[pl.BlockSpec((1, T, LANES), lambda b: (b, 0, EV_F_OFF // LANES)),
                  pl.BlockSpec((1, LANES), lambda b: (0, 0))],
        out_specs=[spec, spec],
        out_shape=[sds, sds],
        compiler_params=_cparams(("parallel",)),
        name="fox_gate",
    )(proj, b_f_pad)


FLASH_HG = 4


def _flash_kernel(q_ref, k_ref, v_ref, l2c_ref, *rest, n_heads, tq, tk, has_bias, table_everywhere):
    if has_bias:
        cq_ref, ckt_ref, o_ref, m_scr, acc_scr, qb_scr, cq2_scr = rest
    else:
        o_ref, m_scr, acc_scr, qb_scr = rest
    qi = pl.program_id(1)
    m_scr[...] = jnp.full_like(m_scr, -jnp.inf)
    acc_scr[...] = jnp.zeros_like(acc_scr)
    low_q = lax.broadcasted_iota(jnp.int32, (tq, LANES), 1) < HEAD_DIM
    low_k = lax.broadcasted_iota(jnp.int32, (tk, LANES), 1) < HEAD_DIM
    for h in range(n_heads):
        q_pair = q_ref[0, :, (h // 2) * LANES:(h // 2 + 1) * LANES] * (ATTN_SCALE * LOG2E)
        qb_scr[h] = jnp.where(low_q == (h % 2 == 0), q_pair, 0.0).astype(bf16)
        if has_bias:
            cq2_scr[h] = jnp.broadcast_to(cq_ref[0, :, h:h + 1] * LOG2E, (tq, LANES))
    nt = (((1,), (1,)), ((), ()))

    def block(it, use_table):
        k0 = pl.multiple_of((qi - it) * tk, tk)
        for g0 in range(0, n_heads, FLASH_HG):
            heads = range(g0, g0 + FLASH_HG)
            pairs = range(g0 // 2, (g0 + FLASH_HG) // 2)
            k_pairs = {p: k_ref[0, pl.ds(k0, tk), p * LANES:(p + 1) * LANES].astype(bf16) for p in pairs}
            parts = []
            for h in heads:
                s = lax.dot_general(qb_scr[h], k_pairs[h // 2], nt, preferred_element_type=f32)
                if has_bias:
                    s = s - ckt_ref[0, h:h + 1, pl.ds(k0, tk)] * LOG2E
                parts.append(s)
            s3 = jnp.stack(parts)
            if use_table:
                s3 = s3 + l2c_ref[it][None]
            m_prev = m_scr[g0:g0 + FLASH_HG]
            m_cur = jnp.max(s3, axis=2, keepdims=True)
            if has_bias:
                cq2 = cq2_scr[g0:g0 + FLASH_HG]
                m_next = jnp.maximum(m_prev, m_cur + cq2)
                m_sub = m_next - cq2
            else:
                m_next = jnp.maximum(m_prev, m_cur)
                m_sub = m_next
            p3 = jnp.exp2(s3 - jnp.concatenate([m_sub] * (tk // LANES), axis=2)).astype(bf16)
            alpha = jnp.exp2(m_prev - m_next)
            m_scr[g0:g0 + FLASH_HG] = m_next
            v_pairs = {p: v_ref[0, pl.ds(k0, tk), p * LANES:(p + 1) * LANES].astype(bf16) for p in pairs}
            for n, h in enumerate(heads):
                v_ext = jnp.where(low_k == (h % 2 == 0), v_pairs[h // 2], 1.0)
                acc_scr[h] = acc_scr[h] * alpha[n] + jnp.dot(p3[n], v_ext, preferred_element_type=f32)

    block(0, True)

    def body(it, carry):
        block(it, table_everywhere)
        return carry

    lax.fori_loop(1, qi + 1, body, 0)
    for p in range(n_heads // 2):
        acc_e, acc_o = acc_scr[2 * p], acc_scr[2 * p + 1]
        out_e = acc_e / pltpu.roll(acc_e, HEAD_DIM, 1)
        out_o = acc_o / pltpu.roll(acc_o, HEAD_DIM, 1)
        o_ref[0, :, p * LANES:(p + 1) * LANES] = jnp.where(low_q, out_e, out_o)


def _flash_attention(q_src, k_src, v_src, q_blk, k_blk, v_blk, l2c, n_heads, *, tq, tk, bias=None,
                     table_everywhere):
    assert tq == tk
    B, T, _ = q_src.shape
    hd = n_heads * HEAD_DIM
    in_specs = [pl.BlockSpec((1, tq, hd), lambda b, i: (b, i, q_blk)),
                pl.BlockSpec((1, T, hd), lambda b, i: (b, 0, k_blk)),
                pl.BlockSpec((1, T, hd), lambda b, i: (b, 0, v_blk)),
                pl.BlockSpec(l2c.shape, lambda b, i: (0, 0, 0))]
    args = [q_src, k_src, v_src, l2c]
    if bias is not None:
        cum, cum_t = bias
        in_specs += [pl.BlockSpec((1, tq, LANES), lambda b, i: (b, i, 0)),
                     pl.BlockSpec((1, cum_t.shape[1], T), lambda b, i: (b, 0, 0))]
        args += [cum, cum_t]
    scratch = [pltpu.VMEM((n_heads, tq, LANES), f32), pltpu.VMEM((n_heads, tq, LANES), f32),
               pltpu.VMEM((n_heads, tq, LANES), bf16)]
    if bias is not None:
        scratch.append(pltpu.VMEM((n_heads, tq, LANES), f32))
    return pl.pallas_call(
        functools.partial(_flash_kernel, n_heads=n_heads, tq=tq, tk=tk, has_bias=bias is not None,
                          table_everywhere=table_everywhere),
        grid=(B, T // tq),
        in_specs=in_specs,
        out_specs=pl.BlockSpec((1, tq, hd), lambda b, i: (b, i, 0)),
        out_shape=jax.ShapeDtypeStruct((B, T, hd), f32),
        scratch_shapes=scratch,
        compiler_params=_cparams(("parallel", "arbitrary")),
        name="flash_attention",
    )(*args)


def _qk(q_h, k_h):
    return lax.dot_general(q_h, k_h, (((1,), (1,)), ((), ())), preferred_element_type=f32)


FOX_PP = 32
DIL_KC = 512


def _block_diag_queries(q2, n_heads):
    nq, hd = q2.shape
    qt = jnp.concatenate([q2] * n_heads, axis=0)
    rh = lax.broadcasted_iota(jnp.int32, (n_heads * nq, hd), 0) // nq
    ch = lax.broadcasted_iota(jnp.int32, (n_heads * nq, hd), 1) // HEAD_DIM
    return jnp.where(rh == ch, qt, 0.0).astype(bf16)


def _rows_per_head(x, nq):
    n_heads, n = x.shape
    return jnp.broadcast_to(x[:, None, :], (n_heads, nq, n)).reshape(n_heads * nq, n)


def _decode_first(s, v_new, m_scr, l_scr, accn_scr, acct_scr):
    m = jnp.max(s, axis=1, keepdims=True)
    p = jnp.exp2(s - m)
    m_scr[...] = jnp.broadcast_to(m, m_scr.shape)
    l_scr[...] = jnp.broadcast_to(jnp.sum(p, axis=1, keepdims=True), l_scr.shape)
    accn_scr[...] = jnp.dot(p.astype(bf16), v_new, preferred_element_type=f32)
    acct_scr[...] = jnp.zeros_like(acct_scr)


def _decode_update(s, vt_all, m_scr, l_scr, accn_scr, acct_scr):
    R = s.shape[0]
    hd = vt_all.shape[0]
    m_prev = m_scr[...]
    m_next = jnp.maximum(m_prev, jnp.max(s, axis=1, keepdims=True))
    p = jnp.exp2(s - m_next[:, 0:1])
    alpha = jnp.exp2(m_prev - m_next)
    l_scr[...] = alpha * l_scr[...] + jnp.sum(p, axis=1, keepdims=True)
    m_scr[...] = m_next
    accn_scr[...] = accn_scr[...] * jnp.concatenate([alpha] * (hd // LANES), axis=1)
    new_t = lax.dot_general(vt_all, p.astype(bf16), (((1,), (1,)), ((), ())), preferred_element_type=f32)
    if R < LANES:
        alpha = jnp.concatenate([alpha, jnp.zeros((LANES - R, LANES), f32)], axis=0)
    alpha_row = jnp.transpose(alpha)[0:1, 0:R]
    acct_scr[:, 0:R] = acct_scr[:, 0:R] * alpha_row + new_t


def _decode_finish(o_ref, n_heads, nq, l_scr, accn_scr, acct_scr):
    for h in range(n_heads):
        blk = acct_scr[h * HEAD_DIM:(h + 1) * HEAD_DIM, :]
        sq = jnp.transpose(jnp.concatenate([blk, jnp.zeros((LANES - HEAD_DIM, LANES), f32)], axis=0))
        rows = slice(h * nq, (h + 1) * nq)
        cols = slice(h * HEAD_DIM, (h + 1) * HEAD_DIM)
        num = sq[rows, 0:HEAD_DIM] + accn_scr[rows, cols]
        o_ref[0, :, cols] = num / l_scr[rows, 0:HEAD_DIM]


def _fox_decode2_kernel(pt_ref, q_ref, kn_ref, vn_ref, cqr_ref, cnt_ref, *rest):
    lf_refs = rest[:FOX_PP]
    k_refs = rest[FOX_PP:2 * FOX_PP]
    v_refs = rest[2 * FOX_PP:3 * FOX_PP]
    o_ref, m_scr, l_scr, accn_scr, acct_scr, tot_scr, qbd_scr = rest[3 * FOX_PP:]
    j = pl.program_id(1)
    nq = q_ref.shape[1]
    R = H_B * nq
    cq2 = cqr_ref[0] * LOG2E

    @pl.when(j == 0)
    def _():
        qbd = _block_diag_queries(q_ref[0] * (ATTN_SCALE * LOG2E), H_B)
        qbd_scr[...] = qbd
        tot_scr[...] = jnp.zeros_like(tot_scr)
        s = _qk(qbd, kn_ref[0].astype(bf16))
        s = s + cq2[:, 0:nq] - _rows_per_head(cnt_ref[0][:, 0:nq] * LOG2E, nq)
        qi = lax.broadcasted_iota(jnp.int32, (R, nq), 0) % nq
        kj = lax.broadcasted_iota(jnp.int32, (R, nq), 1)
        s = jnp.where(kj <= qi, s, -jnp.inf)
        _decode_first(s, vn_ref[0].astype(bf16), m_scr, l_scr, accn_scr, acct_scr)

    lane = lax.broadcasted_iota(jnp.int32, (H_B, PAGE_SIZE), 1)
    qbd = qbd_scr[...]
    tot = tot_scr[...]
    s_parts = []
    v_parts = []
    for i in range(FOX_PP):
        x = lf_refs[i][0]
        sft = 1
        while sft < PAGE_SIZE:
            x = x + jnp.where(lane >= sft, pltpu.roll(x, sft, 1), 0.0)
            sft *= 2
        page_tot = x[:, PAGE_SIZE - 1:PAGE_SIZE]
        bias2 = _rows_per_head((page_tot - x + tot) * LOG2E, nq) + cq2
        tot = tot + page_tot
        kt_all = k_refs[i][0].reshape(D_B, PAGE_SIZE).astype(bf16)
        s_parts.append(jnp.dot(qbd, kt_all, preferred_element_type=f32) + bias2)
        v_parts.append(v_refs[i][0].reshape(D_B, PAGE_SIZE).astype(bf16))
    tot_scr[...] = tot
    _decode_update(jnp.concatenate(s_parts, axis=1), jnp.concatenate(v_parts, axis=1),
                   m_scr, l_scr, accn_scr, acct_scr)

    @pl.when(j == pl.num_programs(1) - 1)
    def _():
        _decode_finish(o_ref, H_B, nq, l_scr, accn_scr, acct_scr)


def _fox_decode2(page_table, proj, cum_rows, cum_new_t, lf_cache_t, k_cache, v_cache):
    NB, nq, _ = proj.shape
    n_pages = page_table.shape[1]
    steps = n_pages // FOX_PP
    R = H_B * nq

    def lf_spec(i):
        return pl.BlockSpec((1, H_B, PAGE_SIZE),
                            lambda b, j, pt: (pt[b, n_pages - 1 - (j * FOX_PP + i)], 0, 0))

    def kv_spec(i):
        return pl.BlockSpec((1, H_B, HEAD_DIM, PAGE_SIZE),
                            lambda b, j, pt: (pt[b, n_pages - 1 - (j * FOX_PP + i)], 0, 0, 0))

    in_specs = [pl.BlockSpec((1, nq, D_B), lambda b, j, pt: (b, 0, EV_Q_OFF // D_B)),
                pl.BlockSpec((1, nq, D_B), lambda b, j, pt: (b, 0, EV_K_OFF // D_B)),
                pl.BlockSpec((1, nq, D_B), lambda b, j, pt: (b, 0, EV_V_OFF // D_B)),
                pl.BlockSpec((1, R, LANES), lambda b, j, pt: (b, 0, 0)),
                pl.BlockSpec((1, SUBLANES, LANES), lambda b, j, pt: (b, 0, 0))]
    args = [proj, proj, proj, cum_rows, cum_new_t]
    for i in range(FOX_PP):
        in_specs.append(lf_spec(i))
        args.append(lf_cache_t)
    for cache in (k_cache, v_cache):
        for i in range(FOX_PP):
            in_specs.append(kv_spec(i))
            args.append(cache)
    grid_spec = pltpu.PrefetchScalarGridSpec(
        num_scalar_prefetch=1,
        grid=(NB, steps),
        in_specs=in_specs,
        out_specs=pl.BlockSpec((1, nq, D_B), lambda b, j, pt: (b, 0, 0)),
        scratch_shapes=[pltpu.VMEM((R, LANES), f32), pltpu.VMEM((R, LANES), f32),
                        pltpu.VMEM((R, D_B), f32), pltpu.VMEM((D_B, LANES), f32),
                        pltpu.VMEM((H_B, LANES), f32), pltpu.VMEM((R, D_B), bf16)],
    )
    return pl.pallas_call(
        _fox_decode2_kernel,
        grid_spec=grid_spec,
        out_shape=jax.ShapeDtypeStruct((NB, nq, D_B), f32),
        compiler_params=_cparams(("parallel", "arbitrary")),
        name="fox_decode",
    )(page_table, *args)


def _dil_decode2_kernel(q_ref, kn_ref, vn_ref, l2new_ref, l2past_ref, k_ref, v_ref, o_ref,
                        m_scr, l_scr, accn_scr, acct_scr, qbd_scr):
    j = pl.program_id(1)
    nq = q_ref.shape[1]

    @pl.when(j == 0)
    def _():
        qbd = _block_diag_queries(q_ref[0] * (ATTN_SCALE * LOG2E), H_C)
        qbd_scr[...] = qbd
        s = _qk(qbd, kn_ref[0].astype(bf16)) + jnp.concatenate([l2new_ref[:, 0:nq]] * H_C, axis=0)
        _decode_first(s, vn_ref[0].astype(bf16), m_scr, l_scr, accn_scr, acct_scr)

    kt_all = k_ref[0].reshape(D_C, DIL_KC).astype(bf16)
    s = jnp.dot(qbd_scr[...], kt_all, preferred_element_type=f32)
    s = s + jnp.concatenate([l2past_ref[...]] * H_C, axis=0)
    _decode_update(s, v_ref[0].reshape(D_C, DIL_KC).astype(bf16), m_scr, l_scr, accn_scr, acct_scr)

    @pl.when(j == pl.num_programs(1) - 1)
    def _():
        _decode_finish(o_ref, H_C, nq, l_scr, accn_scr, acct_scr)


def _dil_decode2(qkv, l2_new, l2_past, k_state, v_state):
    NB, nq, _ = qkv.shape
    L = k_state.shape[3]
    R = H_C * nq
    kv_spec = pl.BlockSpec((1, H_C, HEAD_DIM, DIL_KC), lambda b, j: (b, 0, 0, j))
    return pl.pallas_call(
        _dil_decode2_kernel,
        grid=(NB, L // DIL_KC),
        in_specs=[pl.BlockSpec((1, nq, D_C), lambda b, j: (b, 0, 0)),
                  pl.BlockSpec((1, nq, D_C), lambda b, j: (b, 0, 1)),
                  pl.BlockSpec((1, nq, D_C), lambda b, j: (b, 0, 2)),
                  pl.BlockSpec(l2_new.shape, lambda b, j: (0, 0)),
                  pl.BlockSpec((nq, DIL_KC), lambda b, j: (0, j)),
                  kv_spec, kv_spec],
        out_specs=pl.BlockSpec((1, nq, D_C), lambda b, j: (b, 0, 0)),
        out_shape=jax.ShapeDtypeStruct((NB, nq, D_C), f32),
        scratch_shapes=[pltpu.VMEM((R, LANES), f32), pltpu.VMEM((R, LANES), f32),
                        pltpu.VMEM((R, D_C), f32), pltpu.VMEM((D_C, LANES), f32),
                        pltpu.VMEM((R, D_C), bf16)],
        compiler_params=_cparams(("parallel", "arbitrary")),
        name="dil_decode",
    )(qkv, qkv, qkv, l2_new, l2_past, k_state, v_state)


def _dil_multiplicity(delta):
    delta = np.asarray(delta)
    c = np.zeros(delta.shape, np.float32)
    for w, d in DIL_PAIRS:
        c += ((delta >= 0) & (delta % d == 0) & (delta <= w)).astype(np.float32)
    return c


def _log2_block_tables(n_blocks, t, mult_fn):
    i = np.arange(t)[:, None]
    j = np.arange(t)[None, :]
    with np.errstate(divide="ignore"):
        tabs = [np.log2(mult_fn(i - j + d * t)).astype(np.float32) for d in range(n_blocks)]
    return jnp.asarray(np.stack(tabs))


def _rope_tables(pos):
    half = HEAD_DIM // 2
    inv = ROPE_THETA ** (-jnp.arange(half, dtype=f32) / half)
    ang = pos.astype(f32)[:, None] * inv[None, :]
    cos = jnp.cos(ang)
    sin = jnp.sin(ang)
    return jnp.tile(cos, (1, 4)), jnp.concatenate([-sin, sin, -sin, sin], axis=1)


def _pack_state(s):
    NB = s.shape[0]
    return s.reshape(NB, PAIRS, 2, HEAD_DIM, HEAD_DIM).transpose(0, 1, 3, 2, 4).reshape(NB * PAIRS * HEAD_DIM, LANES)


def _unpack_state(sp, NB):
    return sp.reshape(NB, PAIRS, HEAD_DIM, 2, HEAD_DIM).transpose(0, 1, 3, 2, 4).reshape(NB, H_A, HEAD_DIM, HEAD_DIM)


def _mods(mod_lj, n_prompt, per_row_repeat):
    mp = mod_lj[:n_prompt][:, None, :]
    ms = jnp.repeat(mod_lj[n_prompt:], per_row_repeat, axis=0)[None]
    sp = [mp[..., i * D_MODEL:(i + 1) * D_MODEL] for i in range(3)]
    ss = [ms[..., i * D_MODEL:(i + 1) * D_MODEL] for i in range(3)]
    return sp, ss


def kernel(x_prompt, x_sample, cache_fox_k, cache_fox_v, cache_fox_logf, state_rwkv_wkv, state_rwkv_shift,
           state_dil_k, state_dil_v, page_table, c_prompt, c_sample, ada_w, ada_b, norm_pre, norm_post,
           mlp_up, mlp_down, ev_w_in, ev_w_out, rw_mu, rw_w0, rw_w2, rw_a0, rw_a2, rw_g2, rw_k_k, rw_k_a,
           rw_r_k, rw_lnx_w, rw_lnx_b, fox_b_f, od_w_in, od_w_out):
    BP, T, D = x_prompt.shape
    BS, TS, _ = x_sample.shape
    depth = ada_w.shape[0]
    past_len = page_table.shape[1] * PAGE_SIZE

    mod = _ada_mod(jnp.concatenate([c_prompt, c_sample], axis=0), ada_w, ada_b)

    idx = np.arange(LANES)
    idx2 = np.arange(2 * LANES)
    bd256 = jnp.asarray((idx2[:, None] // HEAD_DIM == idx2[None, :] // HEAD_DIM).astype(np.float32)).astype(bf16)
    idx5 = np.arange(D_A)
    bd512 = jnp.asarray((idx5[:, None] // HEAD_DIM == idx5[None, :] // HEAD_DIM).astype(np.float32)).astype(bf16)
    e4 = jnp.asarray(np.tile((idx[None, :] % HEAD_DIM == np.arange(HEAD_DIM)[:, None]).astype(np.float32),
                             (PAIRS, 1)))
    TQ = 256
    causal_tab = _log2_block_tables(1, TQ, lambda dl: (dl >= 0).astype(np.float32))
    dil_tab = _log2_block_tables(T // TQ, TQ, _dil_multiplicity)

    xp = x_prompt
    xs = x_sample.reshape(1, BS * TS, D)
    outs = {}
    for l in range(depth):
        i = l // 2
        (sh_p, sc_p, gt_p), (sh_s, sc_s, gt_s) = _mods(mod[2 * l], BP, TS)
        g_pre = norm_pre[l, 0][None]
        g_post = norm_post[l, 0][None]
        if l % 2 == 0:
            w_in = ev_w_in[i]
            zeros = lambda n: jnp.zeros((D, n), f32)
            w_pad = jnp.concatenate([w_in[:, :RW_COLS], zeros(EV_Q_OFF - RW_COLS), w_in[:, RW_COLS:],
                                     zeros(EV_COLS_PAD - EV_F_OFF - H_B)], axis=1).astype(bf16)
            w_out = ev_w_out[i].astype(bf16)
            b_f_pad = jnp.pad(fox_b_f[i], (0, LANES - H_B))[None]
            rw_args = (rw_mu[i][None], rw_w0[i][None], rw_w2[i].astype(bf16), rw_a0[i][None],
                       rw_a2[i].astype(bf16), rw_g2[i].astype(bf16), rw_k_k[i][None], rw_k_a[i][None],
                       rw_r_k[i].reshape(1, D_A), bd512)

            proj_p = _norm_matmul(xp, sh_p, sc_p, g_pre, w_pad, tm=1024, tn=1280)
            r, w, k2, v, kk, kka, g, bonus = _rwkv_pre(proj_p, jnp.zeros((BP, 1, RW_COLS), f32), *rw_args, tt=256)
            y, st = _wkv_scan(r, w, k2, v, kk, kka, jnp.zeros((BP * PAIRS * HEAD_DIM, LANES), f32), bd256, e4,
                              chunk=64)
            ya = _rwkv_post(y, g, bonus, rw_lnx_w[i][None], rw_lnx_b[i][None], bd512, tt=512)
            lf_p, cum_p = _fox_gate(proj_p, b_f_pad)
            cum_t = jnp.transpose(cum_p[:, :, :SUBLANES], (0, 2, 1))
            yb = _flash_attention(proj_p, proj_p, proj_p, EV_Q_OFF // D_B, EV_K_OFF // D_B, EV_V_OFF // D_B,
                                  causal_tab, H_B, tq=TQ, tk=TQ, bias=(cum_p, cum_t), table_everywhere=False)
            xp = _matmul_post([ya, yb], [w_out[:D_A], w_out[D_A:]], xp, gt_p, g_post, tm=512)
            outs.setdefault("fk_p", []).append(proj_p[:, :, EV_K_OFF:EV_K_OFF + D_B].reshape(BP, T, H_B, HEAD_DIM))
            outs.setdefault("fv_p", []).append(proj_p[:, :, EV_V_OFF:EV_V_OFF + D_B].reshape(BP, T, H_B, HEAD_DIM))
            outs.setdefault("fl_p", []).append(lf_p[:, :, :H_B])
            outs.setdefault("wkv_p", []).append(_unpack_state(st, BP))
            outs.setdefault("sh_p", []).append(proj_p[:, T - 1, :RW_COLS])

            proj_s = _norm_matmul(xs, sh_s, sc_s, g_pre, w_pad, tm=BS * TS, tn=1280).reshape(BS, TS, EV_COLS_PAD)
            r, w, k2, v, kk, kka, g, bonus = _rwkv_pre(proj_s, state_rwkv_shift[i][:, None, :], *rw_args, tt=TS)
            y, st = _wkv_scan(r, w, k2, v, kk, kka, _pack_state(state_rwkv_wkv[i]), bd256, e4, chunk=TS)
            ya = _rwkv_post(y, g, bonus, rw_lnx_w[i][None], rw_lnx_b[i][None], bd512, tt=TS)
            lf_s, cum_s = _fox_gate(proj_s, b_f_pad)
            cum_s_t = jnp.pad(jnp.transpose(cum_s[:, :, :SUBLANES], (0, 2, 1)), ((0, 0), (0, 0), (0, LANES - TS)))
            lf_cache_t = jnp.transpose(cache_fox_logf[i], (0, 2, 1))
            cum_rows = jnp.broadcast_to(
                jnp.transpose(cum_s[:, :, :H_B], (0, 2, 1)).reshape(BS, H_B * TS, 1), (BS, H_B * TS, LANES))
            yb = _fox_decode2(page_table, proj_s, cum_rows, cum_s_t, lf_cache_t,
                              jnp.transpose(cache_fox_k[i], (0, 2, 3, 1)),
                              jnp.transpose(cache_fox_v[i], (0, 2, 3, 1)))
            xs = _matmul_post([ya.reshape(1, BS * TS, D_A), yb.reshape(1, BS * TS, D_B)],
                              [w_out[:D_A], w_out[D_A:]], xs, gt_s, g_post, tm=BS * TS)
            outs.setdefault("fk_s", []).append(proj_s[:, :, EV_K_OFF:EV_K_OFF + D_B].reshape(BS, TS, H_B, HEAD_DIM))
            outs.setdefault("fv_s", []).append(proj_s[:, :, EV_V_OFF:EV_V_OFF + D_B].reshape(BS, TS, H_B, HEAD_DIM))
            outs.setdefault("fl_s", []).append(lf_s[:, :, :H_B])
            outs.setdefault("wkv_s", []).append(_unpack_state(st, BS))
            outs.setdefault("sh_s", []).append(proj_s[:, TS - 1, :RW_COLS])
        else:
            w_in = od_w_in[i].astype(bf16)
            w_out = od_w_out[i].astype(bf16)
            rope_p = _rope_tables(jnp.arange(T))
            cos_s, sin_s = _rope_tables(past_len + jnp.arange(TS))
            rope_s = (jnp.tile(cos_s, (BS, 1)), jnp.tile(sin_s, (BS, 1)))

            qkv_p = _norm_matmul(xp, sh_p, sc_p, g_pre, w_in, tm=1024, tn=D_C, rope=rope_p, n_rope_cols=2 * D_C)
            o_p = _flash_attention(qkv_p, qkv_p, qkv_p, 0, 1, 2, dil_tab, H_C, tq=TQ, tk=TQ,
                                   table_everywhere=True)
            xp = _matmul_post([o_p], [w_out], xp, gt_p, g_post, tm=512)
            keep = min(DIL_PAIRS[-1][0], T)
            outs.setdefault("dk_p", []).append(qkv_p[:, T - keep:, D_C:2 * D_C].reshape(BP, keep, H_C, HEAD_DIM))
            outs.setdefault("dv_p", []).append(qkv_p[:, T - keep:, 2 * D_C:].reshape(BP, keep, H_C, HEAD_DIM))

            qkv_s = _norm_matmul(xs, sh_s, sc_s, g_pre, w_in, tm=BS * TS, tn=D_C, rope=rope_s,
                                 n_rope_cols=2 * D_C).reshape(BS, TS, 3 * D_C)
            L = state_dil_k.shape[2]
            tq_idx = np.arange(TS)[:, None]
            c_new = np.zeros((TS, LANES), np.float32)
            c_new[:, :TS] = _dil_multiplicity(tq_idx - np.arange(TS)[None, :])
            c_past = _dil_multiplicity(L + tq_idx - np.arange(L)[None, :])
            with np.errstate(divide="ignore"):
                l2_new, l2_past = jnp.asarray(np.log2(c_new)), jnp.asarray(np.log2(c_past))
            o_s = _dil_decode2(qkv_s, l2_new, l2_past,
                               jnp.transpose(state_dil_k[i], (0, 2, 3, 1)),
                               jnp.transpose(state_dil_v[i], (0, 2, 3, 1)))
            xs = _matmul_post([o_s.reshape(1, BS * TS, D_C)], [w_out], xs, gt_s, g_post, tm=BS * TS)
            outs.setdefault("dk_s", []).append(qkv_s[:, :, D_C:2 * D_C].reshape(BS, TS, H_C, HEAD_DIM))
            outs.setdefault("dv_s", []).append(qkv_s[:, :, 2 * D_C:].reshape(BS, TS, H_C, HEAD_DIM))

        (sh_p, sc_p, gt_p), (sh_s, sc_s, gt_s) = _mods(mod[2 * l + 1], BP, TS)
        g_pre = norm_pre[l, 1][None]
        g_post = norm_post[l, 1][None]
        w_up = mlp_up[l].astype(bf16)
        w_down = mlp_down[l].astype(bf16)
        xp = _mlp(xp, sh_p, sc_p, gt_p, g_pre, g_post, w_up, w_down, tm=1024, tf=1024)
        xs = _mlp(xs, sh_s, sc_s, gt_s, g_pre, g_post, w_up, w_down, tm=BS * TS, tf=1024)

    st = lambda name: jnp.stack(outs[name])
    return (xp, xs.reshape(BS, TS, D), st("fk_p"), st("fv_p"), st("fl_p"), st("fk_s"), st("fv_s"), st("fl_s"),
            st("wkv_p"), st("wkv_s"), st("sh_p"), st("sh_s"), st("dk_p"), st("dv_p"), st("dk_s"), st("dv_s"))
```

```python
import functools

import numpy as np
import jax
import jax.numpy as jnp
from jax import lax
from jax.experimental import pallas as pl
from jax.experimental.pallas import tpu as pltpu

f32 = jnp.float32
bf16 = jnp.bfloat16

D_MODEL = 1024
HEAD_DIM = 64
H_A = 8
H_B = 8
H_C = 16
D_A = H_A * HEAD_DIM
D_B = H_B * HEAD_DIM
D_C = H_C * HEAD_DIM
LORA_W = 64
LORA_A = 64
LORA_G = 128
RW_COLS = 3 * D_A + LORA_W + LORA_A + LORA_G
D_FF = 4 * D_MODEL
PAGE_SIZE = 128
DIL_PAIRS = ((128, 1), (512, 4), (2048, 16))
ROPE_THETA = 10000.0
NORM_EPS = 1e-6
LNX_EPS = 64e-5
ATTN_SCALE = HEAD_DIM ** -0.5
LOG2E = 1.4426950408889634

LANES = 128
SUBLANES = 8
VMEM_LIMIT = 56 * 1024 * 1024

EV_Q_OFF = 2048
EV_K_OFF = 2560
EV_V_OFF = 3072
EV_F_OFF = 3584
EV_COLS_PAD = 3840


def _cparams(sem):
    return pltpu.CompilerParams(dimension_semantics=sem, vmem_limit_bytes=VMEM_LIMIT)


def _split_hi_lo(x):
    hi = x.astype(bf16)
    lo = (x - hi.astype(f32)).astype(bf16)
    return hi, lo


def _seg_sum(x, bd):
    hi, lo = _split_hi_lo(x)
    return (jnp.dot(hi, bd, preferred_element_type=f32) + jnp.dot(lo, bd, preferred_element_type=f32))


def _seg_sum_paired(x, bd2):
    n = x.shape[0] // 2
    xb = x.astype(bf16)
    out = jnp.dot(jnp.concatenate([xb[:n], xb[n:]], axis=1), bd2, preferred_element_type=f32)
    return jnp.concatenate([out[:, :LANES], out[:, LANES:]], axis=0)


def _rmsnorm_rows(x, g):
    ms = jnp.mean(x * x, axis=-1, keepdims=True)
    return x * lax.rsqrt(ms + NORM_EPS) * g


def _softplus(z):
    return jnp.maximum(z, 0.0) + jnp.log(1.0 + jnp.exp(-jnp.abs(z)))


def _ada_kernel(c_ref, w_ref, b_ref, o_ref):
    c = c_ref[...]
    s = (c * jax.nn.sigmoid(c)).astype(bf16)
    o_ref[0] = jnp.dot(s, w_ref[0].astype(bf16), preferred_element_type=f32) + b_ref[0]


def _ada_mod(c_all, ada_w, ada_b):
    n = c_all.shape[0]
    nl = ada_w.shape[0] * ada_w.shape[1]
    w = ada_w.reshape(nl, D_MODEL, 3 * D_MODEL)
    b = ada_b.reshape(nl, 1, 3 * D_MODEL)
    tn = 1024
    return pl.pallas_call(
        _ada_kernel,
        grid=(nl, 3 * D_MODEL // tn),
        in_specs=[pl.BlockSpec((n, D_MODEL), lambda l, j: (0, 0)),
                  pl.BlockSpec((1, D_MODEL, tn), lambda l, j: (l, 0, j)),
                  pl.BlockSpec((1, 1, tn), lambda l, j: (l, 0, j))],
        out_specs=pl.BlockSpec((1, n, tn), lambda l, j: (l, 0, j)),
        out_shape=jax.ShapeDtypeStruct((nl, n, 3 * D_MODEL), f32),
        compiler_params=_cparams(("parallel", "parallel")),
        name="ada_mod",
    )(c_all, w, b)


def _rope_tile(x, cos, sin_signed):
    lane = lax.broadcasted_iota(jnp.int32, (x.shape[0], LANES), 1)
    first = (lane % HEAD_DIM) < (HEAD_DIM // 2)
    outs = []
    for gidx in range(x.shape[1] // LANES):
        xg = x[:, gidx * LANES:(gidx + 1) * LANES]
        partner = jnp.where(first, pltpu.roll(xg, LANES - HEAD_DIM // 2, 1), pltpu.roll(xg, HEAD_DIM // 2, 1))
        outs.append(xg * cos + partner * sin_signed)
    return jnp.concatenate(outs, axis=1) if len(outs) > 1 else outs[0]


def _norm_mm_kernel(x_ref, sh_ref, sc_ref, g_ref, w_ref, *rest, n_rope_tiles, n_split):
    if n_rope_tiles:
        cos_ref, sin_ref = rest[:2]
        rest = rest[2:]
    o_refs = rest[:-1]
    h_scr = rest[-1]
    j = pl.program_id(2)

    @pl.when(j == 0)
    def _():
        h = _rmsnorm_rows(x_ref[0], g_ref[...]) * (1.0 + sc_ref[0]) + sh_ref[0]
        h_scr[...] = h.astype(bf16)

    acc = jnp.dot(h_scr[...], w_ref[...].astype(bf16), preferred_element_type=f32)
    if n_split:
        for jj in range(n_split):
            @pl.when(j == jj)
            def _(jj=jj):
                o_refs[jj][0] = _rope_tile(acc, cos_ref[...], sin_ref[...]) if jj < n_rope_tiles else acc
    elif n_rope_tiles:
        @pl.when(j < n_rope_tiles)
        def _():
            o_refs[0][0] = _rope_tile(acc, cos_ref[...], sin_ref[...])

        @pl.when(j >= n_rope_tiles)
        def _():
            o_refs[0][0] = acc
    else:
        o_refs[0][0] = acc


def _norm_matmul(x, shift, scale, g, w, *, tm, tn, rope=None, n_rope_cols=0, split=False):
    B, T, D = x.shape
    N = w.shape[1]
    per_row = shift.shape[1] != 1
    mod_spec = (pl.BlockSpec((1, tm, D), lambda b, i, j: (b, i, 0)) if per_row
                else pl.BlockSpec((1, 1, D), lambda b, i, j: (b, 0, 0)))
    in_specs = [pl.BlockSpec((1, tm, D), lambda b, i, j: (b, i, 0)), mod_spec, mod_spec,
                pl.BlockSpec((1, D), lambda b, i, j: (0, 0)),
                pl.BlockSpec((D, tn), lambda b, i, j: (0, j))]
    args = [x, shift, scale, g, w]
    n_rope_tiles = 0
    if rope is not None:
        cos, sin = rope
        n_rope_tiles = n_rope_cols // tn
        if cos.shape[0] == T:
            rspec = pl.BlockSpec((tm, LANES), lambda b, i, j: (i, 0))
        else:
            nt = T // tm
            rspec = pl.BlockSpec((tm, LANES), lambda b, i, j: (b * nt + i, 0))
        in_specs += [rspec, rspec]
        args += [cos, sin]
    n_split = N // tn if split else 0
    if split:
        out_specs = [pl.BlockSpec((1, tm, tn), lambda b, i, j: (b, i, 0))] * n_split
        out_shape = [jax.ShapeDtypeStruct((B, T, tn), f32)] * n_split
    else:
        out_specs = pl.BlockSpec((1, tm, tn), lambda b, i, j: (b, i, j))
        out_shape = jax.ShapeDtypeStruct((B, T, N), f32)
    return pl.pallas_call(
        functools.partial(_norm_mm_kernel, n_rope_tiles=n_rope_tiles, n_split=n_split),
        grid=(B, T // tm, N // tn),
        in_specs=in_specs,
        out_specs=out_specs,
        out_shape=out_shape,
        scratch_shapes=[pltpu.VMEM((tm, D), bf16)],
        compiler_params=_cparams(("parallel", "parallel", "arbitrary")),
        name="norm_matmul",
    )(*args)


def _mm_post_kernel(*refs, n_in):
    a_refs = refs[:n_in]
    w_refs = refs[n_in:2 * n_in]
    x_ref, gate_ref, g_ref, o_ref = refs[2 * n_in:]
    y = None
    for a_ref, w_ref in zip(a_refs, w_refs):
        part = jnp.dot(a_ref[0].astype(bf16), w_ref[...], preferred_element_type=f32)
        y = part if y is None else y + part
    o_ref[0] = x_ref[0] + gate_ref[0] * _rmsnorm_rows(y, g_ref[...])


def _matmul_post(a_list, w_list, x, gate, g, *, tm):
    B, T, D = x.shape
    per_row = gate.shape[1] != 1
    gate_spec = (pl.BlockSpec((1, tm, D), lambda b, i: (b, i, 0)) if per_row
                 else pl.BlockSpec((1, 1, D), lambda b, i: (b, 0, 0)))
    in_specs = []
    for a in a_list:
        ka = a.shape[2]
        in_specs.append(pl.BlockSpec((1, tm, ka), lambda b, i: (b, i, 0)))
    for w in w_list:
        in_specs.append(pl.BlockSpec(w.shape, lambda b, i: (0, 0)))
    in_specs += [pl.BlockSpec((1, tm, D), lambda b, i: (b, i, 0)), gate_spec,
                 pl.BlockSpec((1, D), lambda b, i: (0, 0))]
    return pl.pallas_call(
        functools.partial(_mm_post_kernel, n_in=len(a_list)),
        grid=(B, T // tm),
        in_specs=in_specs,
        out_specs=pl.BlockSpec((1, tm, D), lambda b, i: (b, i, 0)),
        out_shape=jax.ShapeDtypeStruct((B, T, D), f32),
        compiler_params=_cparams(("parallel", "parallel")),
        name="matmul_post",
    )(*a_list, *w_list, x, gate, g)


def _mlp_kernel(x_ref, sh_ref, sc_ref, gate_ref, gpre_ref, gpost_ref, wu_ref, wd_ref, o_ref, h_scr, acc_scr):
    k = pl.program_id(2)

    @pl.when(k == 0)
    def _():
        h = _rmsnorm_rows(x_ref[0], gpre_ref[...]) * (1.0 + sc_ref[0]) + sh_ref[0]
        h_scr[...] = h.astype(bf16)
        acc_scr[...] = jnp.zeros_like(acc_scr)

    u = jnp.dot(h_scr[...], wu_ref[...], preferred_element_type=f32)
    a = jnp.square(jnp.maximum(u, 0.0)).astype(bf16)
    acc_scr[...] += jnp.dot(a, wd_ref[...], preferred_element_type=f32)

    @pl.when(k == pl.num_programs(2) - 1)
    def _():
        o_ref[0] = x_ref[0] + gate_ref[0] * _rmsnorm_rows(acc_scr[...], gpost_ref[...])


def _mlp(x, shift, scale, gate, g_pre, g_post, w_up, w_down, *, tm, tf):
    B, T, D = x.shape
    per_row = shift.shape[1] != 1
    mod_spec = (pl.BlockSpec((1, tm, D), lambda b, i, k: (b, i, 0)) if per_row
                else pl.BlockSpec((1, 1, D), lambda b, i, k: (b, 0, 0)))
    vec_spec = pl.BlockSpec((1, D), lambda b, i, k: (0, 0))
    return pl.pallas_call(
        _mlp_kernel,
        grid=(B, T // tm, D_FF // tf),
        in_specs=[pl.BlockSpec((1, tm, D), lambda b, i, k: (b, i, 0)), mod_spec, mod_spec, mod_spec,
                  vec_spec, vec_spec,
                  pl.BlockSpec((D, tf), lambda b, i, k: (0, k)),
                  pl.BlockSpec((tf, D), lambda b, i, k: (k, 0))],
        out_specs=pl.BlockSpec((1, tm, D), lambda b, i, k: (b, i, 0)),
        out_shape=jax.ShapeDtypeStruct((B, T, D), f32),
        scratch_shapes=[pltpu.VMEM((tm, D), bf16), pltpu.VMEM((tm, D), f32)],
        compiler_params=_cparams(("parallel", "parallel", "arbitrary")),
        name="mlp",
    )(x, shift, scale, gate, g_pre, g_post, w_up, w_down)


def _rwkv_pre_kernel(p_ref, sh0_ref, mu_ref, w0_ref, w2_ref, a0_ref, a2_ref, g2_ref, kk_w_ref, ka_ref,
                     rk_ref, bd_ref, r_o, w_o, k_o, v_o, kk_o, kka_o, g_o, bon_o, carry):
    ti = pl.program_id(1)

    @pl.when(ti == 0)
    def _():
        carry[...] = sh0_ref[0]

    p = p_ref[0]
    tt = p.shape[0]
    row = lax.broadcasted_iota(jnp.int32, (tt, 1), 0)
    prev = jnp.where(row == 0, carry[...], pltpu.roll(p, 1, 0))
    carry[...] = p[tt - 1:tt, :]
    ps = p + mu_ref[...] * (prev - p)
    r = ps[:, 0:D_A]
    k = ps[:, D_A:2 * D_A]
    v = ps[:, 2 * D_A:3 * D_A]
    o = 3 * D_A
    dw = ps[:, o:o + LORA_W]
    da = ps[:, o + LORA_W:o + LORA_W + LORA_A]
    dg = ps[:, o + LORA_W + LORA_A:o + LORA_W + LORA_A + LORA_G]
    bd = bd_ref[...]
    wl = w0_ref[...] + jnp.dot(jnp.tanh(dw).astype(bf16), w2_ref[...], preferred_element_type=f32)
    w_log = -_softplus(-wl) - 0.5
    decay = jnp.exp(-jnp.exp(w_log))
    a = jax.nn.sigmoid(a0_ref[...] + jnp.dot(da.astype(bf16), a2_ref[...], preferred_element_type=f32))
    g = jnp.dot(jax.nn.sigmoid(dg).astype(bf16), g2_ref[...], preferred_element_type=f32)
    kk = k * kk_w_ref[...]
    nrm = jnp.sqrt(_seg_sum(kk * kk, bd))
    kk = kk / jnp.maximum(nrm, 1e-12)
    k2 = k * (1.0 + (a - 1.0) * ka_ref[...])
    bonus = _seg_sum(r * k2 * rk_ref[...], bd) * v
    r_o[0] = r
    w_o[0] = decay
    k_o[0] = k2
    v_o[0] = v
    kk_o[0] = kk
    kka_o[0] = kk * a
    g_o[0] = g
    bon_o[0] = bonus


def _rwkv_pre(proj, shift0, mu, w0, w2, a0, a2, g2, k_k, k_a, r_k, bd512, *, tt):
    B, T, _ = proj.shape
    vec = lambda n: pl.BlockSpec((1, n), lambda b, i: (0, 0))
    mat = lambda s: pl.BlockSpec(s, lambda b, i: (0, 0))
    out_spec = pl.BlockSpec((1, tt, D_A), lambda b, i: (b, i, 0))
    out_sds = jax.ShapeDtypeStruct((B, T, D_A), f32)
    return pl.pallas_call(
        _rwkv_pre_kernel,
        grid=(B, T // tt),
        in_specs=[pl.BlockSpec((1, tt, RW_COLS), lambda b, i: (b, i, 0)),
                  pl.BlockSpec((1, 1, RW_COLS), lambda b, i: (b, 0, 0)),
                  vec(RW_COLS), vec(D_A), mat((LORA_W, D_A)), vec(D_A), mat((LORA_A, D_A)),
                  mat((LORA_G, D_A)), vec(D_A), vec(D_A), vec(D_A), mat((D_A, D_A))],
        out_specs=[out_spec] * 8,
        out_shape=[out_sds] * 8,
        scratch_shapes=[pltpu.VMEM((1, RW_COLS), f32)],
        compiler_params=_cparams(("parallel", "arbitrary")),
        name="rwkv_pre",
    )(proj, shift0, mu, w0, w2, a0, a2, g2, k_k, k_a, r_k, bd512)


SCAN_NB = 8
PAIRS = H_A // 2


def _wkv_scan_kernel(r_ref, w_ref, k_ref, v_ref, kk_ref, kka_ref, s0_ref, bd_ref, e_ref,
                     y_ref, st_ref, s_scr):
    c = pl.program_id(1)
    n_steps = r_ref.shape[1]
    rows_per_b = PAIRS * HEAD_DIM

    @pl.when(c == 0)
    def _():
        for b in range(SCAN_NB):
            for p in range(PAIRS):
                r0 = (b * PAIRS + p) * HEAD_DIM
                s_scr[r0:r0 + HEAD_DIM, :] = jnp.concatenate([s0_ref[b, 2 * p], s0_ref[b, 2 * p + 1]], axis=1)

    bd = bd_ref[...]
    e4 = e_ref[...]

    def step8(t8, carry):
        t0 = pl.multiple_of(t8 * SUBLANES, SUBLANES)
        names = (("r", r_ref), ("w", w_ref), ("k", k_ref), ("v", v_ref), ("kk", kk_ref), ("kka", kka_ref))
        blks = [{name: ref[b, pl.ds(t0, SUBLANES), :] for name, ref in names} for b in range(SCAN_NB)]
        yrows = [[] for _ in range(SCAN_NB)]
        e_all = jnp.concatenate([e4] * SCAN_NB, axis=0)
        for j in range(SUBLANES):
            def rows(name):
                parts = [jnp.broadcast_to(blks[b][name][j:j + 1, LANES * p:LANES * (p + 1)], (HEAD_DIM, LANES))
                         for b in range(SCAN_NB) for p in range(PAIRS)]
                return jnp.concatenate(parts, axis=0)

            s = s_scr[...]
            sk = _seg_sum_paired(s * rows("kk"), bd)
            vcol = _seg_sum_paired(e_all * rows("v"), bd)
            s = s * rows("w") - sk * rows("kka") + vcol * rows("k")
            s_scr[...] = s
            yrep = _seg_sum_paired(s * rows("r"), bd)
            yrow = jnp.sum((e_all * yrep).reshape(SCAN_NB * PAIRS, HEAD_DIM, LANES), axis=1)
            for b in range(SCAN_NB):
                yrows[b].append(jnp.concatenate([yrow[b * PAIRS + p:b * PAIRS + p + 1, :] for p in range(PAIRS)],
                                                axis=1))
        for b in range(SCAN_NB):
            y_ref[b, pl.ds(t0, SUBLANES), :] = jnp.concatenate(yrows[b], axis=0)
        return carry

    lax.fori_loop(0, n_steps // SUBLANES, step8, 0)

    @pl.when(c == pl.num_programs(1) - 1)
    def _():
        for b in range(SCAN_NB):
            for p in range(PAIRS):
                r0 = (b * PAIRS + p) * HEAD_DIM
                tile = s_scr[r0:r0 + HEAD_DIM, :]
                st_ref[b, 2 * p] = tile[:, 0:HEAD_DIM]
                st_ref[b, 2 * p + 1] = tile[:, HEAD_DIM:]


def _wkv_scan(r, w, k, v, kk, kka, s0, bd256, e4, *, chunk):
    NB, T, _ = r.shape
    rows = SCAN_NB * PAIRS * HEAD_DIM
    xspec = pl.BlockSpec((SCAN_NB, chunk, D_A), lambda g, c: (g, c, 0))
    sspec = pl.BlockSpec((SCAN_NB, H_A, HEAD_DIM, HEAD_DIM), lambda g, c: (g, 0, 0, 0))
    return pl.pallas_call(
        _wkv_scan_kernel,
        grid=(NB // SCAN_NB, T // chunk),
        in_specs=[xspec] * 6 + [sspec,
                                pl.BlockSpec(bd256.shape, lambda g, c: (0, 0)),
                                pl.BlockSpec((PAIRS * HEAD_DIM, LANES), lambda g, c: (0, 0))],
        out_specs=[xspec, sspec],
        out_shape=[jax.ShapeDtypeStruct((NB, T, D_A), f32),
                   jax.ShapeDtypeStruct((NB, H_A, HEAD_DIM, HEAD_DIM), f32)],
        scratch_shapes=[pltpu.VMEM((rows, LANES), f32)],
        compiler_params=_cparams(("parallel", "arbitrary")),
        name="wkv_scan",
    )(r, w, k, v, kk, kka, s0, bd256, e4)


def _rwkv_post_kernel(y_ref, g_ref, bon_ref, lw_ref, lb_ref, bd_ref, o_ref):
    y = y_ref[0]
    bd = bd_ref[...]
    mean = _seg_sum(y, bd) * (1.0 / HEAD_DIM)
    d = y - mean
    var = _seg_sum(d * d, bd) * (1.0 / HEAD_DIM)
    yn = d * lax.rsqrt(var + LNX_EPS) * lw_ref[...] + lb_ref[...]
    o_ref[0] = (yn + bon_ref[0]) * g_ref[0]


def _rwkv_post(y, g, bonus, lnx_w, lnx_b, bd512, *, tt):
    B, T, _ = y.shape
    spec = pl.BlockSpec((1, tt, D_A), lambda b, i: (b, i, 0))
    vec = pl.BlockSpec((1, D_A), lambda b, i: (0, 0))
    return pl.pallas_call(
        _rwkv_post_kernel,
        grid=(B, T // tt),
        in_specs=[spec, spec, spec, vec, vec, pl.BlockSpec((D_A, D_A), lambda b, i: (0, 0))],
        out_specs=spec,
        out_shape=jax.ShapeDtypeStruct((B, T, D_A), f32),
        compiler_params=_cparams(("parallel", "parallel")),
        name="rwkv_post",
    )(y, g, bonus, lnx_w, lnx_b, bd512)


def _fox_gate_kernel(f_ref, b_ref, lf_ref, cum_ref):
    z = f_ref[0] + b_ref[...]
    lf = -_softplus(-z)
    lf_ref[0] = lf
    T = lf.shape[0]
    row = lax.broadcasted_iota(jnp.int32, (T, 1), 0)
    x = lf
    s = 1
    while s < T:
        x = x + jnp.where(row >= s, pltpu.roll(x, s, 0), 0.0)
        s *= 2
    cum_ref[0] = x


def _fox_gate(proj, b_f_pad):
    B, T, _ = proj.shape
    spec = pl.BlockSpec((1, T, LANES), lambda b: (b, 0, 0))
    sds = jax.ShapeDtypeStruct((B, T, LANES), f32)
    return pl.pallas_call(
        _fox_gate_kernel,
        grid=(B,),
        in_specs=[pl.BlockSpec((1, T, LANES), lambda b: (b, 0, EV_F_OFF // LANES)),
                  pl.BlockSpec((1, LANES), lambda b: (0, 0))],
        out_specs=[spec, spec],
        out_shape=[sds, sds],
        compiler_params=_cparams(("parallel",)),
        name="fox_gate",
    )(proj, b_f_pad)


FLASH_HG = 4


def _flash_kernel(q_ref, k_ref, v_ref, l2c_ref, *rest, n_heads, tq, tk, has_bias, table_everywhere, copy_kv):
    if has_bias:
        cq_ref, ckt_ref = rest[:2]
        rest = rest[2:]
    o_ref = rest[0]
    rest = rest[1:]
    qi = pl.program_id(1)
    if copy_kv:
        ko_ref, vo_ref = rest[:2]
        rest = rest[2:]

        @pl.when(qi == 0)
        def _():
            ko_ref[0] = k_ref[0]
            vo_ref[0] = v_ref[0]
    if has_bias:
        m_scr, acc_scr, qb_scr, cq2_scr = rest
    else:
        m_scr, acc_scr, qb_scr = rest
    m_scr[...] = jnp.full_like(m_scr, -jnp.inf)
    acc_scr[...] = jnp.zeros_like(acc_scr)
    low_q = lax.broadcasted_iota(jnp.int32, (tq, LANES), 1) < HEAD_DIM
    low_k = lax.broadcasted_iota(jnp.int32, (tk, LANES), 1) < HEAD_DIM
    for h in range(n_heads):
        q_pair = q_ref[0, :, (h // 2) * LANES:(h // 2 + 1) * LANES] * (ATTN_SCALE * LOG2E)
        qb_scr[h] = jnp.where(low_q == (h % 2 == 0), q_pair, 0.0).astype(bf16)
        if has_bias:
            cq2_scr[h] = jnp.broadcast_to(cq_ref[0, :, h:h + 1] * LOG2E, (tq, LANES))
    nt = (((1,), (1,)), ((), ()))

    def block(it, use_table):
        k0 = pl.multiple_of((qi - it) * tk, tk)
        for g0 in range(0, n_heads, FLASH_HG):
            heads = range(g0, g0 + FLASH_HG)
            pairs = range(g0 // 2, (g0 + FLASH_HG) // 2)
            parts = []
            for p in pairs:
                k_pair = k_ref[0, pl.ds(k0, tk), p * LANES:(p + 1) * LANES].astype(bf16)
                q2 = qb_scr[2 * p:2 * p + 2].reshape(2 * tq, LANES)
                s2 = lax.dot_general(q2, k_pair, nt, preferred_element_type=f32)
                for n in range(2):
                    s = s2[n * tq:(n + 1) * tq]
                    if has_bias:
                        s = s - ckt_ref[0, 2 * p + n:2 * p + n + 1, pl.ds(k0, tk)] * LOG2E
                    parts.append(s)
            s3 = jnp.stack(parts)
            if use_table:
                s3 = s3 + l2c_ref[it][None]
            m_prev = m_scr[g0:g0 + FLASH_HG]
            m_cur = jnp.max(s3, axis=2, keepdims=True)
            if has_bias:
                cq2 = cq2_scr[g0:g0 + FLASH_HG]
                m_next = jnp.maximum(m_prev, m_cur + cq2)
                m_sub = m_next - cq2
            else:
                m_next = jnp.maximum(m_prev, m_cur)
                m_sub = m_next
            p3 = jnp.exp2(s3 - jnp.concatenate([m_sub] * (tk // LANES), axis=2)).astype(bf16)
            alpha = jnp.exp2(m_prev - m_next)
            m_scr[g0:g0 + FLASH_HG] = m_next
            v_pairs = {p: v_ref[0, pl.ds(k0, tk), p * LANES:(p + 1) * LANES].astype(bf16) for p in pairs}
            for n, h in enumerate(heads):
                v_ext = jnp.where(low_k == (h % 2 == 0), v_pairs[h // 2], 1.0)
                acc_scr[h] = acc_scr[h] * alpha[n] + jnp.dot(p3[n], v_ext, preferred_element_type=f32)

    block(0, True)

    def body(it, carry):
        block(it, table_everywhere)
        return carry

    lax.fori_loop(1, qi + 1, body, 0)
    for p in range(n_heads // 2):
        acc_e, acc_o = acc_scr[2 * p], acc_scr[2 * p + 1]
        out_e = acc_e / pltpu.roll(acc_e, HEAD_DIM, 1)
        out_o = acc_o / pltpu.roll(acc_o, HEAD_DIM, 1)
        o_ref[0, :, p * LANES:(p + 1) * LANES] = jnp.where(low_q, out_e, out_o)


def _flash_attention(q_src, k_src, v_src, q_blk, k_blk, v_blk, l2c, n_heads, *, tq, tk, bias=None,
                     table_everywhere, copy_kv=False):
    assert tq == tk
    B, T, _ = q_src.shape
    hd = n_heads * HEAD_DIM
    in_specs = [pl.BlockSpec((1, tq, hd), lambda b, i: (b, i, q_blk)),
                pl.BlockSpec((1, T, hd), lambda b, i: (b, 0, k_blk)),
                pl.BlockSpec((1, T, hd), lambda b, i: (b, 0, v_blk)),
                pl.BlockSpec(l2c.shape, lambda b, i: (0, 0, 0))]
    args = [q_src, k_src, v_src, l2c]
    if bias is not None:
        cum, cum_t = bias
        in_specs += [pl.BlockSpec((1, tq, LANES), lambda b, i: (b, i, 0)),
                     pl.BlockSpec((1, cum_t.shape[1], T), lambda b, i: (b, 0, 0))]
        args += [cum, cum_t]
    scratch = [pltpu.VMEM((n_heads, tq, LANES), f32), pltpu.VMEM((n_heads, tq, LANES), f32),
               pltpu.VMEM((n_heads, tq, LANES), bf16)]
    if bias is not None:
        scratch.append(pltpu.VMEM((n_heads, tq, LANES), f32))
    out_specs = pl.BlockSpec((1, tq, hd), lambda b, i: (b, i, 0))
    out_shape = jax.ShapeDtypeStruct((B, T, hd), f32)
    if copy_kv:
        kv_out = pl.BlockSpec((1, T, hd), lambda b, i: (b, 0, 0))
        out_specs = [out_specs, kv_out, kv_out]
        out_shape = [out_shape] * 3
    return pl.pallas_call(
        functools.partial(_flash_kernel, n_heads=n_heads, tq=tq, tk=tk, has_bias=bias is not None,
                          table_everywhere=table_everywhere, copy_kv=copy_kv),
        grid=(B, T // tq),
        in_specs=in_specs,
        out_specs=out_specs,
        out_shape=out_shape,
        scratch_shapes=scratch,
        compiler_params=_cparams(("parallel", "arbitrary")),
        name="flash_attention",
    )(*args)


def _qk(q_h, k_h):
    return lax.dot_general(q_h, k_h, (((1,), (1,)), ((), ())), preferred_element_type=f32)


FOX_PP = 32
DIL_KC = 1024


def _block_diag_queries(q2, n_heads):
    nq, hd = q2.shape
    qt = jnp.concatenate([q2] * n_heads, axis=0)
    rh = lax.broadcasted_iota(jnp.int32, (n_heads * nq, hd), 0) // nq
    ch = lax.broadcasted_iota(jnp.int32, (n_heads * nq, hd), 1) // HEAD_DIM
    return jnp.where(rh == ch, qt, 0.0).astype(bf16)


def _rows_per_head(x, nq):
    n_heads, n = x.shape
    return jnp.broadcast_to(x[:, None, :], (n_heads, nq, n)).reshape(n_heads * nq, n)


def _decode_first(s, v_new, m_scr, l_scr, accn_scr, acct_scr):
    m = jnp.max(s, axis=1, keepdims=True)
    p = jnp.exp2(s - m)
    m_scr[...] = jnp.broadcast_to(m, m_scr.shape)
    l_scr[...] = jnp.broadcast_to(jnp.sum(p, axis=1, keepdims=True), l_scr.shape)
    accn_scr[...] = jnp.dot(p.astype(bf16), v_new, preferred_element_type=f32)
    acct_scr[...] = jnp.zeros_like(acct_scr)


def _decode_update(s, vt_all, m_scr, l_scr, accn_scr, acct_scr):
    R = s.shape[0]
    hd = vt_all.shape[0]
    m_prev = m_scr[...]
    m_next = jnp.maximum(m_prev, jnp.max(s, axis=1, keepdims=True))
    p = jnp.exp2(s - m_next[:, 0:1])
    alpha = jnp.exp2(m_prev - m_next)
    l_scr[...] = alpha * l_scr[...] + jnp.sum(p, axis=1, keepdims=True)
    m_scr[...] = m_next
    accn_scr[...] = accn_scr[...] * jnp.concatenate([alpha] * (hd // LANES), axis=1)
    new_t = lax.dot_general(vt_all, p.astype(bf16), (((1,), (1,)), ((), ())), preferred_element_type=f32)
    if R < LANES:
        alpha = jnp.concatenate([alpha, jnp.zeros((LANES - R, LANES), f32)], axis=0)
    alpha_row = jnp.transpose(alpha)[0:1, 0:R]
    acct_scr[:, 0:R] = acct_scr[:, 0:R] * alpha_row + new_t


def _decode_finish(o_ref, n_heads, nq, l_scr, accn_scr, acct_scr):
    for h in range(n_heads):
        blk = acct_scr[h * HEAD_DIM:(h + 1) * HEAD_DIM, :]
        sq = jnp.transpose(jnp.concatenate([blk, jnp.zeros((LANES - HEAD_DIM, LANES), f32)], axis=0))
        rows = slice(h * nq, (h + 1) * nq)
        cols = slice(h * HEAD_DIM, (h + 1) * HEAD_DIM)
        num = sq[rows, 0:HEAD_DIM] + accn_scr[rows, cols]
        o_ref[0, :, cols] = num / l_scr[rows, 0:HEAD_DIM]


def _fox_decode2_kernel(pt_ref, q_ref, kn_ref, vn_ref, cqr_ref, cnt_ref, *rest):
    lf_refs = rest[:FOX_PP]
    k_refs = rest[FOX_PP:2 * FOX_PP]
    v_refs = rest[2 * FOX_PP:3 * FOX_PP]
    o_ref, m_scr, l_scr, accn_scr, acct_scr, tot_scr, qbd_scr = rest[3 * FOX_PP:]
    j = pl.program_id(1)
    nq = q_ref.shape[1]
    R = H_B * nq
    cq2 = cqr_ref[0] * LOG2E

    @pl.when(j == 0)
    def _():
        qbd = _block_diag_queries(q_ref[0] * (ATTN_SCALE * LOG2E), H_B)
        qbd_scr[...] = qbd
        tot_scr[...] = jnp.zeros_like(tot_scr)
        s = _qk(qbd, kn_ref[0].astype(bf16))
        s = s + cq2[:, 0:nq] - _rows_per_head(cnt_ref[0][:, 0:nq] * LOG2E, nq)
        qi = lax.broadcasted_iota(jnp.int32, (R, nq), 0) % nq
        kj = lax.broadcasted_iota(jnp.int32, (R, nq), 1)
        s = jnp.where(kj <= qi, s, -jnp.inf)
        _decode_first(s, vn_ref[0].astype(bf16), m_scr, l_scr, accn_scr, acct_scr)

    lane = lax.broadcasted_iota(jnp.int32, (H_B, PAGE_SIZE), 1)
    qbd = qbd_scr[...]
    tot = tot_scr[...]
    s_parts = []
    v_parts = []
    for i in range(FOX_PP):
        x = lf_refs[i][0]
        sft = 1
        while sft < PAGE_SIZE:
            x = x + jnp.where(lane >= sft, pltpu.roll(x, sft, 1), 0.0)
            sft *= 2
        page_tot = x[:, PAGE_SIZE - 1:PAGE_SIZE]
        bias2 = _rows_per_head((page_tot - x + tot) * LOG2E, nq) + cq2
        tot = tot + page_tot
        kt_all = k_refs[i][0].reshape(D_B, PAGE_SIZE).astype(bf16)
        s_parts.append(jnp.dot(qbd, kt_all, preferred_element_type=f32) + bias2)
        v_parts.append(v_refs[i][0].reshape(D_B, PAGE_SIZE).astype(bf16))
    tot_scr[...] = tot
    _decode_update(jnp.concatenate(s_parts, axis=1), jnp.concatenate(v_parts, axis=1),
                   m_scr, l_scr, accn_scr, acct_scr)

    @pl.when(j == pl.num_programs(1) - 1)
    def _():
        _decode_finish(o_ref, H_B, nq, l_scr, accn_scr, acct_scr)


def _fox_decode2(page_table, proj, cum_rows, cum_new_t, lf_cache_t, k_cache, v_cache):
    NB, nq, _ = proj.shape
    n_pages = page_table.shape[1]
    steps = n_pages // FOX_PP
    R = H_B * nq

    def lf_spec(i):
        return pl.BlockSpec((1, H_B, PAGE_SIZE),
                            lambda b, j, pt: (pt[b, n_pages - 1 - (j * FOX_PP + i)], 0, 0))

    def kv_spec(i):
        return pl.BlockSpec((1, H_B, HEAD_DIM, PAGE_SIZE),
                            lambda b, j, pt: (pt[b, n_pages - 1 - (j * FOX_PP + i)], 0, 0, 0))

    in_specs = [pl.BlockSpec((1, nq, D_B), lambda b, j, pt: (b, 0, EV_Q_OFF // D_B)),
                pl.BlockSpec((1, nq, D_B), lambda b, j, pt: (b, 0, EV_K_OFF // D_B)),
                pl.BlockSpec((1, nq, D_B), lambda b, j, pt: (b, 0, EV_V_OFF // D_B)),
                pl.BlockSpec((1, R, LANES), lambda b, j, pt: (b, 0, 0)),
                pl.BlockSpec((1, SUBLANES, LANES), lambda b, j, pt: (b, 0, 0))]
    args = [proj, proj, proj, cum_rows, cum_new_t]
    for i in range(FOX_PP):
        in_specs.append(lf_spec(i))
        args.append(lf_cache_t)
    for cache in (k_cache, v_cache):
        for i in range(FOX_PP):
            in_specs.append(kv_spec(i))
            args.append(cache)
    grid_spec = pltpu.PrefetchScalarGridSpec(
        num_scalar_prefetch=1,
        grid=(NB, steps),
        in_specs=in_specs,
        out_specs=pl.BlockSpec((1, nq, D_B), lambda b, j, pt: (b, 0, 0)),
        scratch_shapes=[pltpu.VMEM((R, LANES), f32), pltpu.VMEM((R, LANES), f32),
                        pltpu.VMEM((R, D_B), f32), pltpu.VMEM((D_B, LANES), f32),
                        pltpu.VMEM((H_B, LANES), f32), pltpu.VMEM((R, D_B), bf16)],
    )
    return pl.pallas_call(
        _fox_decode2_kernel,
        grid_spec=grid_spec,
        out_shape=jax.ShapeDtypeStruct((NB, nq, D_B), f32),
        compiler_params=_cparams(("parallel", "arbitrary")),
        name="fox_decode",
    )(page_table, *args)


def _dil_decode2_kernel(q_ref, kn_ref, vn_ref, l2new_ref, l2past_ref, k_ref, v_ref, o_ref,
                        m_scr, l_scr, accn_scr, acct_scr, qbd_scr):
    j = pl.program_id(1)
    nq = q_ref.shape[1]

    @pl.when(j == 0)
    def _():
        qbd = _block_diag_queries(q_ref[0] * (ATTN_SCALE * LOG2E), H_C)
        qbd_scr[...] = qbd
        s = _qk(qbd, kn_ref[0].astype(bf16)) + jnp.concatenate([l2new_ref[:, 0:nq]] * H_C, axis=0)
        _decode_first(s, vn_ref[0].astype(bf16), m_scr, l_scr, accn_scr, acct_scr)

    kt_all = k_ref[0].reshape(D_C, DIL_KC).astype(bf16)
    s = jnp.dot(qbd_scr[...], kt_all, preferred_element_type=f32)
    s = s + jnp.concatenate([l2past_ref[...]] * H_C, axis=0)
    _decode_update(s, v_ref[0].reshape(D_C, DIL_KC).astype(bf16), m_scr, l_scr, accn_scr, acct_scr)

    @pl.when(j == pl.num_programs(1) - 1)
    def _():
        _decode_finish(o_ref, H_C, nq, l_scr, accn_scr, acct_scr)


def _dil_decode2(qkv, l2_new, l2_past, k_state, v_state):
    NB, nq, _ = qkv.shape
    L = k_state.shape[3]
    R = H_C * nq
    kv_spec = pl.BlockSpec((1, H_C, HEAD_DIM, DIL_KC), lambda b, j: (b, 0, 0, j))
    return pl.pallas_call(
        _dil_decode2_kernel,
        grid=(NB, L // DIL_KC),
        in_specs=[pl.BlockSpec((1, nq, D_C), lambda b, j: (b, 0, 0)),
                  pl.BlockSpec((1, nq, D_C), lambda b, j: (b, 0, 1)),
                  pl.BlockSpec((1, nq, D_C), lambda b, j: (b, 0, 2)),
                  pl.BlockSpec(l2_new.shape, lambda b, j: (0, 0)),
                  pl.BlockSpec((nq, DIL_KC), lambda b, j: (0, j)),
                  kv_spec, kv_spec],
        out_specs=pl.BlockSpec((1, nq, D_C), lambda b, j: (b, 0, 0)),
        out_shape=jax.ShapeDtypeStruct((NB, nq, D_C), f32),
        scratch_shapes=[pltpu.VMEM((R, LANES), f32), pltpu.VMEM((R, LANES), f32),
                        pltpu.VMEM((R, D_C), f32), pltpu.VMEM((D_C, LANES), f32),
                        pltpu.VMEM((R, D_C), bf16)],
        compiler_params=_cparams(("parallel", "arbitrary")),
        name="dil_decode",
    )(qkv, qkv, qkv, l2_new, l2_past, k_state, v_state)


def _dil_multiplicity(delta):
    delta = np.asarray(delta)
    c = np.zeros(delta.shape, np.float32)
    for w, d in DIL_PAIRS:
        c += ((delta >= 0) & (delta % d == 0) & (delta <= w)).astype(np.float32)
    return c


def _log2_block_tables(n_blocks, t, mult_fn):
    i = np.arange(t)[:, None]
    j = np.arange(t)[None, :]
    with np.errstate(divide="ignore"):
        tabs = [np.log2(mult_fn(i - j + d * t)).astype(np.float32) for d in range(n_blocks)]
    return jnp.asarray(np.stack(tabs))


def _rope_tables(pos):
    half = HEAD_DIM // 2
    inv = ROPE_THETA ** (-jnp.arange(half, dtype=f32) / half)
    ang = pos.astype(f32)[:, None] * inv[None, :]
    cos = jnp.cos(ang)
    sin = jnp.sin(ang)
    return jnp.tile(cos, (1, 4)), jnp.concatenate([-sin, sin, -sin, sin], axis=1)


def _mods(mod_lj, n_prompt, per_row_repeat):
    mp = mod_lj[:n_prompt][:, None, :]
    ms = jnp.repeat(mod_lj[n_prompt:], per_row_repeat, axis=0)[None]
    sp = [mp[..., i * D_MODEL:(i + 1) * D_MODEL] for i in range(3)]
    ss = [ms[..., i * D_MODEL:(i + 1) * D_MODEL] for i in range(3)]
    return sp, ss


def kernel(x_prompt, x_sample, cache_fox_k, cache_fox_v, cache_fox_logf, state_rwkv_wkv, state_rwkv_shift,
           state_dil_k, state_dil_v, page_table, c_prompt, c_sample, ada_w, ada_b, norm_pre, norm_post,
           mlp_up, mlp_down, ev_w_in, ev_w_out, rw_mu, rw_w0, rw_w2, rw_a0, rw_a2, rw_g2, rw_k_k, rw_k_a,
           rw_r_k, rw_lnx_w, rw_lnx_b, fox_b_f, od_w_in, od_w_out):
    BP, T, D = x_prompt.shape
    BS, TS, _ = x_sample.shape
    depth = ada_w.shape[0]
    past_len = page_table.shape[1] * PAGE_SIZE

    mod = _ada_mod(jnp.concatenate([c_prompt, c_sample], axis=0), ada_w, ada_b)

    idx = np.arange(LANES)
    idx2 = np.arange(2 * LANES)
    bd256 = jnp.asarray((idx2[:, None] // HEAD_DIM == idx2[None, :] // HEAD_DIM).astype(np.float32)).astype(bf16)
    idx5 = np.arange(D_A)
    bd512 = jnp.asarray((idx5[:, None] // HEAD_DIM == idx5[None, :] // HEAD_DIM).astype(np.float32)).astype(bf16)
    e4 = jnp.asarray(np.tile((idx[None, :] % HEAD_DIM == np.arange(HEAD_DIM)[:, None]).astype(np.float32),
                             (PAIRS, 1)))
    TQ = 256
    causal_tab = _log2_block_tables(1, TQ, lambda dl: (dl >= 0).astype(np.float32))
    dil_tab = _log2_block_tables(T // TQ, TQ, _dil_multiplicity)

    xp = x_prompt
    xs = x_sample.reshape(1, BS * TS, D)
    outs = {}
    for l in range(depth):
        i = l // 2
        (sh_p, sc_p, gt_p), (sh_s, sc_s, gt_s) = _mods(mod[2 * l], BP, TS)
        g_pre = norm_pre[l, 0][None]
        g_post = norm_post[l, 0][None]
        if l % 2 == 0:
            w_in = ev_w_in[i]
            zeros = lambda n: jnp.zeros((D, n), f32)
            w_pad = jnp.concatenate([w_in[:, :RW_COLS], zeros(EV_Q_OFF - RW_COLS), w_in[:, RW_COLS:],
                                     zeros(EV_COLS_PAD - EV_F_OFF - H_B)], axis=1).astype(bf16)
            w_out = ev_w_out[i].astype(bf16)
            b_f_pad = jnp.pad(fox_b_f[i], (0, LANES - H_B))[None]
            rw_args = (rw_mu[i][None], rw_w0[i][None], rw_w2[i].astype(bf16), rw_a0[i][None],
                       rw_a2[i].astype(bf16), rw_g2[i].astype(bf16), rw_k_k[i][None], rw_k_a[i][None],
                       rw_r_k[i].reshape(1, D_A), bd512)

            proj_p = _norm_matmul(xp, sh_p, sc_p, g_pre, w_pad, tm=1024, tn=1280)
            r, w, k2, v, kk, kka, g, bonus = _rwkv_pre(proj_p, jnp.zeros((BP, 1, RW_COLS), f32), *rw_args, tt=256)
            y, st = _wkv_scan(r, w, k2, v, kk, kka, jnp.zeros((BP, H_A, HEAD_DIM, HEAD_DIM), f32), bd256, e4,
                              chunk=64)
            ya = _rwkv_post(y, g, bonus, rw_lnx_w[i][None], rw_lnx_b[i][None], bd512, tt=512)
            lf_p, cum_p = _fox_gate(proj_p, b_f_pad)
            cum_t = jnp.transpose(cum_p[:, :, :SUBLANES], (0, 2, 1))
            yb, fk, fv = _flash_attention(proj_p, proj_p, proj_p, EV_Q_OFF // D_B, EV_K_OFF // D_B,
                                          EV_V_OFF // D_B, causal_tab, H_B, tq=TQ, tk=TQ, bias=(cum_p, cum_t),
                                          table_everywhere=False, copy_kv=True)
            xp = _matmul_post([ya, yb], [w_out[:D_A], w_out[D_A:]], xp, gt_p, g_post, tm=512)
            outs.setdefault("fk_p", []).append(fk.reshape(BP, T, H_B, HEAD_DIM))
            outs.setdefault("fv_p", []).append(fv.reshape(BP, T, H_B, HEAD_DIM))
            outs.setdefault("fl_p", []).append(lf_p[:, :, :H_B])
            outs.setdefault("wkv_p", []).append(st)
            outs.setdefault("sh_p", []).append(proj_p[:, T - 1, :RW_COLS])

            proj_s = _norm_matmul(xs, sh_s, sc_s, g_pre, w_pad, tm=BS * TS, tn=1280).reshape(BS, TS, EV_COLS_PAD)
            r, w, k2, v, kk, kka, g, bonus = _rwkv_pre(proj_s, state_rwkv_shift[i][:, None, :], *rw_args, tt=TS)
            y, st = _wkv_scan(r, w, k2, v, kk, kka, state_rwkv_wkv[i], bd256, e4, chunk=TS)
            ya = _rwkv_post(y, g, bonus, rw_lnx_w[i][None], rw_lnx_b[i][None], bd512, tt=TS)
            lf_s, cum_s = _fox_gate(proj_s, b_f_pad)
            cum_s_t = jnp.pad(jnp.transpose(cum_s[:, :, :SUBLANES], (0, 2, 1)), ((0, 0), (0, 0), (0, LANES - TS)))
            lf_cache_t = jnp.transpose(cache_fox_logf[i], (0, 2, 1))
            cum_rows = jnp.broadcast_to(
                jnp.transpose(cum_s[:, :, :H_B], (0, 2, 1)).reshape(BS, H_B * TS, 1), (BS, H_B * TS, LANES))
            yb = _fox_decode2(page_table, proj_s, cum_rows, cum_s_t, lf_cache_t,
                              jnp.transpose(cache_fox_k[i], (0, 2, 3, 1)),
                              jnp.transpose(cache_fox_v[i], (0, 2, 3, 1)))
            xs = _matmul_post([ya.reshape(1, BS * TS, D_A), yb.reshape(1, BS * TS, D_B)],
                              [w_out[:D_A], w_out[D_A:]], xs, gt_s, g_post, tm=BS * TS)
            outs.setdefault("fk_s", []).append(proj_s[:, :, EV_K_OFF:EV_K_OFF + D_B].reshape(BS, TS, H_B, HEAD_DIM))
            outs.setdefault("fv_s", []).append(proj_s[:, :, EV_V_OFF:EV_V_OFF + D_B].reshape(BS, TS, H_B, HEAD_DIM))
            outs.setdefault("fl_s", []).append(lf_s[:, :, :H_B])
            outs.setdefault("wkv_s", []).append(st)
            outs.setdefault("sh_s", []).append(proj_s[:, TS - 1, :RW_COLS])
        else:
            w_in = od_w_in[i].astype(bf16)
            w_out = od_w_out[i].astype(bf16)
            rope_p = _rope_tables(jnp.arange(T))
            cos_s, sin_s = _rope_tables(past_len + jnp.arange(TS))
            rope_s = (jnp.tile(cos_s, (BS, 1)), jnp.tile(sin_s, (BS, 1)))

            q_p, k_p, v_p = _norm_matmul(xp, sh_p, sc_p, g_pre, w_in, tm=1024, tn=D_C, rope=rope_p,
                                         n_rope_cols=2 * D_C, split=True)
            o_p = _flash_attention(q_p, k_p, v_p, 0, 0, 0, dil_tab, H_C, tq=TQ, tk=TQ, table_everywhere=True)
            xp = _matmul_post([o_p], [w_out], xp, gt_p, g_post, tm=512)
            keep = min(DIL_PAIRS[-1][0], T)
            outs.setdefault("dk_p", []).append(k_p[:, T - keep:].reshape(BP, keep, H_C, HEAD_DIM))
            outs.setdefault("dv_p", []).append(v_p[:, T - keep:].reshape(BP, keep, H_C, HEAD_DIM))

            qkv_s = _norm_matmul(xs, sh_s, sc_s, g_pre, w_in, tm=BS * TS, tn=D_C, rope=rope_s,
                                 n_rope_cols=2 * D_C).reshape(BS, TS, 3 * D_C)
            L = state_dil_k.shape[2]
            tq_idx = np.arange(TS)[:, None]
            c_new = np.zeros((TS, LANES), np.float32)
            c_new[:, :TS] = _dil_multiplicity(tq_idx - np.arange(TS)[None, :])
            c_past = _dil_multiplicity(L + tq_idx - np.arange(L)[None, :])
            with np.errstate(divide="ignore"):
                l2_new, l2_past = jnp.asarray(np.log2(c_new)), jnp.asarray(np.log2(c_past))
            o_s = _dil_decode2(qkv_s, l2_new, l2_past,
                               jnp.transpose(state_dil_k[i], (0, 2, 3, 1)),
                               jnp.transpose(state_dil_v[i], (0, 2, 3, 1)))
            xs = _matmul_post([o_s.reshape(1, BS * TS, D_C)], [w_out], xs, gt_s, g_post, tm=BS * TS)
            outs.setdefault("dk_s", []).append(qkv_s[:, :, D_C:2 * D_C].reshape(BS, TS, H_C, HEAD_DIM))
            outs.setdefault("dv_s", []).append(qkv_s[:, :, 2 * D_C:].reshape(BS, TS, H_C, HEAD_DIM))

        (sh_p, sc_p, gt_p), (sh_s, sc_s, gt_s) = _mods(mod[2 * l + 1], BP, TS)
        g_pre = norm_pre[l, 1][None]
        g_post = norm_post[l, 1][None]
        w_up = mlp_up[l].astype(bf16)
        w_down = mlp_down[l].astype(bf16)
        xp = _mlp(xp, sh_p, sc_p, gt_p, g_pre, g_post, w_up, w_down, tm=1024, tf=1024)
        xs = _mlp(xs, sh_s, sc_s, gt_s, g_pre, g_post, w_up, w_down, tm=BS * TS, tf=1024)

    st = lambda name: jnp.stack(outs[name])
    return (xp, xs.reshape(BS, TS, D), st("fk_p"), st("fv_p"), st("fl_p"), st("fk_s"), st("fv_s"), st("fl_s"),
            st("wkv_p"), st("wkv_s"), st("sh_p"), st("sh_s"), st("dk_p"), st("dv_p"), st("dk_s"), st("dv_s"))
```

```python
import functools

import numpy as np
import jax
import jax.numpy as jnp
from jax import lax
from jax.experimental import pallas as pl
from jax.experimental.pallas import tpu as pltpu

f32 = jnp.float32
bf16 = jnp.bfloat16

D_MODEL = 1024
HEAD_DIM = 64
H_A = 8
H_B = 8
H_C = 16
D_A = H_A * HEAD_DIM
D_B = H_B * HEAD_DIM
D_C = H_C * HEAD_DIM
LORA_W = 64
LORA_A = 64
LORA_G = 128
RW_COLS = 3 * D_A + LORA_W + LORA_A + LORA_G
D_FF = 4 * D_MODEL
PAGE_SIZE = 128
DIL_PAIRS = ((128, 1), (512, 4), (2048, 16))
ROPE_THETA = 10000.0
NORM_EPS = 1e-6
LNX_EPS = 64e-5
ATTN_SCALE = HEAD_DIM ** -0.5
LOG2E = 1.4426950408889634

LANES = 128
SUBLANES = 8
VMEM_LIMIT = 56 * 1024 * 1024

EV_Q_OFF = 2048
EV_K_OFF = 2560
EV_V_OFF = 3072
EV_F_OFF = 3584
EV_COLS_PAD = 3840


def _cparams(sem):
    return pltpu.CompilerParams(dimension_semantics=sem, vmem_limit_bytes=VMEM_LIMIT)


def _split_hi_lo(x):
    hi = x.astype(bf16)
    lo = (x - hi.astype(f32)).astype(bf16)
    return hi, lo


def _seg_sum(x, bd):
    hi, lo = _split_hi_lo(x)
    return (jnp.dot(hi, bd, preferred_element_type=f32) + jnp.dot(lo, bd, preferred_element_type=f32))


def _seg_sum_paired(x, bd2):
    n = x.shape[0] // 2
    xb = x.astype(bf16)
    out = jnp.dot(jnp.concatenate([xb[:n], xb[n:]], axis=1), bd2, preferred_element_type=f32)
    return jnp.concatenate([out[:, :LANES], out[:, LANES:]], axis=0)


def _rmsnorm_rows(x, g):
    ms = jnp.mean(x * x, axis=-1, keepdims=True)
    return x * lax.rsqrt(ms + NORM_EPS) * g


def _softplus(z):
    return jnp.maximum(z, 0.0) + jnp.log(1.0 + jnp.exp(-jnp.abs(z)))


def _ada_kernel(c_ref, w_ref, b_ref, o_ref):
    c = c_ref[...]
    s = (c * jax.nn.sigmoid(c)).astype(bf16)
    o_ref[0] = jnp.dot(s, w_ref[0].astype(bf16), preferred_element_type=f32) + b_ref[0]


def _ada_mod(c_all, ada_w, ada_b):
    n = c_all.shape[0]
    nl = ada_w.shape[0] * ada_w.shape[1]
    w = ada_w.reshape(nl, D_MODEL, 3 * D_MODEL)
    b = ada_b.reshape(nl, 1, 3 * D_MODEL)
    tn = 1024
    return pl.pallas_call(
        _ada_kernel,
        grid=(nl, 3 * D_MODEL // tn),
        in_specs=[pl.BlockSpec((n, D_MODEL), lambda l, j: (0, 0)),
                  pl.BlockSpec((1, D_MODEL, tn), lambda l, j: (l, 0, j)),
                  pl.BlockSpec((1, 1, tn), lambda l, j: (l, 0, j))],
        out_specs=pl.BlockSpec((1, n, tn), lambda l, j: (l, 0, j)),
        out_shape=jax.ShapeDtypeStruct((nl, n, 3 * D_MODEL), f32),
        compiler_params=_cparams(("parallel", "parallel")),
        name="ada_mod",
    )(c_all, w, b)


def _rope_tile(x, cos, sin_signed):
    lane = lax.broadcasted_iota(jnp.int32, (x.shape[0], LANES), 1)
    first = (lane % HEAD_DIM) < (HEAD_DIM // 2)
    outs = []
    for gidx in range(x.shape[1] // LANES):
        xg = x[:, gidx * LANES:(gidx + 1) * LANES]
        partner = jnp.where(first, pltpu.roll(xg, LANES - HEAD_DIM // 2, 1), pltpu.roll(xg, HEAD_DIM // 2, 1))
        outs.append(xg * cos + partner * sin_signed)
    return jnp.concatenate(outs, axis=1) if len(outs) > 1 else outs[0]


def _norm_mm_kernel(x_ref, sh_ref, sc_ref, g_ref, w_ref, *rest, n_rope_tiles, n_split):
    if n_rope_tiles:
        cos_ref, sin_ref = rest[:2]
        rest = rest[2:]
    o_refs = rest[:-1]
    h_scr = rest[-1]
    j = pl.program_id(2)

    @pl.when(j == 0)
    def _():
        h = _rmsnorm_rows(x_ref[0], g_ref[...]) * (1.0 + sc_ref[0]) + sh_ref[0]
        h_scr[...] = h.astype(bf16)

    acc = jnp.dot(h_scr[...], w_ref[...].astype(bf16), preferred_element_type=f32)
    if n_split:
        for jj in range(n_split):
            @pl.when(j == jj)
            def _(jj=jj):
                o_refs[jj][0] = _rope_tile(acc, cos_ref[...], sin_ref[...]) if jj < n_rope_tiles else acc
    elif n_rope_tiles:
        @pl.when(j < n_rope_tiles)
        def _():
            o_refs[0][0] = _rope_tile(acc, cos_ref[...], sin_ref[...])

        @pl.when(j >= n_rope_tiles)
        def _():
            o_refs[0][0] = acc
    else:
        o_refs[0][0] = acc


def _norm_matmul(x, shift, scale, g, w, *, tm, tn, rope=None, n_rope_cols=0, split=False):
    B, T, D = x.shape
    N = w.shape[1]
    per_row = shift.shape[1] != 1
    mod_spec = (pl.BlockSpec((1, tm, D), lambda b, i, j: (b, i, 0)) if per_row
                else pl.BlockSpec((1, 1, D), lambda b, i, j: (b, 0, 0)))
    in_specs = [pl.BlockSpec((1, tm, D), lambda b, i, j: (b, i, 0)), mod_spec, mod_spec,
                pl.BlockSpec((1, D), lambda b, i, j: (0, 0)),
                pl.BlockSpec((D, tn), lambda b, i, j: (0, j))]
    args = [x, shift, scale, g, w]
    n_rope_tiles = 0
    if rope is not None:
        cos, sin = rope
        n_rope_tiles = n_rope_cols // tn
        if cos.shape[0] == T:
            rspec = pl.BlockSpec((tm, LANES), lambda b, i, j: (i, 0))
        else:
            nt = T // tm
            rspec = pl.BlockSpec((tm, LANES), lambda b, i, j: (b * nt + i, 0))
        in_specs += [rspec, rspec]
        args += [cos, sin]
    n_split = N // tn if split else 0
    if split:
        out_specs = [pl.BlockSpec((1, tm, tn), lambda b, i, j: (b, i, 0))] * n_split
        out_shape = [jax.ShapeDtypeStruct((B, T, tn), f32)] * n_split
    else:
        out_specs = pl.BlockSpec((1, tm, tn), lambda b, i, j: (b, i, j))
        out_shape = jax.ShapeDtypeStruct((B, T, N), f32)
    return pl.pallas_call(
        functools.partial(_norm_mm_kernel, n_rope_tiles=n_rope_tiles, n_split=n_split),
        grid=(B, T // tm, N // tn),
        in_specs=in_specs,
        out_specs=out_specs,
        out_shape=out_shape,
        scratch_shapes=[pltpu.VMEM((tm, D), bf16)],
        compiler_params=_cparams(("parallel", "parallel", "arbitrary")),
        name="norm_matmul",
    )(*args)


def _mm_post_kernel(*refs, n_in):
    a_refs = refs[:n_in]
    w_refs = refs[n_in:2 * n_in]
    x_ref, gate_ref, g_ref, o_ref = refs[2 * n_in:]
    y = None
    for a_ref, w_ref in zip(a_refs, w_refs):
        part = jnp.dot(a_ref[0].astype(bf16), w_ref[...], preferred_element_type=f32)
        y = part if y is None else y + part
    o_ref[0] = x_ref[0] + gate_ref[0] * _rmsnorm_rows(y, g_ref[...])


def _matmul_post(a_list, w_list, x, gate, g, *, tm):
    B, T, D = x.shape
    per_row = gate.shape[1] != 1
    gate_spec = (pl.BlockSpec((1, tm, D), lambda b, i: (b, i, 0)) if per_row
                 else pl.BlockSpec((1, 1, D), lambda b, i: (b, 0, 0)))
    in_specs = []
    for a in a_list:
        ka = a.shape[2]
        in_specs.append(pl.BlockSpec((1, tm, ka), lambda b, i: (b, i, 0)))
    for w in w_list:
        in_specs.append(pl.BlockSpec(w.shape, lambda b, i: (0, 0)))
    in_specs += [pl.BlockSpec((1, tm, D), lambda b, i: (b, i, 0)), gate_spec,
                 pl.BlockSpec((1, D), lambda b, i: (0, 0))]
    return pl.pallas_call(
        functools.partial(_mm_post_kernel, n_in=len(a_list)),
        grid=(B, T // tm),
        in_specs=in_specs,
        out_specs=pl.BlockSpec((1, tm, D), lambda b, i: (b, i, 0)),
        out_shape=jax.ShapeDtypeStruct((B, T, D), f32),
        compiler_params=_cparams(("parallel", "parallel")),
        name="matmul_post",
    )(*a_list, *w_list, x, gate, g)


def _mlp_kernel(x_ref, sh_ref, sc_ref, gate_ref, gpre_ref, gpost_ref, wu_ref, wd_ref, o_ref, h_scr, acc_scr):
    k = pl.program_id(2)

    @pl.when(k == 0)
    def _():
        h = _rmsnorm_rows(x_ref[0], gpre_ref[...]) * (1.0 + sc_ref[0]) + sh_ref[0]
        h_scr[...] = h.astype(bf16)
        acc_scr[...] = jnp.zeros_like(acc_scr)

    u = jnp.dot(h_scr[...], wu_ref[...], preferred_element_type=f32)
    a = jnp.square(jnp.maximum(u, 0.0)).astype(bf16)
    acc_scr[...] += jnp.dot(a, wd_ref[...], preferred_element_type=f32)

    @pl.when(k == pl.num_programs(2) - 1)
    def _():
        o_ref[0] = x_ref[0] + gate_ref[0] * _rmsnorm_rows(acc_scr[...], gpost_ref[...])


def _mlp(x, shift, scale, gate, g_pre, g_post, w_up, w_down, *, tm, tf):
    B, T, D = x.shape
    per_row = shift.shape[1] != 1
    mod_spec = (pl.BlockSpec((1, tm, D), lambda b, i, k: (b, i, 0)) if per_row
                else pl.BlockSpec((1, 1, D), lambda b, i, k: (b, 0, 0)))
    vec_spec = pl.BlockSpec((1, D), lambda b, i, k: (0, 0))
    return pl.pallas_call(
        _mlp_kernel,
        grid=(B, T // tm, D_FF // tf),
        in_specs=[pl.BlockSpec((1, tm, D), lambda b, i, k: (b, i, 0)), mod_spec, mod_spec, mod_spec,
                  vec_spec, vec_spec,
                  pl.BlockSpec((D, tf), lambda b, i, k: (0, k)),
                  pl.BlockSpec((tf, D), lambda b, i, k: (k, 0))],
        out_specs=pl.BlockSpec((1, tm, D), lambda b, i, k: (b, i, 0)),
        out_shape=jax.ShapeDtypeStruct((B, T, D), f32),
        scratch_shapes=[pltpu.VMEM((tm, D), bf16), pltpu.VMEM((tm, D), f32)],
        compiler_params=_cparams(("parallel", "parallel", "arbitrary")),
        name="mlp",
    )(x, shift, scale, gate, g_pre, g_post, w_up, w_down)


def _rwkv_pre_kernel(p_ref, sh0_ref, mu_ref, w0_ref, w2_ref, a0_ref, a2_ref, g2_ref, kk_w_ref, ka_ref,
                     rk_ref, bd_ref, r_o, w_o, k_o, v_o, kk_o, kka_o, g_o, bon_o, carry):
    ti = pl.program_id(1)

    @pl.when(ti == 0)
    def _():
        carry[...] = sh0_ref[0]

    p = p_ref[0]
    tt = p.shape[0]
    row = lax.broadcasted_iota(jnp.int32, (tt, 1), 0)
    prev = jnp.where(row == 0, carry[...], pltpu.roll(p, 1, 0))
    carry[...] = p[tt - 1:tt, :]
    ps = p + mu_ref[...] * (prev - p)
    r = ps[:, 0:D_A]
    k = ps[:, D_A:2 * D_A]
    v = ps[:, 2 * D_A:3 * D_A]
    o = 3 * D_A
    dw = ps[:, o:o + LORA_W]
    da = ps[:, o + LORA_W:o + LORA_W + LORA_A]
    dg = ps[:, o + LORA_W + LORA_A:o + LORA_W + LORA_A + LORA_G]
    bd = bd_ref[...]
    wl = w0_ref[...] + jnp.dot(jnp.tanh(dw).astype(bf16), w2_ref[...], preferred_element_type=f32)
    w_log = -_softplus(-wl) - 0.5
    decay = jnp.exp(-jnp.exp(w_log))
    a = jax.nn.sigmoid(a0_ref[...] + jnp.dot(da.astype(bf16), a2_ref[...], preferred_element_type=f32))
    g = jnp.dot(jax.nn.sigmoid(dg).astype(bf16), g2_ref[...], preferred_element_type=f32)
    kk = k * kk_w_ref[...]
    nrm = jnp.sqrt(_seg_sum(kk * kk, bd))
    kk = kk / jnp.maximum(nrm, 1e-12)
    k2 = k * (1.0 + (a - 1.0) * ka_ref[...])
    bonus = _seg_sum(r * k2 * rk_ref[...], bd) * v
    r_o[0] = r
    w_o[0] = decay
    k_o[0] = k2
    v_o[0] = v
    kk_o[0] = kk
    kka_o[0] = kk * a
    g_o[0] = g
    bon_o[0] = bonus


def _rwkv_pre(proj, shift0, mu, w0, w2, a0, a2, g2, k_k, k_a, r_k, bd512, *, tt):
    B, T, _ = proj.shape
    vec = lambda n: pl.BlockSpec((1, n), lambda b, i: (0, 0))
    mat = lambda s: pl.BlockSpec(s, lambda b, i: (0, 0))
    out_spec = pl.BlockSpec((1, tt, D_A), lambda b, i: (b, i, 0))
    out_sds = jax.ShapeDtypeStruct((B, T, D_A), f32)
    return pl.pallas_call(
        _rwkv_pre_kernel,
        grid=(B, T // tt),
        in_specs=[pl.BlockSpec((1, tt, RW_COLS), lambda b, i: (b, i, 0)),
                  pl.BlockSpec((1, 1, RW_COLS), lambda b, i: (b, 0, 0)),
                  vec(RW_COLS), vec(D_A), mat((LORA_W, D_A)), vec(D_A), mat((LORA_A, D_A)),
                  mat((LORA_G, D_A)), vec(D_A), vec(D_A), vec(D_A), mat((D_A, D_A))],
        out_specs=[out_spec] * 8,
        out_shape=[out_sds] * 8,
        scratch_shapes=[pltpu.VMEM((1, RW_COLS), f32)],
        compiler_params=_cparams(("parallel", "arbitrary")),
        name="rwkv_pre",
    )(proj, shift0, mu, w0, w2, a0, a2, g2, k_k, k_a, r_k, bd512)


SCAN_NB = 8
PAIRS = H_A // 2


def _wkv_scan_kernel(r_ref, w_ref, k_ref, v_ref, kk_ref, kka_ref, s0_ref, bd_ref, e_ref,
                     y_ref, st_ref, s_scr):
    c = pl.program_id(1)
    n_steps = r_ref.shape[1]
    rows_per_b = PAIRS * HEAD_DIM

    @pl.when(c == 0)
    def _():
        for b in range(SCAN_NB):
            for p in range(PAIRS):
                r0 = (b * PAIRS + p) * HEAD_DIM
                s_scr[r0:r0 + HEAD_DIM, :] = jnp.concatenate([s0_ref[b, 2 * p], s0_ref[b, 2 * p + 1]], axis=1)

    bd = bd_ref[...]
    e4 = e_ref[...]

    def step8(t8, carry):
        t0 = pl.multiple_of(t8 * SUBLANES, SUBLANES)
        names = (("r", r_ref), ("w", w_ref), ("k", k_ref), ("v", v_ref), ("kk", kk_ref), ("kka", kka_ref))
        blks = [{name: ref[b, pl.ds(t0, SUBLANES), :] for name, ref in names} for b in range(SCAN_NB)]
        yrows = [[] for _ in range(SCAN_NB)]
        e_all = jnp.concatenate([e4] * SCAN_NB, axis=0)
        for j in range(SUBLANES):
            def rows(name):
                parts = [jnp.broadcast_to(blks[b][name][j:j + 1, LANES * p:LANES * (p + 1)], (HEAD_DIM, LANES))
                         for b in range(SCAN_NB) for p in range(PAIRS)]
                return jnp.concatenate(parts, axis=0)

            s = s_scr[...]
            sk = _seg_sum_paired(s * rows("kk"), bd)
            vcol = _seg_sum_paired(e_all * rows("v"), bd)
            s = s * rows("w") - sk * rows("kka") + vcol * rows("k")
            s_scr[...] = s
            yrep = _seg_sum_paired(s * rows("r"), bd)
            yrow = jnp.sum((e_all * yrep).reshape(SCAN_NB * PAIRS, HEAD_DIM, LANES), axis=1)
            for b in range(SCAN_NB):
                yrows[b].append(jnp.concatenate([yrow[b * PAIRS + p:b * PAIRS + p + 1, :] for p in range(PAIRS)],
                                                axis=1))
        for b in range(SCAN_NB):
            y_ref[b, pl.ds(t0, SUBLANES), :] = jnp.concatenate(yrows[b], axis=0)
        return carry

    lax.fori_loop(0, n_steps // SUBLANES, step8, 0)

    @pl.when(c == pl.num_programs(1) - 1)
    def _():
        for b in range(SCAN_NB):
            for p in range(PAIRS):
                r0 = (b * PAIRS + p) * HEAD_DIM
                tile = s_scr[r0:r0 + HEAD_DIM, :]
                st_ref[b, 2 * p] = tile[:, 0:HEAD_DIM]
                st_ref[b, 2 * p + 1] = tile[:, HEAD_DIM:]


def _wkv_scan(r, w, k, v, kk, kka, s0, bd256, e4, *, chunk):
    NB, T, _ = r.shape
    rows = SCAN_NB * PAIRS * HEAD_DIM
    xspec = pl.BlockSpec((SCAN_NB, chunk, D_A), lambda g, c: (g, c, 0))
    sspec = pl.BlockSpec((SCAN_NB, H_A, HEAD_DIM, HEAD_DIM), lambda g, c: (g, 0, 0, 0))
    return pl.pallas_call(
        _wkv_scan_kernel,
        grid=(NB // SCAN_NB, T // chunk),
        in_specs=[xspec] * 6 + [sspec,
                                pl.BlockSpec(bd256.shape, lambda g, c: (0, 0)),
                                pl.BlockSpec((PAIRS * HEAD_DIM, LANES), lambda g, c: (0, 0))],
        out_specs=[xspec, sspec],
        out_shape=[jax.ShapeDtypeStruct((NB, T, D_A), f32),
                   jax.ShapeDtypeStruct((NB, H_A, HEAD_DIM, HEAD_DIM), f32)],
        scratch_shapes=[pltpu.VMEM((rows, LANES), f32)],
        compiler_params=_cparams(("parallel", "arbitrary")),
        name="wkv_scan",
    )(r, w, k, v, kk, kka, s0, bd256, e4)


def _rwkv_post_kernel(y_ref, g_ref, bon_ref, lw_ref, lb_ref, bd_ref, o_ref):
    y = y_ref[0]
    bd = bd_ref[...]
    mean = _seg_sum(y, bd) * (1.0 / HEAD_DIM)
    d = y - mean
    var = _seg_sum(d * d, bd) * (1.0 / HEAD_DIM)
    yn = d * lax.rsqrt(var + LNX_EPS) * lw_ref[...] + lb_ref[...]
    o_ref[0] = (yn + bon_ref[0]) * g_ref[0]


def _rwkv_post(y, g, bonus, lnx_w, lnx_b, bd512, *, tt):
    B, T, _ = y.shape
    spec = pl.BlockSpec((1, tt, D_A), lambda b, i: (b, i, 0))
    vec = pl.BlockSpec((1, D_A), lambda b, i: (0, 0))
    return pl.pallas_call(
        _rwkv_post_kernel,
        grid=(B, T // tt),
        in_specs=[spec, spec, spec, vec, vec, pl.BlockSpec((D_A, D_A), lambda b, i: (0, 0))],
        out_specs=spec,
        out_shape=jax.ShapeDtypeStruct((B, T, D_A), f32),
        compiler_params=_cparams(("parallel", "parallel")),
        name="rwkv_post",
    )(y, g, bonus, lnx_w, lnx_b, bd512)


def _fox_gate_kernel(f_ref, b_ref, lf_ref, cum_ref):
    z = f_ref[0] + b_ref[...]
    lf = -_softplus(-z)
    lf_ref[0] = lf
    T = lf.shape[0]
    row = lax.broadcasted_iota(jnp.int32, (T, 1), 0)
    x = lf
    s = 1
    while s < T:
        x = x + jnp.where(row >= s, pltpu.roll(x, s, 0), 0.0)
        s *= 2
    cum_ref[0] = x


def _fox_gate(proj, b_f_pad):
    B, T, _ = proj.shape
    spec = pl.BlockSpec((1, T, LANES), lambda b: (b, 0, 0))
    sds = jax.ShapeDtypeStruct((B, T, LANES), f32)
    return pl.pallas_call(
        _fox_gate_kernel,
        grid=(B,),
        in_specs=[pl.BlockSpec((1, T, LANES), lambda b: (b, 0, EV_F_OFF // LANES)),
                  pl.BlockSpec((1, LANES), lambda b: (0, 0))],
        out_specs=[spec, spec],
        out_shape=[sds, sds],
        compiler_params=_cparams(("parallel",)),
        name="fox_gate",
    )(proj, b_f_pad)


def _flash_kernel(q_ref, k_ref, v_ref, l2c_ref, *rest, n_heads, tq, tk, has_bias, table_everywhere, copy_kv,
                  head_group):
    if has_bias:
        cq_ref, ckt_ref = rest[:2]
        rest = rest[2:]
    o_ref = rest[0]
    rest = rest[1:]
    qi = pl.program_id(1)
    if copy_kv:
        ko_ref, vo_ref = rest[:2]
        rest = rest[2:]

        @pl.when(qi == 0)
        def _():
            ko_ref[0] = k_ref[0]
            vo_ref[0] = v_ref[0]
    if has_bias:
        m_scr, acc_scr, qb_scr, cq2_scr = rest
    else:
        m_scr, acc_scr, qb_scr = rest
    m_scr[...] = jnp.full_like(m_scr, -jnp.inf)
    acc_scr[...] = jnp.zeros_like(acc_scr)
    low_q = lax.broadcasted_iota(jnp.int32, (tq, LANES), 1) < HEAD_DIM
    low_k = lax.broadcasted_iota(jnp.int32, (tk, LANES), 1) < HEAD_DIM
    for h in range(n_heads):
        q_pair = q_ref[0, :, (h // 2) * LANES:(h // 2 + 1) * LANES] * (ATTN_SCALE * LOG2E)
        qb_scr[h] = jnp.where(low_q == (h % 2 == 0), q_pair, 0.0).astype(bf16)
        if has_bias:
            cq2_scr[h] = jnp.broadcast_to(cq_ref[0, :, h:h + 1] * LOG2E, (tq, LANES))
    nt = (((1,), (1,)), ((), ()))

    def block(it, use_table):
        k0 = pl.multiple_of((qi - it) * tk, tk)
        for g0 in range(0, n_heads, head_group):
            heads = range(g0, g0 + head_group)
            pairs = range(g0 // 2, (g0 + head_group) // 2)
            parts = []
            for p in pairs:
                k_pair = k_ref[0, pl.ds(k0, tk), p * LANES:(p + 1) * LANES].astype(bf16)
                q2 = qb_scr[2 * p:2 * p + 2].reshape(2 * tq, LANES)
                s2 = lax.dot_general(q2, k_pair, nt, preferred_element_type=f32)
                for n in range(2):
                    s = s2[n * tq:(n + 1) * tq]
                    if has_bias:
                        s = s - ckt_ref[0, 2 * p + n:2 * p + n + 1, pl.ds(k0, tk)] * LOG2E
                    parts.append(s)
            s3 = jnp.stack(parts)
            if use_table:
                s3 = s3 + l2c_ref[it][None]
            m_prev = m_scr[g0:g0 + head_group]
            m_cur = jnp.max(s3, axis=2, keepdims=True)
            if has_bias:
                cq2 = cq2_scr[g0:g0 + head_group]
                m_next = jnp.maximum(m_prev, m_cur + cq2)
                m_sub = m_next - cq2
            else:
                m_next = jnp.maximum(m_prev, m_cur)
                m_sub = m_next
            p3 = jnp.exp2(s3 - jnp.concatenate([m_sub] * (tk // LANES), axis=2)).astype(bf16)
            alpha = jnp.exp2(m_prev - m_next)
            m_scr[g0:g0 + head_group] = m_next
            v_pairs = {p: v_ref[0, pl.ds(k0, tk), p * LANES:(p + 1) * LANES].astype(bf16) for p in pairs}
            for n, h in enumerate(heads):
                v_ext = jnp.where(low_k == (h % 2 == 0), v_pairs[h // 2], 1.0)
                acc_scr[h] = acc_scr[h] * alpha[n] + jnp.dot(p3[n], v_ext, preferred_element_type=f32)

    block(0, True)

    def body(it, carry):
        block(it, table_everywhere)
        return carry

    lax.fori_loop(1, qi + 1, body, 0)
    for p in range(n_heads // 2):
        acc_e, acc_o = acc_scr[2 * p], acc_scr[2 * p + 1]
        out_e = acc_e / pltpu.roll(acc_e, HEAD_DIM, 1)
        out_o = acc_o / pltpu.roll(acc_o, HEAD_DIM, 1)
        o_ref[0, :, p * LANES:(p + 1) * LANES] = jnp.where(low_q, out_e, out_o)


def _flash_attention(q_src, k_src, v_src, q_blk, k_blk, v_blk, l2c, n_heads, *, tq, tk, bias=None,
                     table_everywhere, head_group, copy_kv=False):
    assert tq == tk
    B, T, _ = q_src.shape
    hd = n_heads * HEAD_DIM
    in_specs = [pl.BlockSpec((1, tq, hd), lambda b, i: (b, i, q_blk)),
                pl.BlockSpec((1, T, hd), lambda b, i: (b, 0, k_blk)),
                pl.BlockSpec((1, T, hd), lambda b, i: (b, 0, v_blk)),
                pl.BlockSpec(l2c.shape, lambda b, i: (0, 0, 0))]
    args = [q_src, k_src, v_src, l2c]
    if bias is not None:
        cum, cum_t = bias
        in_specs += [pl.BlockSpec((1, tq, LANES), lambda b, i: (b, i, 0)),
                     pl.BlockSpec((1, cum_t.shape[1], T), lambda b, i: (b, 0, 0))]
        args += [cum, cum_t]
    scratch = [pltpu.VMEM((n_heads, tq, LANES), f32), pltpu.VMEM((n_heads, tq, LANES), f32),
               pltpu.VMEM((n_heads, tq, LANES), bf16)]
    if bias is not None:
        scratch.append(pltpu.VMEM((n_heads, tq, LANES), f32))
    out_specs = pl.BlockSpec((1, tq, hd), lambda b, i: (b, i, 0))
    out_shape = jax.ShapeDtypeStruct((B, T, hd), f32)
    if copy_kv:
        kv_out = pl.BlockSpec((1, T, hd), lambda b, i: (b, 0, 0))
        out_specs = [out_specs, kv_out, kv_out]
        out_shape = [out_shape] * 3
    return pl.pallas_call(
        functools.partial(_flash_kernel, n_heads=n_heads, tq=tq, tk=tk, has_bias=bias is not None,
                          table_everywhere=table_everywhere, copy_kv=copy_kv, head_group=head_group),
        grid=(B, T // tq),
        in_specs=in_specs,
        out_specs=out_specs,
        out_shape=out_shape,
        scratch_shapes=scratch,
        compiler_params=_cparams(("parallel", "arbitrary")),
        name="flash_attention",
    )(*args)


def _qk(q_h, k_h):
    return lax.dot_general(q_h, k_h, (((1,), (1,)), ((), ())), preferred_element_type=f32)


FOX_PP = 32
DIL_KC = 1024


def _block_diag_queries(q2, n_heads):
    nq, hd = q2.shape
    qt = jnp.concatenate([q2] * n_heads, axis=0)
    rh = lax.broadcasted_iota(jnp.int32, (n_heads * nq, hd), 0) // nq
    ch = lax.broadcasted_iota(jnp.int32, (n_heads * nq, hd), 1) // HEAD_DIM
    return jnp.where(rh == ch, qt, 0.0).astype(bf16)


def _rows_per_head(x, nq):
    n_heads, n = x.shape
    return jnp.broadcast_to(x[:, None, :], (n_heads, nq, n)).reshape(n_heads * nq, n)


def _decode_first(s, v_new, m_scr, l_scr, accn_scr, acct_scr):
    m = jnp.max(s, axis=1, keepdims=True)
    p = jnp.exp2(s - m)
    m_scr[...] = jnp.broadcast_to(m, m_scr.shape)
    l_scr[...] = jnp.broadcast_to(jnp.sum(p, axis=1, keepdims=True), l_scr.shape)
    accn_scr[...] = jnp.dot(p.astype(bf16), v_new, preferred_element_type=f32)
    acct_scr[...] = jnp.zeros_like(acct_scr)


def _decode_update(s, vt_all, m_scr, l_scr, accn_scr, acct_scr):
    R = s.shape[0]
    hd = vt_all.shape[0]
    m_prev = m_scr[...]
    m_next = jnp.maximum(m_prev, jnp.max(s, axis=1, keepdims=True))
    p = jnp.exp2(s - m_next[:, 0:1])
    alpha = jnp.exp2(m_prev - m_next)
    l_scr[...] = alpha * l_scr[...] + jnp.sum(p, axis=1, keepdims=True)
    m_scr[...] = m_next
    accn_scr[...] = accn_scr[...] * jnp.concatenate([alpha] * (hd // LANES), axis=1)
    new_t = lax.dot_general(vt_all, p.astype(bf16), (((1,), (1,)), ((), ())), preferred_element_type=f32)
    if R < LANES:
        alpha = jnp.concatenate([alpha, jnp.zeros((LANES - R, LANES), f32)], axis=0)
    alpha_row = jnp.transpose(alpha)[0:1, 0:R]
    acct_scr[:, 0:R] = acct_scr[:, 0:R] * alpha_row + new_t


def _decode_finish(o_ref, n_heads, nq, l_scr, accn_scr, acct_scr):
    for h in range(n_heads):
        blk = acct_scr[h * HEAD_DIM:(h + 1) * HEAD_DIM, :]
        sq = jnp.transpose(jnp.concatenate([blk, jnp.zeros((LANES - HEAD_DIM, LANES), f32)], axis=0))
        rows = slice(h * nq, (h + 1) * nq)
        cols = slice(h * HEAD_DIM, (h + 1) * HEAD_DIM)
        num = sq[rows, 0:HEAD_DIM] + accn_scr[rows, cols]
        o_ref[0, :, cols] = num / l_scr[rows, 0:HEAD_DIM]


def _fox_decode2_kernel(pt_ref, q_ref, kn_ref, vn_ref, cqr_ref, cnt_ref, *rest):
    lf_refs = rest[:FOX_PP]
    k_refs = rest[FOX_PP:2 * FOX_PP]
    v_refs = rest[2 * FOX_PP:3 * FOX_PP]
    o_ref, m_scr, l_scr, accn_scr, acct_scr, tot_scr, qbd_scr = rest[3 * FOX_PP:]
    j = pl.program_id(1)
    nq = q_ref.shape[1]
    R = H_B * nq
    cq2 = cqr_ref[0] * LOG2E

    @pl.when(j == 0)
    def _():
        qbd = _block_diag_queries(q_ref[0] * (ATTN_SCALE * LOG2E), H_B)
        qbd_scr[...] = qbd
        tot_scr[...] = jnp.zeros_like(tot_scr)
        s = _qk(qbd, kn_ref[0].astype(bf16))
        s = s + cq2[:, 0:nq] - _rows_per_head(cnt_ref[0][:, 0:nq] * LOG2E, nq)
        qi = lax.broadcasted_iota(jnp.int32, (R, nq), 0) % nq
        kj = lax.broadcasted_iota(jnp.int32, (R, nq), 1)
        s = jnp.where(kj <= qi, s, -jnp.inf)
        _decode_first(s, vn_ref[0].astype(bf16), m_scr, l_scr, accn_scr, acct_scr)

    lane = lax.broadcasted_iota(jnp.int32, (H_B, PAGE_SIZE), 1)
    qbd = qbd_scr[...]
    tot = tot_scr[...]
    s_parts = []
    v_parts = []
    for i in range(FOX_PP):
        x = lf_refs[i][0]
        sft = 1
        while sft < PAGE_SIZE:
            x = x + jnp.where(lane >= sft, pltpu.roll(x, sft, 1), 0.0)
            sft *= 2
        page_tot = x[:, PAGE_SIZE - 1:PAGE_SIZE]
        bias2 = _rows_per_head((page_tot - x + tot) * LOG2E, nq) + cq2
        tot = tot + page_tot
        kt_all = k_refs[i][0].reshape(D_B, PAGE_SIZE).astype(bf16)
        s_parts.append(jnp.dot(qbd, kt_all, preferred_element_type=f32) + bias2)
        v_parts.append(v_refs[i][0].reshape(D_B, PAGE_SIZE).astype(bf16))
    tot_scr[...] = tot
    _decode_update(jnp.concatenate(s_parts, axis=1), jnp.concatenate(v_parts, axis=1),
                   m_scr, l_scr, accn_scr, acct_scr)

    @pl.when(j == pl.num_programs(1) - 1)
    def _():
        _decode_finish(o_ref, H_B, nq, l_scr, accn_scr, acct_scr)


def _fox_decode2(page_table, proj, cum_rows, cum_new_t, lf_cache_t, k_cache, v_cache):
    NB, nq, _ = proj.shape
    n_pages = page_table.shape[1]
    steps = n_pages // FOX_PP
    R = H_B * nq

    def lf_spec(i):
        return pl.BlockSpec((1, H_B, PAGE_SIZE),
                            lambda b, j, pt: (pt[b, n_pages - 1 - (j * FOX_PP + i)], 0, 0))

    def kv_spec(i):
        return pl.BlockSpec((1, H_B, HEAD_DIM, PAGE_SIZE),
                            lambda b, j, pt: (pt[b, n_pages - 1 - (j * FOX_PP + i)], 0, 0, 0))

    in_specs = [pl.BlockSpec((1, nq, D_B), lambda b, j, pt: (b, 0, EV_Q_OFF // D_B)),
                pl.BlockSpec((1, nq, D_B), lambda b, j, pt: (b, 0, EV_K_OFF // D_B)),
                pl.BlockSpec((1, nq, D_B), lambda b, j, pt: (b, 0, EV_V_OFF // D_B)),
                pl.BlockSpec((1, R, LANES), lambda b, j, pt: (b, 0, 0)),
                pl.BlockSpec((1, SUBLANES, LANES), lambda b, j, pt: (b, 0, 0))]
    args = [proj, proj, proj, cum_rows, cum_new_t]
    for i in range(FOX_PP):
        in_specs.append(lf_spec(i))
        args.append(lf_cache_t)
    for cache in (k_cache, v_cache):
        for i in range(FOX_PP):
            in_specs.append(kv_spec(i))
            args.append(cache)
    grid_spec = pltpu.PrefetchScalarGridSpec(
        num_scalar_prefetch=1,
        grid=(NB, steps),
        in_specs=in_specs,
        out_specs=pl.BlockSpec((1, nq, D_B), lambda b, j, pt: (b, 0, 0)),
        scratch_shapes=[pltpu.VMEM((R, LANES), f32), pltpu.VMEM((R, LANES), f32),
                        pltpu.VMEM((R, D_B), f32), pltpu.VMEM((D_B, LANES), f32),
                        pltpu.VMEM((H_B, LANES), f32), pltpu.VMEM((R, D_B), bf16)],
    )
    return pl.pallas_call(
        _fox_decode2_kernel,
        grid_spec=grid_spec,
        out_shape=jax.ShapeDtypeStruct((NB, nq, D_B), f32),
        compiler_params=_cparams(("parallel", "arbitrary")),
        name="fox_decode",
    )(page_table, *args)


def _dil_decode2_kernel(q_ref, kn_ref, vn_ref, l2new_ref, l2past_ref, k_ref, v_ref, o_ref,
                        m_scr, l_scr, accn_scr, acct_scr, qbd_scr):
    j = pl.program_id(1)
    nq = q_ref.shape[1]

    @pl.when(j == 0)
    def _():
        qbd = _block_diag_queries(q_ref[0] * (ATTN_SCALE * LOG2E), H_C)
        qbd_scr[...] = qbd
        s = _qk(qbd, kn_ref[0].astype(bf16)) + jnp.concatenate([l2new_ref[:, 0:nq]] * H_C, axis=0)
        _decode_first(s, vn_ref[0].astype(bf16), m_scr, l_scr, accn_scr, acct_scr)

    kt_all = k_ref[0].reshape(D_C, DIL_KC).astype(bf16)
    s = jnp.dot(qbd_scr[...], kt_all, preferred_element_type=f32)
    s = s + jnp.concatenate([l2past_ref[...]] * H_C, axis=0)
    _decode_update(s, v_ref[0].reshape(D_C, DIL_KC).astype(bf16), m_scr, l_scr, accn_scr, acct_scr)

    @pl.when(j == pl.num_programs(1) - 1)
    def _():
        _decode_finish(o_ref, H_C, nq, l_scr, accn_scr, acct_scr)


def _dil_decode2(qkv, l2_new, l2_past, k_state, v_state):
    NB, nq, _ = qkv.shape
    L = k_state.shape[3]
    R = H_C * nq
    kv_spec = pl.BlockSpec((1, H_C, HEAD_DIM, DIL_KC), lambda b, j: (b, 0, 0, j))
    return pl.pallas_call(
        _dil_decode2_kernel,
        grid=(NB, L // DIL_KC),
        in_specs=[pl.BlockSpec((1, nq, D_C), lambda b, j: (b, 0, 0)),
                  pl.BlockSpec((1, nq, D_C), lambda b, j: (b, 0, 1)),
                  pl.BlockSpec((1, nq, D_C), lambda b, j: (b, 0, 2)),
                  pl.BlockSpec(l2_new.shape, lambda b, j: (0, 0)),
                  pl.BlockSpec((nq, DIL_KC), lambda b, j: (0, j)),
                  kv_spec, kv_spec],
        out_specs=pl.BlockSpec((1, nq, D_C), lambda b, j: (b, 0, 0)),
        out_shape=jax.ShapeDtypeStruct((NB, nq, D_C), f32),
        scratch_shapes=[pltpu.VMEM((R, LANES), f32), pltpu.VMEM((R, LANES), f32),
                        pltpu.VMEM((R, D_C), f32), pltpu.VMEM((D_C, LANES), f32),
                        pltpu.VMEM((R, D_C), bf16)],
        compiler_params=_cparams(("parallel", "arbitrary")),
        name="dil_decode",
    )(qkv, qkv, qkv, l2_new, l2_past, k_state, v_state)


def _dil_multiplicity(delta):
    delta = np.asarray(delta)
    c = np.zeros(delta.shape, np.float32)
    for w, d in DIL_PAIRS:
        c += ((delta >= 0) & (delta % d == 0) & (delta <= w)).astype(np.float32)
    return c


def _log2_block_tables(n_blocks, t, mult_fn):
    i = np.arange(t)[:, None]
    j = np.arange(t)[None, :]
    with np.errstate(divide="ignore"):
        tabs = [np.log2(mult_fn(i - j + d * t)).astype(np.float32) for d in range(n_blocks)]
    return jnp.asarray(np.stack(tabs))


def _rope_tables(pos):
    half = HEAD_DIM // 2
    inv = ROPE_THETA ** (-jnp.arange(half, dtype=f32) / half)
    ang = pos.astype(f32)[:, None] * inv[None, :]
    cos = jnp.cos(ang)
    sin = jnp.sin(ang)
    return jnp.tile(cos, (1, 4)), jnp.concatenate([-sin, sin, -sin, sin], axis=1)


def _mods(mod_lj, n_prompt, per_row_repeat):
    mp = mod_lj[:n_prompt][:, None, :]
    ms = jnp.repeat(mod_lj[n_prompt:], per_row_repeat, axis=0)[None]
    sp = [mp[..., i * D_MODEL:(i + 1) * D_MODEL] for i in range(3)]
    ss = [ms[..., i * D_MODEL:(i + 1) * D_MODEL] for i in range(3)]
    return sp, ss


def kernel(x_prompt, x_sample, cache_fox_k, cache_fox_v, cache_fox_logf, state_rwkv_wkv, state_rwkv_shift,
           state_dil_k, state_dil_v, page_table, c_prompt, c_sample, ada_w, ada_b, norm_pre, norm_post,
           mlp_up, mlp_down, ev_w_in, ev_w_out, rw_mu, rw_w0, rw_w2, rw_a0, rw_a2, rw_g2, rw_k_k, rw_k_a,
           rw_r_k, rw_lnx_w, rw_lnx_b, fox_b_f, od_w_in, od_w_out):
    BP, T, D = x_prompt.shape
    BS, TS, _ = x_sample.shape
    depth = ada_w.shape[0]
    past_len = page_table.shape[1] * PAGE_SIZE

    mod = _ada_mod(jnp.concatenate([c_prompt, c_sample], axis=0), ada_w, ada_b)

    idx = np.arange(LANES)
    idx2 = np.arange(2 * LANES)
    bd256 = jnp.asarray((idx2[:, None] // HEAD_DIM == idx2[None, :] // HEAD_DIM).astype(np.float32)).astype(bf16)
    idx5 = np.arange(D_A)
    bd512 = jnp.asarray((idx5[:, None] // HEAD_DIM == idx5[None, :] // HEAD_DIM).astype(np.float32)).astype(bf16)
    e4 = jnp.asarray(np.tile((idx[None, :] % HEAD_DIM == np.arange(HEAD_DIM)[:, None]).astype(np.float32),
                             (PAIRS, 1)))
    TQ = 256
    causal_tab = _log2_block_tables(1, TQ, lambda dl: (dl >= 0).astype(np.float32))
    dil_tab = _log2_block_tables(T // TQ, TQ, _dil_multiplicity)

    xp = x_prompt
    xs = x_sample.reshape(1, BS * TS, D)
    outs = {}
    for l in range(depth):
        i = l // 2
        (sh_p, sc_p, gt_p), (sh_s, sc_s, gt_s) = _mods(mod[2 * l], BP, TS)
        g_pre = norm_pre[l, 0][None]
        g_post = norm_post[l, 0][None]
        if l % 2 == 0:
            w_in = ev_w_in[i]
            zeros = lambda n: jnp.zeros((D, n), f32)
            w_pad = jnp.concatenate([w_in[:, :RW_COLS], zeros(EV_Q_OFF - RW_COLS), w_in[:, RW_COLS:],
                                     zeros(EV_COLS_PAD - EV_F_OFF - H_B)], axis=1).astype(bf16)
            w_out = ev_w_out[i].astype(bf16)
            b_f_pad = jnp.pad(fox_b_f[i], (0, LANES - H_B))[None]
            rw_args = (rw_mu[i][None], rw_w0[i][None], rw_w2[i].astype(bf16), rw_a0[i][None],
                       rw_a2[i].astype(bf16), rw_g2[i].astype(bf16), rw_k_k[i][None], rw_k_a[i][None],
                       rw_r_k[i].reshape(1, D_A), bd512)

            proj_p = _norm_matmul(xp, sh_p, sc_p, g_pre, w_pad, tm=2048, tn=768)
            r, w, k2, v, kk, kka, g, bonus = _rwkv_pre(proj_p, jnp.zeros((BP, 1, RW_COLS), f32), *rw_args, tt=256)
            y, st = _wkv_scan(r, w, k2, v, kk, kka, jnp.zeros((BP, H_A, HEAD_DIM, HEAD_DIM), f32), bd256, e4,
                              chunk=64)
            ya = _rwkv_post(y, g, bonus, rw_lnx_w[i][None], rw_lnx_b[i][None], bd512, tt=512)
            lf_p, cum_p = _fox_gate(proj_p, b_f_pad)
            cum_t = jnp.transpose(cum_p[:, :, :SUBLANES], (0, 2, 1))
            yb, fk, fv = _flash_attention(proj_p, proj_p, proj_p, EV_Q_OFF // D_B, EV_K_OFF // D_B,
                                          EV_V_OFF // D_B, causal_tab, H_B, tq=TQ, tk=TQ, bias=(cum_p, cum_t),
                                          table_everywhere=False, head_group=8, copy_kv=True)
            xp = _matmul_post([ya, yb], [w_out[:D_A], w_out[D_A:]], xp, gt_p, g_post, tm=512)
            outs.setdefault("fk_p", []).append(fk.reshape(BP, T, H_B, HEAD_DIM))
            outs.setdefault("fv_p", []).append(fv.reshape(BP, T, H_B, HEAD_DIM))
            outs.setdefault("fl_p", []).append(lf_p[:, :, :H_B])
            outs.setdefault("wkv_p", []).append(st)
            outs.setdefault("sh_p", []).append(proj_p[:, T - 1, :RW_COLS])

            proj_s = _norm_matmul(xs, sh_s, sc_s, g_pre, w_pad, tm=BS * TS, tn=1280).reshape(BS, TS, EV_COLS_PAD)
            r, w, k2, v, kk, kka, g, bonus = _rwkv_pre(proj_s, state_rwkv_shift[i][:, None, :], *rw_args, tt=TS)
            y, st = _wkv_scan(r, w, k2, v, kk, kka, state_rwkv_wkv[i], bd256, e4, chunk=TS)
            ya = _rwkv_post(y, g, bonus, rw_lnx_w[i][None], rw_lnx_b[i][None], bd512, tt=TS)
            lf_s, cum_s = _fox_gate(proj_s, b_f_pad)
            cum_s_t = jnp.pad(jnp.transpose(cum_s[:, :, :SUBLANES], (0, 2, 1)), ((0, 0), (0, 0), (0, LANES - TS)))
            lf_cache_t = jnp.transpose(cache_fox_logf[i], (0, 2, 1))
            cum_rows = jnp.broadcast_to(
                jnp.transpose(cum_s[:, :, :H_B], (0, 2, 1)).reshape(BS, H_B * TS, 1), (BS, H_B * TS, LANES))
            yb = _fox_decode2(page_table, proj_s, cum_rows, cum_s_t, lf_cache_t,
                              jnp.transpose(cache_fox_k[i], (0, 2, 3, 1)),
                              jnp.transpose(cache_fox_v[i], (0, 2, 3, 1)))
            xs = _matmul_post([ya.reshape(1, BS * TS, D_A), yb.reshape(1, BS * TS, D_B)],
                              [w_out[:D_A], w_out[D_A:]], xs, gt_s, g_post, tm=BS * TS)
            outs.setdefault("fk_s", []).append(proj_s[:, :, EV_K_OFF:EV_K_OFF + D_B].reshape(BS, TS, H_B, HEAD_DIM))
            outs.setdefault("fv_s", []).append(proj_s[:, :, EV_V_OFF:EV_V_OFF + D_B].reshape(BS, TS, H_B, HEAD_DIM))
            outs.setdefault("fl_s", []).append(lf_s[:, :, :H_B])
            outs.setdefault("wkv_s", []).append(st)
            outs.setdefault("sh_s", []).append(proj_s[:, TS - 1, :RW_COLS])
        else:
            w_in = od_w_in[i].astype(bf16)
            w_out = od_w_out[i].astype(bf16)
            rope_p = _rope_tables(jnp.arange(T))
            cos_s, sin_s = _rope_tables(past_len + jnp.arange(TS))
            rope_s = (jnp.tile(cos_s, (BS, 1)), jnp.tile(sin_s, (BS, 1)))

            q_p, k_p, v_p = _norm_matmul(xp, sh_p, sc_p, g_pre, w_in, tm=1024, tn=D_C, rope=rope_p,
                                         n_rope_cols=2 * D_C, split=True)
            o_p = _flash_attention(q_p, k_p, v_p, 0, 0, 0, dil_tab, H_C, tq=TQ, tk=TQ, table_everywhere=True,
                                   head_group=4)
            xp = _matmul_post([o_p], [w_out], xp, gt_p, g_post, tm=512)
            keep = min(DIL_PAIRS[-1][0], T)
            outs.setdefault("dk_p", []).append(k_p[:, T - keep:].reshape(BP, keep, H_C, HEAD_DIM))
            outs.setdefault("dv_p", []).append(v_p[:, T - keep:].reshape(BP, keep, H_C, HEAD_DIM))

            qkv_s = _norm_matmul(xs, sh_s, sc_s, g_pre, w_in, tm=BS * TS, tn=D_C, rope=rope_s,
                                 n_rope_cols=2 * D_C).reshape(BS, TS, 3 * D_C)
            L = state_dil_k.shape[2]
            tq_idx = np.arange(TS)[:, None]
            c_new = np.zeros((TS, LANES), np.float32)
            c_new[:, :TS] = _dil_multiplicity(tq_idx - np.arange(TS)[None, :])
            c_past = _dil_multiplicity(L + tq_idx - np.arange(L)[None, :])
            with np.errstate(divide="ignore"):
                l2_new, l2_past = jnp.asarray(np.log2(c_new)), jnp.asarray(np.log2(c_past))
            o_s = _dil_decode2(qkv_s, l2_new, l2_past,
                               jnp.transpose(state_dil_k[i], (0, 2, 3, 1)),
                               jnp.transpose(state_dil_v[i], (0, 2, 3, 1)))
            xs = _matmul_post([o_s.reshape(1, BS * TS, D_C)], [w_out], xs, gt_s, g_post, tm=BS * TS)
            outs.setdefault("dk_s", []).append(qkv_s[:, :, D_C:2 * D_C].reshape(BS, TS, H_C, HEAD_DIM))
            outs.setdefault("dv_s", []).append(qkv_s[:, :, 2 * D_C:].reshape(BS, TS, H_C, HEAD_DIM))

        (sh_p, sc_p, gt_p), (sh_s, sc_s, gt_s) = _mods(mod[2 * l + 1], BP, TS)
        g_pre = norm_pre[l, 1][None]
        g_post = norm_post[l, 1][None]
        w_up = mlp_up[l].astype(bf16)
        w_down = mlp_down[l].astype(bf16)
        xp = _mlp(xp, sh_p, sc_p, gt_p, g_pre, g_post, w_up, w_down, tm=2048, tf=256)
        xs = _mlp(xs, sh_s, sc_s, gt_s, g_pre, g_post, w_up, w_down, tm=BS * TS, tf=1024)

    st = lambda name: jnp.stack(outs[name])
    return (xp, xs.reshape(BS, TS, D), st("fk_p"), st("fv_p"), st("fl_p"), st("fk_s"), st("fv_s"), st("fl_s"),
            st("wkv_p"), st("wkv_s"), st("sh_p"), st("sh_s"), st("dk_p"), st("dv_p"), st("dk_s"), st("dv_s"))
```

```python
import functools

import numpy as np
import jax
import jax.numpy as jnp
from jax import lax
from jax.experimental import pallas as pl
from jax.experimental.pallas import tpu as pltpu

f32 = jnp.float32
bf16 = jnp.bfloat16

D_MODEL = 1024
HEAD_DIM = 64
H_A = 8
H_B = 8
H_C = 16
D_A = H_A * HEAD_DIM
D_B = H_B * HEAD_DIM
D_C = H_C * HEAD_DIM
LORA_W = 64
LORA_A = 64
LORA_G = 128
RW_COLS = 3 * D_A + LORA_W + LORA_A + LORA_G
D_FF = 4 * D_MODEL
PAGE_SIZE = 128
DIL_PAIRS = ((128, 1), (512, 4), (2048, 16))
ROPE_THETA = 10000.0
NORM_EPS = 1e-6
LNX_EPS = 64e-5
ATTN_SCALE = HEAD_DIM ** -0.5
LOG2E = 1.4426950408889634

LANES = 128
SUBLANES = 8
VMEM_LIMIT = 56 * 1024 * 1024

EV_Q_OFF = 2048
EV_K_OFF = 2560
EV_V_OFF = 3072
EV_F_OFF = 3584
EV_COLS_PAD = 3840


def _cparams(sem):
    return pltpu.CompilerParams(dimension_semantics=sem, vmem_limit_bytes=VMEM_LIMIT)


def _split_hi_lo(x):
    hi = x.astype(bf16)
    lo = (x - hi.astype(f32)).astype(bf16)
    return hi, lo


def _seg_sum(x, bd):
    hi, lo = _split_hi_lo(x)
    return (jnp.dot(hi, bd, preferred_element_type=f32) + jnp.dot(lo, bd, preferred_element_type=f32))


def _seg_sum_paired(x, bd2):
    n = x.shape[0] // 2
    xb = x.astype(bf16)
    out = jnp.dot(jnp.concatenate([xb[:n], xb[n:]], axis=1), bd2, preferred_element_type=f32)
    return jnp.concatenate([out[:, :LANES], out[:, LANES:]], axis=0)


def _rmsnorm_rows(x, g):
    ms = jnp.mean(x * x, axis=-1, keepdims=True)
    return x * lax.rsqrt(ms + NORM_EPS) * g


def _softplus(z):
    return jnp.maximum(z, 0.0) + jnp.log(1.0 + jnp.exp(-jnp.abs(z)))


def _ada_kernel(c_ref, w_ref, b_ref, o_ref):
    c = c_ref[...]
    s = (c * jax.nn.sigmoid(c)).astype(bf16)
    o_ref[0] = jnp.dot(s, w_ref[0].astype(bf16), preferred_element_type=f32) + b_ref[0]


def _ada_mod(c_all, ada_w, ada_b):
    n = c_all.shape[0]
    nl = ada_w.shape[0] * ada_w.shape[1]
    w = ada_w.reshape(nl, D_MODEL, 3 * D_MODEL)
    b = ada_b.reshape(nl, 1, 3 * D_MODEL)
    tn = 1024
    return pl.pallas_call(
        _ada_kernel,
        grid=(nl, 3 * D_MODEL // tn),
        in_specs=[pl.BlockSpec((n, D_MODEL), lambda l, j: (0, 0)),
                  pl.BlockSpec((1, D_MODEL, tn), lambda l, j: (l, 0, j)),
                  pl.BlockSpec((1, 1, tn), lambda l, j: (l, 0, j))],
        out_specs=pl.BlockSpec((1, n, tn), lambda l, j: (l, 0, j)),
        out_shape=jax.ShapeDtypeStruct((nl, n, 3 * D_MODEL), f32),
        compiler_params=_cparams(("parallel", "parallel")),
        name="ada_mod",
    )(c_all, w, b)


def _rope_tile(x, cos, sin_signed):
    lane = lax.broadcasted_iota(jnp.int32, (x.shape[0], LANES), 1)
    first = (lane % HEAD_DIM) < (HEAD_DIM // 2)
    outs = []
    for gidx in range(x.shape[1] // LANES):
        xg = x[:, gidx * LANES:(gidx + 1) * LANES]
        partner = jnp.where(first, pltpu.roll(xg, LANES - HEAD_DIM // 2, 1), pltpu.roll(xg, HEAD_DIM // 2, 1))
        outs.append(xg * cos + partner * sin_signed)
    return jnp.concatenate(outs, axis=1) if len(outs) > 1 else outs[0]


def _norm_mm_kernel(x_ref, sh_ref, sc_ref, g_ref, w_ref, *rest, n_rope_tiles, n_split):
    if n_rope_tiles:
        cos_ref, sin_ref = rest[:2]
        rest = rest[2:]
    o_refs = rest[:-1]
    h_scr = rest[-1]
    j = pl.program_id(2)

    @pl.when(j == 0)
    def _():
        h = _rmsnorm_rows(x_ref[0], g_ref[...]) * (1.0 + sc_ref[0]) + sh_ref[0]
        h_scr[...] = h.astype(bf16)

    acc = jnp.dot(h_scr[...], w_ref[...].astype(bf16), preferred_element_type=f32)
    if n_split:
        for jj in range(n_split):
            @pl.when(j == jj)
            def _(jj=jj):
                o_refs[jj][0] = _rope_tile(acc, cos_ref[...], sin_ref[...]) if jj < n_rope_tiles else acc
    elif n_rope_tiles:
        @pl.when(j < n_rope_tiles)
        def _():
            o_refs[0][0] = _rope_tile(acc, cos_ref[...], sin_ref[...])

        @pl.when(j >= n_rope_tiles)
        def _():
            o_refs[0][0] = acc
    else:
        o_refs[0][0] = acc


def _norm_matmul(x, shift, scale, g, w, *, tm, tn, rope=None, n_rope_cols=0, split=False):
    B, T, D = x.shape
    N = w.shape[1]
    per_row = shift.shape[1] != 1
    mod_spec = (pl.BlockSpec((1, tm, D), lambda b, i, j: (b, i, 0)) if per_row
                else pl.BlockSpec((1, 1, D), lambda b, i, j: (b, 0, 0)))
    in_specs = [pl.BlockSpec((1, tm, D), lambda b, i, j: (b, i, 0)), mod_spec, mod_spec,
                pl.BlockSpec((1, D), lambda b, i, j: (0, 0)),
                pl.BlockSpec((D, tn), lambda b, i, j: (0, j))]
    args = [x, shift, scale, g, w]
    n_rope_tiles = 0
    if rope is not None:
        cos, sin = rope
        n_rope_tiles = n_rope_cols // tn
        if cos.shape[0] == T:
            rspec = pl.BlockSpec((tm, LANES), lambda b, i, j: (i, 0))
        else:
            nt = T // tm
            rspec = pl.BlockSpec((tm, LANES), lambda b, i, j: (b * nt + i, 0))
        in_specs += [rspec, rspec]
        args += [cos, sin]
    n_split = N // tn if split else 0
    if split:
        out_specs = [pl.BlockSpec((1, tm, tn), lambda b, i, j: (b, i, 0))] * n_split
        out_shape = [jax.ShapeDtypeStruct((B, T, tn), f32)] * n_split
    else:
        out_specs = pl.BlockSpec((1, tm, tn), lambda b, i, j: (b, i, j))
        out_shape = jax.ShapeDtypeStruct((B, T, N), f32)
    return pl.pallas_call(
        functools.partial(_norm_mm_kernel, n_rope_tiles=n_rope_tiles, n_split=n_split),
        grid=(B, T // tm, N // tn),
        in_specs=in_specs,
        out_specs=out_specs,
        out_shape=out_shape,
        scratch_shapes=[pltpu.VMEM((tm, D), bf16)],
        compiler_params=_cparams(("parallel", "parallel", "arbitrary")),
        name="norm_matmul",
    )(*args)


def _mm_post_kernel(*refs, n_in):
    a_refs = refs[:n_in]
    w_refs = refs[n_in:2 * n_in]
    x_ref, gate_ref, g_ref, o_ref = refs[2 * n_in:]
    y = None
    for a_ref, w_ref in zip(a_refs, w_refs):
        part = jnp.dot(a_ref[0].astype(bf16), w_ref[...], preferred_element_type=f32)
        y = part if y is None else y + part
    o_ref[0] = x_ref[0] + gate_ref[0] * _rmsnorm_rows(y, g_ref[...])


def _matmul_post(a_list, w_list, x, gate, g, *, tm):
    B, T, D = x.shape
    per_row = gate.shape[1] != 1
    gate_spec = (pl.BlockSpec((1, tm, D), lambda b, i: (b, i, 0)) if per_row
                 else pl.BlockSpec((1, 1, D), lambda b, i: (b, 0, 0)))
    in_specs = []
    for a in a_list:
        ka = a.shape[2]
        in_specs.append(pl.BlockSpec((1, tm, ka), lambda b, i: (b, i, 0)))
    for w in w_list:
        in_specs.append(pl.BlockSpec(w.shape, lambda b, i: (0, 0)))
    in_specs += [pl.BlockSpec((1, tm, D), lambda b, i: (b, i, 0)), gate_spec,
                 pl.BlockSpec((1, D), lambda b, i: (0, 0))]
    return pl.pallas_call(
        functools.partial(_mm_post_kernel, n_in=len(a_list)),
        grid=(B, T // tm),
        in_specs=in_specs,
        out_specs=pl.BlockSpec((1, tm, D), lambda b, i: (b, i, 0)),
        out_shape=jax.ShapeDtypeStruct((B, T, D), f32),
        compiler_params=_cparams(("parallel", "parallel")),
        name="matmul_post",
    )(*a_list, *w_list, x, gate, g)


def _mlp_kernel(x_ref, sh_ref, sc_ref, gate_ref, gpre_ref, gpost_ref, wu_ref, wd_ref, o_ref, h_scr, acc_scr):
    k = pl.program_id(2)

    @pl.when(k == 0)
    def _():
        h = _rmsnorm_rows(x_ref[0], gpre_ref[...]) * (1.0 + sc_ref[0]) + sh_ref[0]
        h_scr[...] = h.astype(bf16)
        acc_scr[...] = jnp.zeros_like(acc_scr)

    u = jnp.dot(h_scr[...], wu_ref[...].astype(bf16), preferred_element_type=f32)
    a = jnp.square(jnp.maximum(u, 0.0)).astype(bf16)
    acc_scr[...] += jnp.dot(a, wd_ref[...].astype(bf16), preferred_element_type=f32)

    @pl.when(k == pl.num_programs(2) - 1)
    def _():
        o_ref[0] = x_ref[0] + gate_ref[0] * _rmsnorm_rows(acc_scr[...], gpost_ref[...])


def _mlp(x, shift, scale, gate, g_pre, g_post, w_up, w_down, *, tm, tf):
    B, T, D = x.shape
    per_row = shift.shape[1] != 1
    mod_spec = (pl.BlockSpec((1, tm, D), lambda b, i, k: (b, i, 0)) if per_row
                else pl.BlockSpec((1, 1, D), lambda b, i, k: (b, 0, 0)))
    vec_spec = pl.BlockSpec((1, D), lambda b, i, k: (0, 0))
    return pl.pallas_call(
        _mlp_kernel,
        grid=(B, T // tm, D_FF // tf),
        in_specs=[pl.BlockSpec((1, tm, D), lambda b, i, k: (b, i, 0)), mod_spec, mod_spec, mod_spec,
                  vec_spec, vec_spec,
                  pl.BlockSpec((D, tf), lambda b, i, k: (0, k)),
                  pl.BlockSpec((tf, D), lambda b, i, k: (k, 0))],
        out_specs=pl.BlockSpec((1, tm, D), lambda b, i, k: (b, i, 0)),
        out_shape=jax.ShapeDtypeStruct((B, T, D), f32),
        scratch_shapes=[pltpu.VMEM((tm, D), bf16), pltpu.VMEM((tm, D), f32)],
        compiler_params=_cparams(("parallel", "parallel", "arbitrary")),
        name="mlp",
    )(x, shift, scale, gate, g_pre, g_post, w_up, w_down)


def _rwkv_pre_kernel(p_ref, sh0_ref, mu_ref, w0_ref, w2_ref, a0_ref, a2_ref, g2_ref, kk_w_ref, ka_ref,
                     rk_ref, bd_ref, r_o, w_o, k_o, v_o, kk_o, kka_o, g_o, bon_o, carry):
    ti = pl.program_id(1)

    @pl.when(ti == 0)
    def _():
        carry[...] = sh0_ref[0]

    p = p_ref[0]
    tt = p.shape[0]
    row = lax.broadcasted_iota(jnp.int32, (tt, 1), 0)
    prev = jnp.where(row == 0, carry[...], pltpu.roll(p, 1, 0))
    carry[...] = p[tt - 1:tt, :]
    ps = p + mu_ref[...] * (prev - p)
    r = ps[:, 0:D_A]
    k = ps[:, D_A:2 * D_A]
    v = ps[:, 2 * D_A:3 * D_A]
    o = 3 * D_A
    dw = ps[:, o:o + LORA_W]
    da = ps[:, o + LORA_W:o + LORA_W + LORA_A]
    dg = ps[:, o + LORA_W + LORA_A:o + LORA_W + LORA_A + LORA_G]
    bd = bd_ref[...]
    wl = w0_ref[...] + jnp.dot(jnp.tanh(dw).astype(bf16), w2_ref[...], preferred_element_type=f32)
    w_log = -_softplus(-wl) - 0.5
    decay = jnp.exp(-jnp.exp(w_log))
    a = jax.nn.sigmoid(a0_ref[...] + jnp.dot(da.astype(bf16), a2_ref[...], preferred_element_type=f32))
    g = jnp.dot(jax.nn.sigmoid(dg).astype(bf16), g2_ref[...], preferred_element_type=f32)
    kk = k * kk_w_ref[...]
    nrm = jnp.sqrt(_seg_sum(kk * kk, bd))
    kk = kk / jnp.maximum(nrm, 1e-12)
    k2 = k * (1.0 + (a - 1.0) * ka_ref[...])
    bonus = _seg_sum(r * k2 * rk_ref[...], bd) * v
    r_o[0] = r
    w_o[0] = decay
    k_o[0] = k2
    v_o[0] = v
    kk_o[0] = kk
    kka_o[0] = kk * a
    g_o[0] = g
    bon_o[0] = bonus


def _rwkv_pre(proj, shift0, mu, w0, w2, a0, a2, g2, k_k, k_a, r_k, bd512, *, tt):
    B, T, _ = proj.shape
    vec = lambda n: pl.BlockSpec((1, n), lambda b, i: (0, 0))
    mat = lambda s: pl.BlockSpec(s, lambda b, i: (0, 0))
    out_spec = pl.BlockSpec((1, tt, D_A), lambda b, i: (b, i, 0))
    out_sds = jax.ShapeDtypeStruct((B, T, D_A), f32)
    return pl.pallas_call(
        _rwkv_pre_kernel,
        grid=(B, T // tt),
        in_specs=[pl.BlockSpec((1, tt, RW_COLS), lambda b, i: (b, i, 0)),
                  pl.BlockSpec((1, 1, RW_COLS), lambda b, i: (b, 0, 0)),
                  vec(RW_COLS), vec(D_A), mat((LORA_W, D_A)), vec(D_A), mat((LORA_A, D_A)),
                  mat((LORA_G, D_A)), vec(D_A), vec(D_A), vec(D_A), mat((D_A, D_A))],
        out_specs=[out_spec] * 8,
        out_shape=[out_sds] * 8,
        scratch_shapes=[pltpu.VMEM((1, RW_COLS), f32)],
        compiler_params=_cparams(("parallel", "arbitrary")),
        name="rwkv_pre",
    )(proj, shift0, mu, w0, w2, a0, a2, g2, k_k, k_a, r_k, bd512)


SCAN_NB = 8
PAIRS = H_A // 2


def _wkv_scan_kernel(r_ref, w_ref, k_ref, v_ref, kk_ref, kka_ref, s0_ref, bd_ref, e_ref,
                     y_ref, st_ref, s_scr):
    c = pl.program_id(1)
    n_steps = r_ref.shape[1]
    rows_per_b = PAIRS * HEAD_DIM

    @pl.when(c == 0)
    def _():
        for b in range(SCAN_NB):
            for p in range(PAIRS):
                r0 = (b * PAIRS + p) * HEAD_DIM
                s_scr[r0:r0 + HEAD_DIM, :] = jnp.concatenate([s0_ref[b, 2 * p], s0_ref[b, 2 * p + 1]], axis=1)

    bd = bd_ref[...]
    e4 = e_ref[...]

    def step8(t8, carry):
        t0 = pl.multiple_of(t8 * SUBLANES, SUBLANES)
        names = (("r", r_ref), ("w", w_ref), ("k", k_ref), ("v", v_ref), ("kk", kk_ref), ("kka", kka_ref))
        blks = [{name: ref[b, pl.ds(t0, SUBLANES), :] for name, ref in names} for b in range(SCAN_NB)]
        yrows = [[] for _ in range(SCAN_NB)]
        e_all = jnp.concatenate([e4] * SCAN_NB, axis=0)
        for j in range(SUBLANES):
            def rows(name):
                parts = [jnp.broadcast_to(blks[b][name][j:j + 1, LANES * p:LANES * (p + 1)], (HEAD_DIM, LANES))
                         for b in range(SCAN_NB) for p in range(PAIRS)]
                return jnp.concatenate(parts, axis=0)

            s = s_scr[...]
            sk = _seg_sum_paired(s * rows("kk"), bd)
            vcol = _seg_sum_paired(e_all * rows("v"), bd)
            s = s * rows("w") - sk * rows("kka") + vcol * rows("k")
            s_scr[...] = s
            yrep = _seg_sum_paired(s * rows("r"), bd)
            yrow = jnp.sum((e_all * yrep).reshape(SCAN_NB * PAIRS, HEAD_DIM, LANES), axis=1)
            for b in range(SCAN_NB):
                yrows[b].append(jnp.concatenate([yrow[b * PAIRS + p:b * PAIRS + p + 1, :] for p in range(PAIRS)],
                                                axis=1))
        for b in range(SCAN_NB):
            y_ref[b, pl.ds(t0, SUBLANES), :] = jnp.concatenate(yrows[b], axis=0)
        return carry

    lax.fori_loop(0, n_steps // SUBLANES, step8, 0)

    @pl.when(c == pl.num_programs(1) - 1)
    def _():
        for b in range(SCAN_NB):
            for p in range(PAIRS):
                r0 = (b * PAIRS + p) * HEAD_DIM
                tile = s_scr[r0:r0 + HEAD_DIM, :]
                st_ref[b, 2 * p] = tile[:, 0:HEAD_DIM]
                st_ref[b, 2 * p + 1] = tile[:, HEAD_DIM:]


def _wkv_scan(r, w, k, v, kk, kka, s0, bd256, e4, *, chunk):
    NB, T, _ = r.shape
    rows = SCAN_NB * PAIRS * HEAD_DIM
    xspec = pl.BlockSpec((SCAN_NB, chunk, D_A), lambda g, c: (g, c, 0))
    sspec = pl.BlockSpec((SCAN_NB, H_A, HEAD_DIM, HEAD_DIM), lambda g, c: (g, 0, 0, 0))
    return pl.pallas_call(
        _wkv_scan_kernel,
        grid=(NB // SCAN_NB, T // chunk),
        in_specs=[xspec] * 6 + [sspec,
                                pl.BlockSpec(bd256.shape, lambda g, c: (0, 0)),
                                pl.BlockSpec((PAIRS * HEAD_DIM, LANES), lambda g, c: (0, 0))],
        out_specs=[xspec, sspec],
        out_shape=[jax.ShapeDtypeStruct((NB, T, D_A), f32),
                   jax.ShapeDtypeStruct((NB, H_A, HEAD_DIM, HEAD_DIM), f32)],
        scratch_shapes=[pltpu.VMEM((rows, LANES), f32)],
        compiler_params=_cparams(("parallel", "arbitrary")),
        name="wkv_scan",
    )(r, w, k, v, kk, kka, s0, bd256, e4)


def _rwkv_post_kernel(y_ref, g_ref, bon_ref, lw_ref, lb_ref, bd_ref, o_ref):
    y = y_ref[0]
    bd = bd_ref[...]
    mean = _seg_sum(y, bd) * (1.0 / HEAD_DIM)
    d = y - mean
    var = _seg_sum(d * d, bd) * (1.0 / HEAD_DIM)
    yn = d * lax.rsqrt(var + LNX_EPS) * lw_ref[...] + lb_ref[...]
    o_ref[0] = (yn + bon_ref[0]) * g_ref[0]


def _rwkv_post(y, g, bonus, lnx_w, lnx_b, bd512, *, tt):
    B, T, _ = y.shape
    spec = pl.BlockSpec((1, tt, D_A), lambda b, i: (b, i, 0))
    vec = pl.BlockSpec((1, D_A), lambda b, i: (0, 0))
    return pl.pallas_call(
        _rwkv_post_kernel,
        grid=(B, T // tt),
        in_specs=[spec, spec, spec, vec, vec, pl.BlockSpec((D_A, D_A), lambda b, i: (0, 0))],
        out_specs=spec,
        out_shape=jax.ShapeDtypeStruct((B, T, D_A), f32),
        compiler_params=_cparams(("parallel", "parallel")),
        name="rwkv_post",
    )(y, g, bonus, lnx_w, lnx_b, bd512)


def _fox_gate_kernel(f_ref, b_ref, lf_ref, cum_ref):
    z = f_ref[0] + b_ref[...]
    lf = -_softplus(-z)
    lf_ref[0] = lf
    T = lf.shape[0]
    row = lax.broadcasted_iota(jnp.int32, (T, 1), 0)
    x = lf
    s = 1
    while s < T:
        x = x + jnp.where(row >= s, pltpu.roll(x, s, 0), 0.0)
        s *= 2
    cum_ref[0] = x


def _fox_gate(proj, b_f_pad):
    B, T, _ = proj.shape
    spec = pl.BlockSpec((1, T, LANES), lambda b: (b, 0, 0))
    sds = jax.ShapeDtypeStruct((B, T, LANES), f32)
    return pl.pallas_call(
        _fox_gate_kernel,
        grid=(B,),
        in_specs=[pl.BlockSpec((1, T, LANES), lambda b: (b, 0, EV_F_OFF // LANES)),
                  pl.BlockSpec((1, LANES), lambda b: (0, 0))],
        out_specs=[spec, spec],
        out_shape=[sds, sds],
        compiler_params=_cparams(("parallel",)),
        name="fox_gate",
    )(proj, b_f_pad)


def _flash_kernel(q_ref, k_ref, v_ref, l2c_ref, *rest, n_heads, tq, tk, has_bias, table_everywhere, copy_kv,
                  head_group):
    if has_bias:
        cq_ref, ckt_ref = rest[:2]
        rest = rest[2:]
    o_ref = rest[0]
    rest = rest[1:]
    qi = pl.program_id(1)
    if copy_kv:
        ko_ref, vo_ref = rest[:2]
        rest = rest[2:]

        @pl.when(qi == 0)
        def _():
            ko_ref[0] = k_ref[0]
            vo_ref[0] = v_ref[0]
    if has_bias:
        m_scr, acc_scr, qb_scr, cq2_scr = rest
    else:
        m_scr, acc_scr, qb_scr = rest
    m_scr[...] = jnp.full_like(m_scr, -jnp.inf)
    acc_scr[...] = jnp.zeros_like(acc_scr)
    low_q = lax.broadcasted_iota(jnp.int32, (tq, LANES), 1) < HEAD_DIM
    low_k = lax.broadcasted_iota(jnp.int32, (tk, LANES), 1) < HEAD_DIM
    for h in range(n_heads):
        q_pair = q_ref[0, :, (h // 2) * LANES:(h // 2 + 1) * LANES] * (ATTN_SCALE * LOG2E)
        qb_scr[h] = jnp.where(low_q == (h % 2 == 0), q_pair, 0.0).astype(bf16)
        if has_bias:
            cq2_scr[h] = jnp.broadcast_to(cq_ref[0, :, h:h + 1] * LOG2E, (tq, LANES))
    nt = (((1,), (1,)), ((), ()))

    def block(it, use_table):
        k0 = pl.multiple_of((qi - it) * tk, tk)
        for g0 in range(0, n_heads, head_group):
            heads = range(g0, g0 + head_group)
            pairs = range(g0 // 2, (g0 + head_group) // 2)
            parts = []
            for p in pairs:
                k_pair = k_ref[0, pl.ds(k0, tk), p * LANES:(p + 1) * LANES].astype(bf16)
                q2 = qb_scr[2 * p:2 * p + 2].reshape(2 * tq, LANES)
                s2 = lax.dot_general(q2, k_pair, nt, preferred_element_type=f32)
                for n in range(2):
                    s = s2[n * tq:(n + 1) * tq]
                    if has_bias:
                        s = s - ckt_ref[0, 2 * p + n:2 * p + n + 1, pl.ds(k0, tk)] * LOG2E
                    parts.append(s)
            s3 = jnp.stack(parts)
            if use_table:
                s3 = s3 + l2c_ref[it][None]
            m_prev = m_scr[g0:g0 + head_group]
            m_cur = jnp.max(s3, axis=2, keepdims=True)
            if has_bias:
                cq2 = cq2_scr[g0:g0 + head_group]
                m_next = jnp.maximum(m_prev, m_cur + cq2)
                m_sub = m_next - cq2
            else:
                m_next = jnp.maximum(m_prev, m_cur)
                m_sub = m_next
            p3 = jnp.exp2(s3 - jnp.concatenate([m_sub] * (tk // LANES), axis=2)).astype(bf16)
            alpha = jnp.exp2(m_prev - m_next)
            m_scr[g0:g0 + head_group] = m_next
            v_pairs = {p: v_ref[0, pl.ds(k0, tk), p * LANES:(p + 1) * LANES].astype(bf16) for p in pairs}
            for n, h in enumerate(heads):
                v_ext = jnp.where(low_k == (h % 2 == 0), v_pairs[h // 2], 1.0)
                acc_scr[h] = acc_scr[h] * alpha[n] + jnp.dot(p3[n], v_ext, preferred_element_type=f32)

    block(0, True)

    def body(it, carry):
        block(it, table_everywhere)
        return carry

    lax.fori_loop(1, qi + 1, body, 0)
    for p in range(n_heads // 2):
        acc_e, acc_o = acc_scr[2 * p], acc_scr[2 * p + 1]
        out_e = acc_e / pltpu.roll(acc_e, HEAD_DIM, 1)
        out_o = acc_o / pltpu.roll(acc_o, HEAD_DIM, 1)
        o_ref[0, :, p * LANES:(p + 1) * LANES] = jnp.where(low_q, out_e, out_o)


def _flash_attention(q_src, k_src, v_src, q_blk, k_blk, v_blk, l2c, n_heads, *, tq, tk, bias=None,
                     table_everywhere, head_group, copy_kv=False):
    assert tq == tk
    B, T, _ = q_src.shape
    hd = n_heads * HEAD_DIM
    in_specs = [pl.BlockSpec((1, tq, hd), lambda b, i: (b, i, q_blk)),
                pl.BlockSpec((1, T, hd), lambda b, i: (b, 0, k_blk)),
                pl.BlockSpec((1, T, hd), lambda b, i: (b, 0, v_blk)),
                pl.BlockSpec(l2c.shape, lambda b, i: (0, 0, 0))]
    args = [q_src, k_src, v_src, l2c]
    if bias is not None:
        cum, cum_t = bias
        in_specs += [pl.BlockSpec((1, tq, LANES), lambda b, i: (b, i, 0)),
                     pl.BlockSpec((1, cum_t.shape[1], T), lambda b, i: (b, 0, 0))]
        args += [cum, cum_t]
    scratch = [pltpu.VMEM((n_heads, tq, LANES), f32), pltpu.VMEM((n_heads, tq, LANES), f32),
               pltpu.VMEM((n_heads, tq, LANES), bf16)]
    if bias is not None:
        scratch.append(pltpu.VMEM((n_heads, tq, LANES), f32))
    out_specs = pl.BlockSpec((1, tq, hd), lambda b, i: (b, i, 0))
    out_shape = jax.ShapeDtypeStruct((B, T, hd), f32)
    if copy_kv:
        kv_out = pl.BlockSpec((1, T, hd), lambda b, i: (b, 0, 0))
        out_specs = [out_specs, kv_out, kv_out]
        out_shape = [out_shape] * 3
    return pl.pallas_call(
        functools.partial(_flash_kernel, n_heads=n_heads, tq=tq, tk=tk, has_bias=bias is not None,
                          table_everywhere=table_everywhere, copy_kv=copy_kv, head_group=head_group),
        grid=(B, T // tq),
        in_specs=in_specs,
        out_specs=out_specs,
        out_shape=out_shape,
        scratch_shapes=scratch,
        compiler_params=_cparams(("parallel", "arbitrary")),
        name="flash_attention",
    )(*args)


def _qk(q_h, k_h):
    return lax.dot_general(q_h, k_h, (((1,), (1,)), ((), ())), preferred_element_type=f32)


FOX_PP = 32
DIL_KC = 1024


def _block_diag_queries(q2, n_heads):
    nq, hd = q2.shape
    qt = jnp.concatenate([q2] * n_heads, axis=0)
    rh = lax.broadcasted_iota(jnp.int32, (n_heads * nq, hd), 0) // nq
    ch = lax.broadcasted_iota(jnp.int32, (n_heads * nq, hd), 1) // HEAD_DIM
    return jnp.where(rh == ch, qt, 0.0).astype(bf16)


def _rows_per_head(x, nq):
    n_heads, n = x.shape
    return jnp.broadcast_to(x[:, None, :], (n_heads, nq, n)).reshape(n_heads * nq, n)


def _decode_first(s, v_new, m_scr, l_scr, accn_scr, acct_scr):
    m = jnp.max(s, axis=1, keepdims=True)
    p = jnp.exp2(s - m)
    m_scr[...] = jnp.broadcast_to(m, m_scr.shape)
    l_scr[...] = jnp.broadcast_to(jnp.sum(p, axis=1, keepdims=True), l_scr.shape)
    accn_scr[...] = jnp.dot(p.astype(bf16), v_new, preferred_element_type=f32)
    acct_scr[...] = jnp.zeros_like(acct_scr)


def _decode_update(s, vt_all, m_scr, l_scr, accn_scr, acct_scr):
    R = s.shape[0]
    hd = vt_all.shape[0]
    m_prev = m_scr[...]
    m_next = jnp.maximum(m_prev, jnp.max(s, axis=1, keepdims=True))
    p = jnp.exp2(s - m_next[:, 0:1])
    alpha = jnp.exp2(m_prev - m_next)
    l_scr[...] = alpha * l_scr[...] + jnp.sum(p, axis=1, keepdims=True)
    m_scr[...] = m_next
    accn_scr[...] = accn_scr[...] * jnp.concatenate([alpha] * (hd // LANES), axis=1)
    new_t = lax.dot_general(vt_all, p.astype(bf16), (((1,), (1,)), ((), ())), preferred_element_type=f32)
    if R < LANES:
        alpha = jnp.concatenate([alpha, jnp.zeros((LANES - R, LANES), f32)], axis=0)
    alpha_row = jnp.transpose(alpha)[0:1, 0:R]
    acct_scr[:, 0:R] = acct_scr[:, 0:R] * alpha_row + new_t


def _decode_finish(o_ref, n_heads, nq, l_scr, accn_scr, acct_scr):
    for h in range(n_heads):
        blk = acct_scr[h * HEAD_DIM:(h + 1) * HEAD_DIM, :]
        sq = jnp.transpose(jnp.concatenate([blk, jnp.zeros((LANES - HEAD_DIM, LANES), f32)], axis=0))
        rows = slice(h * nq, (h + 1) * nq)
        cols = slice(h * HEAD_DIM, (h + 1) * HEAD_DIM)
        num = sq[rows, 0:HEAD_DIM] + accn_scr[rows, cols]
        o_ref[0, :, cols] = num / l_scr[rows, 0:HEAD_DIM]


def _fox_decode2_kernel(pt_ref, q_ref, kn_ref, vn_ref, cqr_ref, cnt_ref, *rest):
    lf_refs = rest[:FOX_PP]
    k_refs = rest[FOX_PP:2 * FOX_PP]
    v_refs = rest[2 * FOX_PP:3 * FOX_PP]
    o_ref, m_scr, l_scr, accn_scr, acct_scr, tot_scr, qbd_scr = rest[3 * FOX_PP:]
    j = pl.program_id(1)
    nq = q_ref.shape[1]
    R = H_B * nq
    cq2 = cqr_ref[0] * LOG2E

    @pl.when(j == 0)
    def _():
        qbd = _block_diag_queries(q_ref[0] * (ATTN_SCALE * LOG2E), H_B)
        qbd_scr[...] = qbd
        tot_scr[...] = jnp.zeros_like(tot_scr)
        s = _qk(qbd, kn_ref[0].astype(bf16))
        s = s + cq2[:, 0:nq] - _rows_per_head(cnt_ref[0][:, 0:nq] * LOG2E, nq)
        qi = lax.broadcasted_iota(jnp.int32, (R, nq), 0) % nq
        kj = lax.broadcasted_iota(jnp.int32, (R, nq), 1)
        s = jnp.where(kj <= qi, s, -jnp.inf)
        _decode_first(s, vn_ref[0].astype(bf16), m_scr, l_scr, accn_scr, acct_scr)

    lane = lax.broadcasted_iota(jnp.int32, (H_B, PAGE_SIZE), 1)
    qbd = qbd_scr[...]
    tot = tot_scr[...]
    s_parts = []
    v_parts = []
    for i in range(FOX_PP):
        x = lf_refs[i][0]
        sft = 1
        while sft < PAGE_SIZE:
            x = x + jnp.where(lane >= sft, pltpu.roll(x, sft, 1), 0.0)
            sft *= 2
        page_tot = x[:, PAGE_SIZE - 1:PAGE_SIZE]
        bias2 = _rows_per_head((page_tot - x + tot) * LOG2E, nq) + cq2
        tot = tot + page_tot
        kt_all = k_refs[i][0].reshape(D_B, PAGE_SIZE).astype(bf16)
        s_parts.append(jnp.dot(qbd, kt_all, preferred_element_type=f32) + bias2)
        v_parts.append(v_refs[i][0].reshape(D_B, PAGE_SIZE).astype(bf16))
    tot_scr[...] = tot
    _decode_update(jnp.concatenate(s_parts, axis=1), jnp.concatenate(v_parts, axis=1),
                   m_scr, l_scr, accn_scr, acct_scr)

    @pl.when(j == pl.num_programs(1) - 1)
    def _():
        _decode_finish(o_ref, H_B, nq, l_scr, accn_scr, acct_scr)


def _fox_decode2(page_table, proj, cum_rows, cum_new_t, lf_cache_t, k_cache, v_cache):
    NB, nq, _ = proj.shape
    n_pages = page_table.shape[1]
    steps = n_pages // FOX_PP
    R = H_B * nq

    def lf_spec(i):
        return pl.BlockSpec((1, H_B, PAGE_SIZE),
                            lambda b, j, pt: (pt[b, n_pages - 1 - (j * FOX_PP + i)], 0, 0))

    def kv_spec(i):
        return pl.BlockSpec((1, H_B, HEAD_DIM, PAGE_SIZE),
                            lambda b, j, pt: (pt[b, n_pages - 1 - (j * FOX_PP + i)], 0, 0, 0))

    in_specs = [pl.BlockSpec((1, nq, D_B), lambda b, j, pt: (b, 0, EV_Q_OFF // D_B)),
                pl.BlockSpec((1, nq, D_B), lambda b, j, pt: (b, 0, EV_K_OFF // D_B)),
                pl.BlockSpec((1, nq, D_B), lambda b, j, pt: (b, 0, EV_V_OFF // D_B)),
                pl.BlockSpec((1, R, LANES), lambda b, j, pt: (b, 0, 0)),
                pl.BlockSpec((1, SUBLANES, LANES), lambda b, j, pt: (b, 0, 0))]
    args = [proj, proj, proj, cum_rows, cum_new_t]
    for i in range(FOX_PP):
        in_specs.append(lf_spec(i))
        args.append(lf_cache_t)
    for cache in (k_cache, v_cache):
        for i in range(FOX_PP):
            in_specs.append(kv_spec(i))
            args.append(cache)
    grid_spec = pltpu.PrefetchScalarGridSpec(
        num_scalar_prefetch=1,
        grid=(NB, steps),
        in_specs=in_specs,
        out_specs=pl.BlockSpec((1, nq, D_B), lambda b, j, pt: (b, 0, 0)),
        scratch_shapes=[pltpu.VMEM((R, LANES), f32), pltpu.VMEM((R, LANES), f32),
                        pltpu.VMEM((R, D_B), f32), pltpu.VMEM((D_B, LANES), f32),
                        pltpu.VMEM((H_B, LANES), f32), pltpu.VMEM((R, D_B), bf16)],
    )
    return pl.pallas_call(
        _fox_decode2_kernel,
        grid_spec=grid_spec,
        out_shape=jax.ShapeDtypeStruct((NB, nq, D_B), f32),
        compiler_params=_cparams(("parallel", "arbitrary")),
        name="fox_decode",
    )(page_table, *args)


def _dil_decode2_kernel(q_ref, kn_ref, vn_ref, l2new_ref, l2past_ref, k_ref, v_ref, o_ref,
                        m_scr, l_scr, accn_scr, acct_scr, qbd_scr):
    j = pl.program_id(1)
    nq = q_ref.shape[1]

    @pl.when(j == 0)
    def _():
        qbd = _block_diag_queries(q_ref[0] * (ATTN_SCALE * LOG2E), H_C)
        qbd_scr[...] = qbd
        s = _qk(qbd, kn_ref[0].astype(bf16)) + jnp.concatenate([l2new_ref[:, 0:nq]] * H_C, axis=0)
        _decode_first(s, vn_ref[0].astype(bf16), m_scr, l_scr, accn_scr, acct_scr)

    kt_all = k_ref[0].reshape(D_C, DIL_KC).astype(bf16)
    s = jnp.dot(qbd_scr[...], kt_all, preferred_element_type=f32)
    s = s + jnp.concatenate([l2past_ref[...]] * H_C, axis=0)
    _decode_update(s, v_ref[0].reshape(D_C, DIL_KC).astype(bf16), m_scr, l_scr, accn_scr, acct_scr)

    @pl.when(j == pl.num_programs(1) - 1)
    def _():
        _decode_finish(o_ref, H_C, nq, l_scr, accn_scr, acct_scr)


def _dil_decode2(qkv, l2_new, l2_past, k_state, v_state):
    NB, nq, _ = qkv.shape
    L = k_state.shape[3]
    R = H_C * nq
    kv_spec = pl.BlockSpec((1, H_C, HEAD_DIM, DIL_KC), lambda b, j: (b, 0, 0, j))
    return pl.pallas_call(
        _dil_decode2_kernel,
        grid=(NB, L // DIL_KC),
        in_specs=[pl.BlockSpec((1, nq, D_C), lambda b, j: (b, 0, 0)),
                  pl.BlockSpec((1, nq, D_C), lambda b, j: (b, 0, 1)),
                  pl.BlockSpec((1, nq, D_C), lambda b, j: (b, 0, 2)),
                  pl.BlockSpec(l2_new.shape, lambda b, j: (0, 0)),
                  pl.BlockSpec((nq, DIL_KC), lambda b, j: (0, j)),
                  kv_spec, kv_spec],
        out_specs=pl.BlockSpec((1, nq, D_C), lambda b, j: (b, 0, 0)),
        out_shape=jax.ShapeDtypeStruct((NB, nq, D_C), f32),
        scratch_shapes=[pltpu.VMEM((R, LANES), f32), pltpu.VMEM((R, LANES), f32),
                        pltpu.VMEM((R, D_C), f32), pltpu.VMEM((D_C, LANES), f32),
                        pltpu.VMEM((R, D_C), bf16)],
        compiler_params=_cparams(("parallel", "arbitrary")),
        name="dil_decode",
    )(qkv, qkv, qkv, l2_new, l2_past, k_state, v_state)


def _dil_multiplicity(delta):
    delta = np.asarray(delta)
    c = np.zeros(delta.shape, np.float32)
    for w, d in DIL_PAIRS:
        c += ((delta >= 0) & (delta % d == 0) & (delta <= w)).astype(np.float32)
    return c


def _log2_block_tables(n_blocks, t, mult_fn):
    i = np.arange(t)[:, None]
    j = np.arange(t)[None, :]
    with np.errstate(divide="ignore"):
        tabs = [np.log2(mult_fn(i - j + d * t)).astype(np.float32) for d in range(n_blocks)]
    return jnp.asarray(np.stack(tabs))


def _rope_tables(pos):
    half = HEAD_DIM // 2
    inv = ROPE_THETA ** (-jnp.arange(half, dtype=f32) / half)
    ang = pos.astype(f32)[:, None] * inv[None, :]
    cos = jnp.cos(ang)
    sin = jnp.sin(ang)
    return jnp.tile(cos, (1, 4)), jnp.concatenate([-sin, sin, -sin, sin], axis=1)


def _mods(mod_lj, n_prompt, per_row_repeat):
    mp = mod_lj[:n_prompt][:, None, :]
    ms = jnp.repeat(mod_lj[n_prompt:], per_row_repeat, axis=0)[None]
    sp = [mp[..., i * D_MODEL:(i + 1) * D_MODEL] for i in range(3)]
    ss = [ms[..., i * D_MODEL:(i + 1) * D_MODEL] for i in range(3)]
    return sp, ss


def kernel(x_prompt, x_sample, cache_fox_k, cache_fox_v, cache_fox_logf, state_rwkv_wkv, state_rwkv_shift,
           state_dil_k, state_dil_v, page_table, c_prompt, c_sample, ada_w, ada_b, norm_pre, norm_post,
           mlp_up, mlp_down, ev_w_in, ev_w_out, rw_mu, rw_w0, rw_w2, rw_a0, rw_a2, rw_g2, rw_k_k, rw_k_a,
           rw_r_k, rw_lnx_w, rw_lnx_b, fox_b_f, od_w_in, od_w_out):
    BP, T, D = x_prompt.shape
    BS, TS, _ = x_sample.shape
    depth = ada_w.shape[0]
    past_len = page_table.shape[1] * PAGE_SIZE

    mod = _ada_mod(jnp.concatenate([c_prompt, c_sample], axis=0), ada_w, ada_b)

    idx = np.arange(LANES)
    idx2 = np.arange(2 * LANES)
    bd256 = jnp.asarray((idx2[:, None] // HEAD_DIM == idx2[None, :] // HEAD_DIM).astype(np.float32)).astype(bf16)
    idx5 = np.arange(D_A)
    bd512 = jnp.asarray((idx5[:, None] // HEAD_DIM == idx5[None, :] // HEAD_DIM).astype(np.float32)).astype(bf16)
    e4 = jnp.asarray(np.tile((idx[None, :] % HEAD_DIM == np.arange(HEAD_DIM)[:, None]).astype(np.float32),
                             (PAIRS, 1)))
    TQ = 256
    causal_tab = _log2_block_tables(1, TQ, lambda dl: (dl >= 0).astype(np.float32))
    dil_tab = _log2_block_tables(T // TQ, TQ, _dil_multiplicity)

    xp = x_prompt
    xs = x_sample.reshape(1, BS * TS, D)
    outs = {}
    for l in range(depth):
        i = l // 2
        (sh_p, sc_p, gt_p), (sh_s, sc_s, gt_s) = _mods(mod[2 * l], BP, TS)
        g_pre = norm_pre[l, 0][None]
        g_post = norm_post[l, 0][None]
        if l % 2 == 0:
            w_in = ev_w_in[i]
            zeros = lambda n: jnp.zeros((D, n), f32)
            w_pad = jnp.concatenate([w_in[:, :RW_COLS], zeros(EV_Q_OFF - RW_COLS), w_in[:, RW_COLS:],
                                     zeros(EV_COLS_PAD - EV_F_OFF - H_B)], axis=1).astype(bf16)
            w_out = ev_w_out[i].astype(bf16)
            b_f_pad = jnp.pad(fox_b_f[i], (0, LANES - H_B))[None]
            rw_args = (rw_mu[i][None], rw_w0[i][None], rw_w2[i].astype(bf16), rw_a0[i][None],
                       rw_a2[i].astype(bf16), rw_g2[i].astype(bf16), rw_k_k[i][None], rw_k_a[i][None],
                       rw_r_k[i].reshape(1, D_A), bd512)

            proj_p = _norm_matmul(xp, sh_p, sc_p, g_pre, w_pad, tm=2048, tn=768)
            r, w, k2, v, kk, kka, g, bonus = _rwkv_pre(proj_p, jnp.zeros((BP, 1, RW_COLS), f32), *rw_args, tt=256)
            y, st = _wkv_scan(r, w, k2, v, kk, kka, jnp.zeros((BP, H_A, HEAD_DIM, HEAD_DIM), f32), bd256, e4,
                              chunk=64)
            ya = _rwkv_post(y, g, bonus, rw_lnx_w[i][None], rw_lnx_b[i][None], bd512, tt=512)
            lf_p, cum_p = _fox_gate(proj_p, b_f_pad)
            cum_t = jnp.transpose(cum_p[:, :, :SUBLANES], (0, 2, 1))
            yb, fk, fv = _flash_attention(proj_p, proj_p, proj_p, EV_Q_OFF // D_B, EV_K_OFF // D_B,
                                          EV_V_OFF // D_B, causal_tab, H_B, tq=TQ, tk=TQ, bias=(cum_p, cum_t),
                                          table_everywhere=False, head_group=8, copy_kv=True)
            xp = _matmul_post([ya, yb], [w_out[:D_A], w_out[D_A:]], xp, gt_p, g_post, tm=512)
            outs.setdefault("fk_p", []).append(fk.reshape(BP, T, H_B, HEAD_DIM))
            outs.setdefault("fv_p", []).append(fv.reshape(BP, T, H_B, HEAD_DIM))
            outs.setdefault("fl_p", []).append(lf_p[:, :, :H_B])
            outs.setdefault("wkv_p", []).append(st)
            outs.setdefault("sh_p", []).append(proj_p[:, T - 1, :RW_COLS])

            proj_s = _norm_matmul(xs, sh_s, sc_s, g_pre, w_pad, tm=BS * TS, tn=1280).reshape(BS, TS, EV_COLS_PAD)
            r, w, k2, v, kk, kka, g, bonus = _rwkv_pre(proj_s, state_rwkv_shift[i][:, None, :], *rw_args, tt=TS)
            y, st = _wkv_scan(r, w, k2, v, kk, kka, state_rwkv_wkv[i], bd256, e4, chunk=TS)
            ya = _rwkv_post(y, g, bonus, rw_lnx_w[i][None], rw_lnx_b[i][None], bd512, tt=TS)
            lf_s, cum_s = _fox_gate(proj_s, b_f_pad)
            cum_s_t = jnp.pad(jnp.transpose(cum_s[:, :, :SUBLANES], (0, 2, 1)), ((0, 0), (0, 0), (0, LANES - TS)))
            lf_cache_t = jnp.transpose(cache_fox_logf[i], (0, 2, 1))
            cum_rows = jnp.broadcast_to(
                jnp.transpose(cum_s[:, :, :H_B], (0, 2, 1)).reshape(BS, H_B * TS, 1), (BS, H_B * TS, LANES))
            yb = _fox_decode2(page_table, proj_s, cum_rows, cum_s_t, lf_cache_t,
                              jnp.transpose(cache_fox_k[i], (0, 2, 3, 1)),
                              jnp.transpose(cache_fox_v[i], (0, 2, 3, 1)))
            xs = _matmul_post([ya.reshape(1, BS * TS, D_A), yb.reshape(1, BS * TS, D_B)],
                              [w_out[:D_A], w_out[D_A:]], xs, gt_s, g_post, tm=BS * TS)
            outs.setdefault("fk_s", []).append(proj_s[:, :, EV_K_OFF:EV_K_OFF + D_B].reshape(BS, TS, H_B, HEAD_DIM))
            outs.setdefault("fv_s", []).append(proj_s[:, :, EV_V_OFF:EV_V_OFF + D_B].reshape(BS, TS, H_B, HEAD_DIM))
            outs.setdefault("fl_s", []).append(lf_s[:, :, :H_B])
            outs.setdefault("wkv_s", []).append(st)
            outs.setdefault("sh_s", []).append(proj_s[:, TS - 1, :RW_COLS])
        else:
            w_in = od_w_in[i].astype(bf16)
            w_out = od_w_out[i].astype(bf16)
            rope_p = _rope_tables(jnp.arange(T))
            cos_s, sin_s = _rope_tables(past_len + jnp.arange(TS))
            rope_s = (jnp.tile(cos_s, (BS, 1)), jnp.tile(sin_s, (BS, 1)))

            q_p, k_p, v_p = _norm_matmul(xp, sh_p, sc_p, g_pre, w_in, tm=1024, tn=D_C, rope=rope_p,
                                         n_rope_cols=2 * D_C, split=True)
            o_p = _flash_attention(q_p, k_p, v_p, 0, 0, 0, dil_tab, H_C, tq=TQ, tk=TQ, table_everywhere=True,
                                   head_group=4)
            xp = _matmul_post([o_p], [w_out], xp, gt_p, g_post, tm=512)
            keep = min(DIL_PAIRS[-1][0], T)
            outs.setdefault("dk_p", []).append(k_p[:, T - keep:].reshape(BP, keep, H_C, HEAD_DIM))
            outs.setdefault("dv_p", []).append(v_p[:, T - keep:].reshape(BP, keep, H_C, HEAD_DIM))

            qkv_s = _norm_matmul(xs, sh_s, sc_s, g_pre, w_in, tm=BS * TS, tn=D_C, rope=rope_s,
                                 n_rope_cols=2 * D_C).reshape(BS, TS, 3 * D_C)
            L = state_dil_k.shape[2]
            tq_idx = np.arange(TS)[:, None]
            c_new = np.zeros((TS, LANES), np.float32)
            c_new[:, :TS] = _dil_multiplicity(tq_idx - np.arange(TS)[None, :])
            c_past = _dil_multiplicity(L + tq_idx - np.arange(L)[None, :])
            with np.errstate(divide="ignore"):
                l2_new, l2_past = jnp.asarray(np.log2(c_new)), jnp.asarray(np.log2(c_past))
            o_s = _dil_decode2(qkv_s, l2_new, l2_past,
                               jnp.transpose(state_dil_k[i], (0, 2, 3, 1)),
                               jnp.transpose(state_dil_v[i], (0, 2, 3, 1)))
            xs = _matmul_post([o_s.reshape(1, BS * TS, D_C)], [w_out], xs, gt_s, g_post, tm=BS * TS)
            outs.setdefault("dk_s", []).append(qkv_s[:, :, D_C:2 * D_C].reshape(BS, TS, H_C, HEAD_DIM))
            outs.setdefault("dv_s", []).append(qkv_s[:, :, 2 * D_C:].reshape(BS, TS, H_C, HEAD_DIM))

        (sh_p, sc_p, gt_p), (sh_s, sc_s, gt_s) = _mods(mod[2 * l + 1], BP, TS)
        g_pre = norm_pre[l, 1][None]
        g_post = norm_post[l, 1][None]
        w_up = mlp_up[l]
        w_down = mlp_down[l]
        xp = _mlp(xp, sh_p, sc_p, gt_p, g_pre, g_post, w_up, w_down, tm=1024, tf=1024)
        xs = _mlp(xs, sh_s, sc_s, gt_s, g_pre, g_post, w_up, w_down, tm=BS * TS, tf=1024)

    st = lambda name: jnp.stack(outs[name])
    return (xp, xs.reshape(BS, TS, D), st("fk_p"), st("fv_p"), st("fl_p"), st("fk_s"), st("fv_s"), st("fl_s"),
            st("wkv_p"), st("wkv_s"), st("sh_p"), st("sh_s"), st("dk_p"), st("dv_p"), st("dk_s"), st("dv_s"))
```

```python
import functools

import numpy as np
import jax
import jax.numpy as jnp
from jax import lax
from jax.experimental import pallas as pl
from jax.experimental.pallas import tpu as pltpu

f32 = jnp.float32
bf16 = jnp.bfloat16

D_MODEL = 1024
HEAD_DIM = 64
H_A = 8
H_B = 8
H_C = 16
D_A = H_A * HEAD_DIM
D_B = H_B * HEAD_DIM
D_C = H_C * HEAD_DIM
LORA_W = 64
LORA_A = 64
LORA_G = 128
RW_COLS = 3 * D_A + LORA_W + LORA_A + LORA_G
D_FF = 4 * D_MODEL
PAGE_SIZE = 128
DIL_PAIRS = ((128, 1), (512, 4), (2048, 16))
ROPE_THETA = 10000.0
NORM_EPS = 1e-6
LNX_EPS = 64e-5
ATTN_SCALE = HEAD_DIM ** -0.5
LOG2E = 1.4426950408889634

LANES = 128
SUBLANES = 8
VMEM_LIMIT = 56 * 1024 * 1024

EV_Q_OFF = 2048
EV_K_OFF = 2560
EV_V_OFF = 3072
EV_F_OFF = 3584
EV_COLS_PAD = 3840


def _cparams(sem):
    return pltpu.CompilerParams(dimension_semantics=sem, vmem_limit_bytes=VMEM_LIMIT)


def _split_hi_lo(x):
    hi = x.astype(bf16)
    lo = (x - hi.astype(f32)).astype(bf16)
    return hi, lo


def _seg_sum(x, bd):
    hi, lo = _split_hi_lo(x)
    return (jnp.dot(hi, bd, preferred_element_type=f32) + jnp.dot(lo, bd, preferred_element_type=f32))


def _seg_sum_paired(x, bd2):
    n = x.shape[0] // 2
    xb = x.astype(bf16)
    out = jnp.dot(jnp.concatenate([xb[:n], xb[n:]], axis=1), bd2, preferred_element_type=f32)
    return jnp.concatenate([out[:, :LANES], out[:, LANES:]], axis=0)


def _rmsnorm_rows(x, g):
    ms = jnp.mean(x * x, axis=-1, keepdims=True)
    return x * lax.rsqrt(ms + NORM_EPS) * g


def _softplus(z):
    return jnp.maximum(z, 0.0) + jnp.log(1.0 + jnp.exp(-jnp.abs(z)))


def _ada_kernel(c_ref, w_ref, b_ref, o_ref):
    c = c_ref[...]
    s = (c * jax.nn.sigmoid(c)).astype(bf16)
    o_ref[0] = jnp.dot(s, w_ref[0].astype(bf16), preferred_element_type=f32) + b_ref[0]


def _ada_mod(c_all, ada_w, ada_b):
    n = c_all.shape[0]
    nl = ada_w.shape[0] * ada_w.shape[1]
    w = ada_w.reshape(nl, D_MODEL, 3 * D_MODEL)
    b = ada_b.reshape(nl, 1, 3 * D_MODEL)
    tn = 1024
    return pl.pallas_call(
        _ada_kernel,
        grid=(nl, 3 * D_MODEL // tn),
        in_specs=[pl.BlockSpec((n, D_MODEL), lambda l, j: (0, 0)),
                  pl.BlockSpec((1, D_MODEL, tn), lambda l, j: (l, 0, j)),
                  pl.BlockSpec((1, 1, tn), lambda l, j: (l, 0, j))],
        out_specs=pl.BlockSpec((1, n, tn), lambda l, j: (l, 0, j)),
        out_shape=jax.ShapeDtypeStruct((nl, n, 3 * D_MODEL), f32),
        compiler_params=_cparams(("parallel", "parallel")),
        name="ada_mod",
    )(c_all, w, b)


def _rope_tile(x, cos, sin_signed):
    lane = lax.broadcasted_iota(jnp.int32, (x.shape[0], LANES), 1)
    first = (lane % HEAD_DIM) < (HEAD_DIM // 2)
    outs = []
    for gidx in range(x.shape[1] // LANES):
        xg = x[:, gidx * LANES:(gidx + 1) * LANES]
        partner = jnp.where(first, pltpu.roll(xg, LANES - HEAD_DIM // 2, 1), pltpu.roll(xg, HEAD_DIM // 2, 1))
        outs.append(xg * cos + partner * sin_signed)
    return jnp.concatenate(outs, axis=1) if len(outs) > 1 else outs[0]


def _norm_mm_kernel(x_ref, sh_ref, sc_ref, g_ref, w_ref, *rest, n_rope_tiles, n_split):
    if n_rope_tiles:
        cos_ref, sin_ref = rest[:2]
        rest = rest[2:]
    o_refs = rest[:-1]
    h_scr = rest[-1]
    j = pl.program_id(2)

    @pl.when(j == 0)
    def _():
        h = _rmsnorm_rows(x_ref[0], g_ref[...]) * (1.0 + sc_ref[0]) + sh_ref[0]
        h_scr[...] = h.astype(bf16)

    acc = jnp.dot(h_scr[...], w_ref[...].astype(bf16), preferred_element_type=f32)
    if n_split:
        for jj in range(n_split):
            @pl.when(j == jj)
            def _(jj=jj):
                o_refs[jj][0] = _rope_tile(acc, cos_ref[...], sin_ref[...]) if jj < n_rope_tiles else acc
    elif n_rope_tiles:
        @pl.when(j < n_rope_tiles)
        def _():
            o_refs[0][0] = _rope_tile(acc, cos_ref[...], sin_ref[...])

        @pl.when(j >= n_rope_tiles)
        def _():
            o_refs[0][0] = acc
    else:
        o_refs[0][0] = acc


def _norm_matmul(x, shift, scale, g, w, *, tm, tn, rope=None, n_rope_cols=0, split=False):
    B, T, D = x.shape
    N = w.shape[1]
    per_row = shift.shape[1] != 1
    mod_spec = (pl.BlockSpec((1, tm, D), lambda b, i, j: (b, i, 0)) if per_row
                else pl.BlockSpec((1, 1, D), lambda b, i, j: (b, 0, 0)))
    in_specs = [pl.BlockSpec((1, tm, D), lambda b, i, j: (b, i, 0)), mod_spec, mod_spec,
                pl.BlockSpec((1, D), lambda b, i, j: (0, 0)),
                pl.BlockSpec((D, tn), lambda b, i, j: (0, j))]
    args = [x, shift, scale, g, w]
    n_rope_tiles = 0
    if rope is not None:
        cos, sin = rope
        n_rope_tiles = n_rope_cols // tn
        if cos.shape[0] == T:
            rspec = pl.BlockSpec((tm, LANES), lambda b, i, j: (i, 0))
        else:
            nt = T // tm
            rspec = pl.BlockSpec((tm, LANES), lambda b, i, j: (b * nt + i, 0))
        in_specs += [rspec, rspec]
        args += [cos, sin]
    n_split = N // tn if split else 0
    if split:
        out_specs = [pl.BlockSpec((1, tm, tn), lambda b, i, j: (b, i, 0))] * n_split
        out_shape = [jax.ShapeDtypeStruct((B, T, tn), f32)] * n_split
    else:
        out_specs = pl.BlockSpec((1, tm, tn), lambda b, i, j: (b, i, j))
        out_shape = jax.ShapeDtypeStruct((B, T, N), f32)
    return pl.pallas_call(
        functools.partial(_norm_mm_kernel, n_rope_tiles=n_rope_tiles, n_split=n_split),
        grid=(B, T // tm, N // tn),
        in_specs=in_specs,
        out_specs=out_specs,
        out_shape=out_shape,
        scratch_shapes=[pltpu.VMEM((tm, D), bf16)],
        compiler_params=_cparams(("parallel", "parallel", "arbitrary")),
        name="norm_matmul",
    )(*args)


def _mm_post_kernel(*refs, n_in):
    a_refs = refs[:n_in]
    w_refs = refs[n_in:2 * n_in]
    x_ref, gate_ref, g_ref, o_ref = refs[2 * n_in:]
    y = None
    for a_ref, w_ref in zip(a_refs, w_refs):
        part = jnp.dot(a_ref[0].astype(bf16), w_ref[...], preferred_element_type=f32)
        y = part if y is None else y + part
    o_ref[0] = x_ref[0] + gate_ref[0] * _rmsnorm_rows(y, g_ref[...])


def _matmul_post(a_list, w_list, x, gate, g, *, tm):
    B, T, D = x.shape
    per_row = gate.shape[1] != 1
    gate_spec = (pl.BlockSpec((1, tm, D), lambda b, i: (b, i, 0)) if per_row
                 else pl.BlockSpec((1, 1, D), lambda b, i: (b, 0, 0)))
    in_specs = []
    for a in a_list:
        ka = a.shape[2]
        in_specs.append(pl.BlockSpec((1, tm, ka), lambda b, i: (b, i, 0)))
    for w in w_list:
        in_specs.append(pl.BlockSpec(w.shape, lambda b, i: (0, 0)))
    in_specs += [pl.BlockSpec((1, tm, D), lambda b, i: (b, i, 0)), gate_spec,
                 pl.BlockSpec((1, D), lambda b, i: (0, 0))]
    return pl.pallas_call(
        functools.partial(_mm_post_kernel, n_in=len(a_list)),
        grid=(B, T // tm),
        in_specs=in_specs,
        out_specs=pl.BlockSpec((1, tm, D), lambda b, i: (b, i, 0)),
        out_shape=jax.ShapeDtypeStruct((B, T, D), f32),
        compiler_params=_cparams(("parallel", "parallel")),
        name="matmul_post",
    )(*a_list, *w_list, x, gate, g)


def _mlp_kernel(x_ref, sh_ref, sc_ref, gate_ref, gpre_ref, gpost_ref, wu_ref, wd_ref, o_ref, h_scr, acc_scr):
    k = pl.program_id(2)

    @pl.when(k == 0)
    def _():
        h = _rmsnorm_rows(x_ref[0], gpre_ref[...]) * (1.0 + sc_ref[0]) + sh_ref[0]
        h_scr[...] = h.astype(bf16)
        acc_scr[...] = jnp.zeros_like(acc_scr)

    u = jnp.dot(h_scr[...], wu_ref[...].astype(bf16), preferred_element_type=f32)
    a = jnp.square(jnp.maximum(u, 0.0)).astype(bf16)
    acc_scr[...] += jnp.dot(a, wd_ref[...].astype(bf16), preferred_element_type=f32)

    @pl.when(k == pl.num_programs(2) - 1)
    def _():
        o_ref[0] = x_ref[0] + gate_ref[0] * _rmsnorm_rows(acc_scr[...], gpost_ref[...])


def _mlp(x, shift, scale, gate, g_pre, g_post, w_up, w_down, *, tm, tf):
    B, T, D = x.shape
    per_row = shift.shape[1] != 1
    mod_spec = (pl.BlockSpec((1, tm, D), lambda b, i, k: (b, i, 0)) if per_row
                else pl.BlockSpec((1, 1, D), lambda b, i, k: (b, 0, 0)))
    vec_spec = pl.BlockSpec((1, D), lambda b, i, k: (0, 0))
    return pl.pallas_call(
        _mlp_kernel,
        grid=(B, T // tm, D_FF // tf),
        in_specs=[pl.BlockSpec((1, tm, D), lambda b, i, k: (b, i, 0)), mod_spec, mod_spec, mod_spec,
                  vec_spec, vec_spec,
                  pl.BlockSpec((D, tf), lambda b, i, k: (0, k)),
                  pl.BlockSpec((tf, D), lambda b, i, k: (k, 0))],
        out_specs=pl.BlockSpec((1, tm, D), lambda b, i, k: (b, i, 0)),
        out_shape=jax.ShapeDtypeStruct((B, T, D), f32),
        scratch_shapes=[pltpu.VMEM((tm, D), bf16), pltpu.VMEM((tm, D), f32)],
        compiler_params=_cparams(("parallel", "parallel", "arbitrary")),
        name="mlp",
    )(x, shift, scale, gate, g_pre, g_post, w_up, w_down)


def _rwkv_pre_kernel(p_ref, sh0_ref, mu_ref, w0_ref, w2_ref, a0_ref, a2_ref, g2_ref, kk_w_ref, ka_ref,
                     rk_ref, bd_ref, r_o, w_o, k_o, v_o, kk_o, kka_o, g_o, bon_o, carry):
    ti = pl.program_id(1)

    @pl.when(ti == 0)
    def _():
        carry[...] = sh0_ref[...]

    nb, tt, _ = p_ref.shape
    p3 = p_ref[...]
    p = p3.reshape(nb * tt, RW_COLS)
    first = lax.broadcasted_iota(jnp.int32, (nb * tt, 1), 0) % tt == 0
    last_prev = jnp.broadcast_to(carry[...], (nb, tt, RW_COLS)).reshape(nb * tt, RW_COLS)
    prev = jnp.where(first, last_prev, pltpu.roll(p, 1, 0))
    carry[...] = p3[:, tt - 1:tt, :]
    ps = p + mu_ref[...] * (prev - p)
    r = ps[:, 0:D_A]
    k = ps[:, D_A:2 * D_A]
    v = ps[:, 2 * D_A:3 * D_A]
    o = 3 * D_A
    dw = ps[:, o:o + LORA_W]
    da = ps[:, o + LORA_W:o + LORA_W + LORA_A]
    dg = ps[:, o + LORA_W + LORA_A:o + LORA_W + LORA_A + LORA_G]
    bd = bd_ref[...]
    wl = w0_ref[...] + jnp.dot(jnp.tanh(dw).astype(bf16), w2_ref[...], preferred_element_type=f32)
    w_log = -_softplus(-wl) - 0.5
    decay = jnp.exp(-jnp.exp(w_log))
    a = jax.nn.sigmoid(a0_ref[...] + jnp.dot(da.astype(bf16), a2_ref[...], preferred_element_type=f32))
    g = jnp.dot(jax.nn.sigmoid(dg).astype(bf16), g2_ref[...], preferred_element_type=f32)
    kk = k * kk_w_ref[...]
    nrm = jnp.sqrt(_seg_sum(kk * kk, bd))
    kk = kk / jnp.maximum(nrm, 1e-12)
    k2 = k * (1.0 + (a - 1.0) * ka_ref[...])
    bonus = _seg_sum(r * k2 * rk_ref[...], bd) * v
    for o_ref, val in ((r_o, r), (w_o, decay), (k_o, k2), (v_o, v), (kk_o, kk), (kka_o, kk * a), (g_o, g),
                       (bon_o, bonus)):
        o_ref[...] = val.reshape(nb, tt, D_A)


def _rwkv_pre(proj, shift0, mu, w0, w2, a0, a2, g2, k_k, k_a, r_k, bd512, *, tt, nb=1):
    B, T, _ = proj.shape
    vec = lambda n: pl.BlockSpec((1, n), lambda b, i: (0, 0))
    mat = lambda s: pl.BlockSpec(s, lambda b, i: (0, 0))
    out_spec = pl.BlockSpec((nb, tt, D_A), lambda b, i: (b, i, 0))
    out_sds = jax.ShapeDtypeStruct((B, T, D_A), f32)
    return pl.pallas_call(
        _rwkv_pre_kernel,
        grid=(B // nb, T // tt),
        in_specs=[pl.BlockSpec((nb, tt, RW_COLS), lambda b, i: (b, i, 0)),
                  pl.BlockSpec((nb, 1, RW_COLS), lambda b, i: (b, 0, 0)),
                  vec(RW_COLS), vec(D_A), mat((LORA_W, D_A)), vec(D_A), mat((LORA_A, D_A)),
                  mat((LORA_G, D_A)), vec(D_A), vec(D_A), vec(D_A), mat((D_A, D_A))],
        out_specs=[out_spec] * 8,
        out_shape=[out_sds] * 8,
        scratch_shapes=[pltpu.VMEM((nb, 1, RW_COLS), f32)],
        compiler_params=_cparams(("parallel", "arbitrary")),
        name="rwkv_pre",
    )(proj, shift0, mu, w0, w2, a0, a2, g2, k_k, k_a, r_k, bd512)


SCAN_NB = 8
PAIRS = H_A // 2


def _wkv_scan_kernel(r_ref, w_ref, k_ref, v_ref, kk_ref, kka_ref, s0_ref, bd_ref, e_ref,
                     y_ref, st_ref, s_scr):
    c = pl.program_id(1)
    n_steps = r_ref.shape[1]
    rows_per_b = PAIRS * HEAD_DIM

    @pl.when(c == 0)
    def _():
        for b in range(SCAN_NB):
            for p in range(PAIRS):
                r0 = (b * PAIRS + p) * HEAD_DIM
                s_scr[r0:r0 + HEAD_DIM, :] = jnp.concatenate([s0_ref[b, 2 * p], s0_ref[b, 2 * p + 1]], axis=1)

    bd = bd_ref[...]
    e4 = e_ref[...]

    def step8(t8, carry):
        t0 = pl.multiple_of(t8 * SUBLANES, SUBLANES)
        names = (("r", r_ref), ("w", w_ref), ("k", k_ref), ("v", v_ref), ("kk", kk_ref), ("kka", kka_ref))
        blks = [{name: ref[b, pl.ds(t0, SUBLANES), :] for name, ref in names} for b in range(SCAN_NB)]
        yrows = [[] for _ in range(SCAN_NB)]
        e_all = jnp.concatenate([e4] * SCAN_NB, axis=0)
        for j in range(SUBLANES):
            def rows(name):
                parts = [jnp.broadcast_to(blks[b][name][j:j + 1, LANES * p:LANES * (p + 1)], (HEAD_DIM, LANES))
                         for b in range(SCAN_NB) for p in range(PAIRS)]
                return jnp.concatenate(parts, axis=0)

            s = s_scr[...]
            sk = _seg_sum_paired(s * rows("kk"), bd)
            vcol = _seg_sum_paired(e_all * rows("v"), bd)
            s = s * rows("w") - sk * rows("kka") + vcol * rows("k")
            s_scr[...] = s
            yrep = _seg_sum_paired(s * rows("r"), bd)
            yrow = jnp.sum((e_all * yrep).reshape(SCAN_NB * PAIRS, HEAD_DIM, LANES), axis=1)
            for b in range(SCAN_NB):
                yrows[b].append(jnp.concatenate([yrow[b * PAIRS + p:b * PAIRS + p + 1, :] for p in range(PAIRS)],
                                                axis=1))
        for b in range(SCAN_NB):
            y_ref[b, pl.ds(t0, SUBLANES), :] = jnp.concatenate(yrows[b], axis=0)
        return carry

    lax.fori_loop(0, n_steps // SUBLANES, step8, 0)

    @pl.when(c == pl.num_programs(1) - 1)
    def _():
        for b in range(SCAN_NB):
            for p in range(PAIRS):
                r0 = (b * PAIRS + p) * HEAD_DIM
                tile = s_scr[r0:r0 + HEAD_DIM, :]
                st_ref[b, 2 * p] = tile[:, 0:HEAD_DIM]
                st_ref[b, 2 * p + 1] = tile[:, HEAD_DIM:]


def _wkv_scan(r, w, k, v, kk, kka, s0, bd256, e4, *, chunk):
    NB, T, _ = r.shape
    rows = SCAN_NB * PAIRS * HEAD_DIM
    xspec = pl.BlockSpec((SCAN_NB, chunk, D_A), lambda g, c: (g, c, 0))
    sspec = pl.BlockSpec((SCAN_NB, H_A, HEAD_DIM, HEAD_DIM), lambda g, c: (g, 0, 0, 0))
    return pl.pallas_call(
        _wkv_scan_kernel,
        grid=(NB // SCAN_NB, T // chunk),
        in_specs=[xspec] * 6 + [sspec,
                                pl.BlockSpec(bd256.shape, lambda g, c: (0, 0)),
                                pl.BlockSpec((PAIRS * HEAD_DIM, LANES), lambda g, c: (0, 0))],
        out_specs=[xspec, sspec],
        out_shape=[jax.ShapeDtypeStruct((NB, T, D_A), f32),
                   jax.ShapeDtypeStruct((NB, H_A, HEAD_DIM, HEAD_DIM), f32)],
        scratch_shapes=[pltpu.VMEM((rows, LANES), f32)],
        compiler_params=_cparams(("parallel", "arbitrary")),
        name="wkv_scan",
    )(r, w, k, v, kk, kka, s0, bd256, e4)


def _rwkv_post_kernel(y_ref, g_ref, bon_ref, lw_ref, lb_ref, bd_ref, o_ref):
    y = y_ref[0]
    bd = bd_ref[...]
    mean = _seg_sum(y, bd) * (1.0 / HEAD_DIM)
    d = y - mean
    var = _seg_sum(d * d, bd) * (1.0 / HEAD_DIM)
    yn = d * lax.rsqrt(var + LNX_EPS) * lw_ref[...] + lb_ref[...]
    o_ref[0] = (yn + bon_ref[0]) * g_ref[0]


def _rwkv_post(y, g, bonus, lnx_w, lnx_b, bd512, *, tt):
    B, T, _ = y.shape
    spec = pl.BlockSpec((1, tt, D_A), lambda b, i: (b, i, 0))
    vec = pl.BlockSpec((1, D_A), lambda b, i: (0, 0))
    return pl.pallas_call(
        _rwkv_post_kernel,
        grid=(B, T // tt),
        in_specs=[spec, spec, spec, vec, vec, pl.BlockSpec((D_A, D_A), lambda b, i: (0, 0))],
        out_specs=spec,
        out_shape=jax.ShapeDtypeStruct((B, T, D_A), f32),
        compiler_params=_cparams(("parallel", "parallel")),
        name="rwkv_post",
    )(y, g, bonus, lnx_w, lnx_b, bd512)


def _fox_gate_kernel(f_ref, b_ref, lf_ref, cum_ref):
    nb, T, _ = f_ref.shape
    z = f_ref[...].reshape(nb * T, LANES) + b_ref[...]
    lf = -_softplus(-z)
    lf_ref[...] = lf.reshape(nb, T, LANES)
    t = lax.broadcasted_iota(jnp.int32, (nb * T, 1), 0) % T
    x = lf
    s = 1
    while s < T:
        x = x + jnp.where(t >= s, pltpu.roll(x, s, 0), 0.0)
        s *= 2
    cum_ref[...] = x.reshape(nb, T, LANES)


def _fox_gate(proj, b_f_pad, *, nb=1):
    B, T, _ = proj.shape
    spec = pl.BlockSpec((nb, T, LANES), lambda b: (b, 0, 0))
    sds = jax.ShapeDtypeStruct((B, T, LANES), f32)
    return pl.pallas_call(
        _fox_gate_kernel,
        grid=(B // nb,),
        in_specs=[pl.BlockSpec((nb, T, LANES), lambda b: (b, 0, EV_F_OFF // LANES)),
                  pl.BlockSpec((1, LANES), lambda b: (0, 0))],
        out_specs=[spec, spec],
        out_shape=[sds, sds],
        compiler_params=_cparams(("parallel",)),
        name="fox_gate",
    )(proj, b_f_pad)


def _flash_kernel(q_ref, k_ref, v_ref, l2c_ref, *rest, n_heads, tq, tk, has_bias, table_everywhere, copy_kv,
                  head_group):
    if has_bias:
        cq_ref, ckt_ref = rest[:2]
        rest = rest[2:]
    o_ref = rest[0]
    rest = rest[1:]
    qi = pl.program_id(1)
    if copy_kv:
        ko_ref, vo_ref = rest[:2]
        rest = rest[2:]

        @pl.when(qi == 0)
        def _():
            ko_ref[0] = k_ref[0]
            vo_ref[0] = v_ref[0]
    if has_bias:
        m_scr, acc_scr, qb_scr, cq2_scr = rest
    else:
        m_scr, acc_scr, qb_scr = rest
    m_scr[...] = jnp.full_like(m_scr, -jnp.inf)
    acc_scr[...] = jnp.zeros_like(acc_scr)
    low_q = lax.broadcasted_iota(jnp.int32, (tq, LANES), 1) < HEAD_DIM
    low_k = lax.broadcasted_iota(jnp.int32, (tk, LANES), 1) < HEAD_DIM
    for h in range(n_heads):
        q_pair = q_ref[0, :, (h // 2) * LANES:(h // 2 + 1) * LANES] * (ATTN_SCALE * LOG2E)
        qb_scr[h] = jnp.where(low_q == (h % 2 == 0), q_pair, 0.0).astype(bf16)
        if has_bias:
            cq2_scr[h] = jnp.broadcast_to(cq_ref[0, :, h:h + 1] * LOG2E, (tq, LANES))
    nt = (((1,), (1,)), ((), ()))

    def block(it, use_table):
        k0 = pl.multiple_of((qi - it) * tk, tk)
        for g0 in range(0, n_heads, head_group):
            heads = range(g0, g0 + head_group)
            pairs = range(g0 // 2, (g0 + head_group) // 2)
            parts = []
            for p in pairs:
                k_pair = k_ref[0, pl.ds(k0, tk), p * LANES:(p + 1) * LANES].astype(bf16)
                q2 = qb_scr[2 * p:2 * p + 2].reshape(2 * tq, LANES)
                s2 = lax.dot_general(q2, k_pair, nt, preferred_element_type=f32)
                for n in range(2):
                    s = s2[n * tq:(n + 1) * tq]
                    if has_bias:
                        s = s - ckt_ref[0, 2 * p + n:2 * p + n + 1, pl.ds(k0, tk)] * LOG2E
                    parts.append(s)
            s3 = jnp.stack(parts)
            if use_table:
                s3 = s3 + l2c_ref[it][None]
            m_prev = m_scr[g0:g0 + head_group]
            m_cur = jnp.max(s3, axis=2, keepdims=True)
            if has_bias:
                cq2 = cq2_scr[g0:g0 + head_group]
                m_next = jnp.maximum(m_prev, m_cur + cq2)
                m_sub = m_next - cq2
            else:
                m_next = jnp.maximum(m_prev, m_cur)
                m_sub = m_next
            p3 = jnp.exp2(s3 - jnp.concatenate([m_sub] * (tk // LANES), axis=2)).astype(bf16)
            alpha = jnp.exp2(m_prev - m_next)
            m_scr[g0:g0 + head_group] = m_next
            v_pairs = {p: v_ref[0, pl.ds(k0, tk), p * LANES:(p + 1) * LANES].astype(bf16) for p in pairs}
            for n, h in enumerate(heads):
                v_ext = jnp.where(low_k == (h % 2 == 0), v_pairs[h // 2], 1.0)
                acc_scr[h] = acc_scr[h] * alpha[n] + jnp.dot(p3[n], v_ext, preferred_element_type=f32)

    block(0, True)

    def body(it, carry):
        block(it, table_everywhere)
        return carry

    lax.fori_loop(1, qi + 1, body, 0)
    for p in range(n_heads // 2):
        acc_e, acc_o = acc_scr[2 * p], acc_scr[2 * p + 1]
        out_e = acc_e / pltpu.roll(acc_e, HEAD_DIM, 1)
        out_o = acc_o / pltpu.roll(acc_o, HEAD_DIM, 1)
        o_ref[0, :, p * LANES:(p + 1) * LANES] = jnp.where(low_q, out_e, out_o)


def _flash_attention(q_src, k_src, v_src, q_blk, k_blk, v_blk, l2c, n_heads, *, tq, tk, bias=None,
                     table_everywhere, head_group, copy_kv=False):
    assert tq == tk
    B, T, _ = q_src.shape
    hd = n_heads * HEAD_DIM
    in_specs = [pl.BlockSpec((1, tq, hd), lambda b, i: (b, i, q_blk)),
                pl.BlockSpec((1, T, hd), lambda b, i: (b, 0, k_blk)),
                pl.BlockSpec((1, T, hd), lambda b, i: (b, 0, v_blk)),
                pl.BlockSpec(l2c.shape, lambda b, i: (0, 0, 0))]
    args = [q_src, k_src, v_src, l2c]
    if bias is not None:
        cum, cum_t = bias
        in_specs += [pl.BlockSpec((1, tq, LANES), lambda b, i: (b, i, 0)),
                     pl.BlockSpec((1, cum_t.shape[1], T), lambda b, i: (b, 0, 0))]
        args += [cum, cum_t]
    scratch = [pltpu.VMEM((n_heads, tq, LANES), f32), pltpu.VMEM((n_heads, tq, LANES), f32),
               pltpu.VMEM((n_heads, tq, LANES), bf16)]
    if bias is not None:
        scratch.append(pltpu.VMEM((n_heads, tq, LANES), f32))
    out_specs = pl.BlockSpec((1, tq, hd), lambda b, i: (b, i, 0))
    out_shape = jax.ShapeDtypeStruct((B, T, hd), f32)
    if copy_kv:
        kv_out = pl.BlockSpec((1, T, hd), lambda b, i: (b, 0, 0))
        out_specs = [out_specs, kv_out, kv_out]
        out_shape = [out_shape] * 3
    return pl.pallas_call(
        functools.partial(_flash_kernel, n_heads=n_heads, tq=tq, tk=tk, has_bias=bias is not None,
                          table_everywhere=table_everywhere, copy_kv=copy_kv, head_group=head_group),
        grid=(B, T // tq),
        in_specs=in_specs,
        out_specs=out_specs,
        out_shape=out_shape,
        scratch_shapes=scratch,
        compiler_params=_cparams(("parallel", "arbitrary")),
        name="flash_attention",
    )(*args)


def _qk(q_h, k_h):
    return lax.dot_general(q_h, k_h, (((1,), (1,)), ((), ())), preferred_element_type=f32)


FOX_PP = 32
DIL_KC = 2048


def _block_diag_queries(q2, n_heads):
    nq, hd = q2.shape
    qt = jnp.concatenate([q2] * n_heads, axis=0)
    rh = lax.broadcasted_iota(jnp.int32, (n_heads * nq, hd), 0) // nq
    ch = lax.broadcasted_iota(jnp.int32, (n_heads * nq, hd), 1) // HEAD_DIM
    return jnp.where(rh == ch, qt, 0.0).astype(bf16)


def _rows_per_head(x, nq):
    n_heads, n = x.shape
    return jnp.broadcast_to(x[:, None, :], (n_heads, nq, n)).reshape(n_heads * nq, n)


def _decode_first(s, v_new, m_scr, l_scr, accn_scr, acct_scr):
    m = jnp.max(s, axis=1, keepdims=True)
    p = jnp.exp2(s - m)
    m_scr[...] = jnp.broadcast_to(m, m_scr.shape)
    l_scr[...] = jnp.broadcast_to(jnp.sum(p, axis=1, keepdims=True), l_scr.shape)
    accn_scr[...] = jnp.dot(p.astype(bf16), v_new, preferred_element_type=f32)
    acct_scr[...] = jnp.zeros_like(acct_scr)


def _decode_update(s, vt_all, m_scr, l_scr, accn_scr, acct_scr):
    R = s.shape[0]
    hd = vt_all.shape[0]
    m_prev = m_scr[...]
    m_next = jnp.maximum(m_prev, jnp.max(s, axis=1, keepdims=True))
    p = jnp.exp2(s - m_next[:, 0:1])
    alpha = jnp.exp2(m_prev - m_next)
    l_scr[...] = alpha * l_scr[...] + jnp.sum(p, axis=1, keepdims=True)
    m_scr[...] = m_next
    accn_scr[...] = accn_scr[...] * jnp.concatenate([alpha] * (hd // LANES), axis=1)
    new_t = lax.dot_general(vt_all, p.astype(bf16), (((1,), (1,)), ((), ())), preferred_element_type=f32)
    if R < LANES:
        alpha = jnp.concatenate([alpha, jnp.zeros((LANES - R, LANES), f32)], axis=0)
    alpha_row = jnp.transpose(alpha)[0:1, 0:R]
    acct_scr[:, 0:R] = acct_scr[:, 0:R] * alpha_row + new_t


def _decode_finish(o_ref, n_heads, nq, l_scr, accn_scr, acct_scr):
    for h in range(n_heads):
        blk = acct_scr[h * HEAD_DIM:(h + 1) * HEAD_DIM, :]
        sq = jnp.transpose(jnp.concatenate([blk, jnp.zeros((LANES - HEAD_DIM, LANES), f32)], axis=0))
        rows = slice(h * nq, (h + 1) * nq)
        cols = slice(h * HEAD_DIM, (h + 1) * HEAD_DIM)
        num = sq[rows, 0:HEAD_DIM] + accn_scr[rows, cols]
        o_ref[0, :, cols] = num / l_scr[rows, 0:HEAD_DIM]


def _fox_decode2_kernel(pt_ref, q_ref, kn_ref, vn_ref, cqr_ref, cnt_ref, *rest):
    lf_refs = rest[:FOX_PP]
    k_refs = rest[FOX_PP:2 * FOX_PP]
    v_refs = rest[2 * FOX_PP:3 * FOX_PP]
    o_ref, m_scr, l_scr, accn_scr, acct_scr, tot_scr, qbd_scr = rest[3 * FOX_PP:]
    j = pl.program_id(1)
    nq = q_ref.shape[1]
    R = H_B * nq
    cq2 = cqr_ref[0] * LOG2E

    @pl.when(j == 0)
    def _():
        qbd = _block_diag_queries(q_ref[0] * (ATTN_SCALE * LOG2E), H_B)
        qbd_scr[...] = qbd
        tot_scr[...] = jnp.zeros_like(tot_scr)
        s = _qk(qbd, kn_ref[0].astype(bf16))
        s = s + cq2[:, 0:nq] - _rows_per_head(cnt_ref[0][:, 0:nq] * LOG2E, nq)
        qi = lax.broadcasted_iota(jnp.int32, (R, nq), 0) % nq
        kj = lax.broadcasted_iota(jnp.int32, (R, nq), 1)
        s = jnp.where(kj <= qi, s, -jnp.inf)
        _decode_first(s, vn_ref[0].astype(bf16), m_scr, l_scr, accn_scr, acct_scr)

    lane = lax.broadcasted_iota(jnp.int32, (H_B, PAGE_SIZE), 1)
    qbd = qbd_scr[...]
    tot = tot_scr[...]
    s_parts = []
    v_parts = []
    for i in range(FOX_PP):
        x = lf_refs[i][0]
        sft = 1
        while sft < PAGE_SIZE:
            x = x + jnp.where(lane >= sft, pltpu.roll(x, sft, 1), 0.0)
            sft *= 2
        page_tot = x[:, PAGE_SIZE - 1:PAGE_SIZE]
        bias2 = _rows_per_head((page_tot - x + tot) * LOG2E, nq) + cq2
        tot = tot + page_tot
        kt_all = k_refs[i][0].reshape(D_B, PAGE_SIZE).astype(bf16)
        s_parts.append(jnp.dot(qbd, kt_all, preferred_element_type=f32) + bias2)
        v_parts.append(v_refs[i][0].reshape(D_B, PAGE_SIZE).astype(bf16))
    tot_scr[...] = tot
    _decode_update(jnp.concatenate(s_parts, axis=1), jnp.concatenate(v_parts, axis=1),
                   m_scr, l_scr, accn_scr, acct_scr)

    @pl.when(j == pl.num_programs(1) - 1)
    def _():
        _decode_finish(o_ref, H_B, nq, l_scr, accn_scr, acct_scr)


def _fox_decode2(page_table, proj, cum_rows, cum_new_t, lf_cache_t, k_cache, v_cache):
    NB, nq, _ = proj.shape
    n_pages = page_table.shape[1]
    steps = n_pages // FOX_PP
    R = H_B * nq

    def lf_spec(i):
        return pl.BlockSpec((1, H_B, PAGE_SIZE),
                            lambda b, j, pt: (pt[b, n_pages - 1 - (j * FOX_PP + i)], 0, 0))

    def kv_spec(i):
        return pl.BlockSpec((1, H_B, HEAD_DIM, PAGE_SIZE),
                            lambda b, j, pt: (pt[b, n_pages - 1 - (j * FOX_PP + i)], 0, 0, 0))

    in_specs = [pl.BlockSpec((1, nq, D_B), lambda b, j, pt: (b, 0, EV_Q_OFF // D_B)),
                pl.BlockSpec((1, nq, D_B), lambda b, j, pt: (b, 0, EV_K_OFF // D_B)),
                pl.BlockSpec((1, nq, D_B), lambda b, j, pt: (b, 0, EV_V_OFF // D_B)),
                pl.BlockSpec((1, R, LANES), lambda b, j, pt: (b, 0, 0)),
                pl.BlockSpec((1, SUBLANES, LANES), lambda b, j, pt: (b, 0, 0))]
    args = [proj, proj, proj, cum_rows, cum_new_t]
    for i in range(FOX_PP):
        in_specs.append(lf_spec(i))
        args.append(lf_cache_t)
    for cache in (k_cache, v_cache):
        for i in range(FOX_PP):
            in_specs.append(kv_spec(i))
            args.append(cache)
    grid_spec = pltpu.PrefetchScalarGridSpec(
        num_scalar_prefetch=1,
        grid=(NB, steps),
        in_specs=in_specs,
        out_specs=pl.BlockSpec((1, nq, D_B), lambda b, j, pt: (b, 0, 0)),
        scratch_shapes=[pltpu.VMEM((R, LANES), f32), pltpu.VMEM((R, LANES), f32),
                        pltpu.VMEM((R, D_B), f32), pltpu.VMEM((D_B, LANES), f32),
                        pltpu.VMEM((H_B, LANES), f32), pltpu.VMEM((R, D_B), bf16)],
    )
    return pl.pallas_call(
        _fox_decode2_kernel,
        grid_spec=grid_spec,
        out_shape=jax.ShapeDtypeStruct((NB, nq, D_B), f32),
        compiler_params=_cparams(("parallel", "arbitrary")),
        name="fox_decode",
    )(page_table, *args)


def _dil_decode2_kernel(q_ref, kn_ref, vn_ref, l2new_ref, l2past_ref, k_ref, v_ref, o_ref,
                        m_scr, l_scr, accn_scr, acct_scr, qbd_scr):
    j = pl.program_id(1)
    nq = q_ref.shape[1]

    @pl.when(j == 0)
    def _():
        qbd = _block_diag_queries(q_ref[0] * (ATTN_SCALE * LOG2E), H_C)
        qbd_scr[...] = qbd
        s = _qk(qbd, kn_ref[0].astype(bf16)) + jnp.concatenate([l2new_ref[:, 0:nq]] * H_C, axis=0)
        _decode_first(s, vn_ref[0].astype(bf16), m_scr, l_scr, accn_scr, acct_scr)

    kt_all = k_ref[0].reshape(D_C, DIL_KC).astype(bf16)
    s = jnp.dot(qbd_scr[...], kt_all, preferred_element_type=f32)
    s = s + jnp.concatenate([l2past_ref[...]] * H_C, axis=0)
    _decode_update(s, v_ref[0].reshape(D_C, DIL_KC).astype(bf16), m_scr, l_scr, accn_scr, acct_scr)

    @pl.when(j == pl.num_programs(1) - 1)
    def _():
        _decode_finish(o_ref, H_C, nq, l_scr, accn_scr, acct_scr)


def _dil_decode2(qkv, l2_new, l2_past, k_state, v_state):
    NB, nq, _ = qkv.shape
    L = k_state.shape[3]
    R = H_C * nq
    kv_spec = pl.BlockSpec((1, H_C, HEAD_DIM, DIL_KC), lambda b, j: (b, 0, 0, j))
    return pl.pallas_call(
        _dil_decode2_kernel,
        grid=(NB, L // DIL_KC),
        in_specs=[pl.BlockSpec((1, nq, D_C), lambda b, j: (b, 0, 0)),
                  pl.BlockSpec((1, nq, D_C), lambda b, j: (b, 0, 1)),
                  pl.BlockSpec((1, nq, D_C), lambda b, j: (b, 0, 2)),
                  pl.BlockSpec(l2_new.shape, lambda b, j: (0, 0)),
                  pl.BlockSpec((nq, DIL_KC), lambda b, j: (0, j)),
                  kv_spec, kv_spec],
        out_specs=pl.BlockSpec((1, nq, D_C), lambda b, j: (b, 0, 0)),
        out_shape=jax.ShapeDtypeStruct((NB, nq, D_C), f32),
        scratch_shapes=[pltpu.VMEM((R, LANES), f32), pltpu.VMEM((R, LANES), f32),
                        pltpu.VMEM((R, D_C), f32), pltpu.VMEM((D_C, LANES), f32),
                        pltpu.VMEM((R, D_C), bf16)],
        compiler_params=_cparams(("parallel", "arbitrary")),
        name="dil_decode",
    )(qkv, qkv, qkv, l2_new, l2_past, k_state, v_state)


def _dil_multiplicity(delta):
    delta = np.asarray(delta)
    c = np.zeros(delta.shape, np.float32)
    for w, d in DIL_PAIRS:
        c += ((delta >= 0) & (delta % d == 0) & (delta <= w)).astype(np.float32)
    return c


def _log2_block_tables(n_blocks, t, mult_fn):
    i = np.arange(t)[:, None]
    j = np.arange(t)[None, :]
    with np.errstate(divide="ignore"):
        tabs = [np.log2(mult_fn(i - j + d * t)).astype(np.float32) for d in range(n_blocks)]
    return jnp.asarray(np.stack(tabs))


def _rope_tables(pos):
    half = HEAD_DIM // 2
    inv = ROPE_THETA ** (-jnp.arange(half, dtype=f32) / half)
    ang = pos.astype(f32)[:, None] * inv[None, :]
    cos = jnp.cos(ang)
    sin = jnp.sin(ang)
    return jnp.tile(cos, (1, 4)), jnp.concatenate([-sin, sin, -sin, sin], axis=1)


def _mods(mod_lj, n_prompt, per_row_repeat):
    mp = mod_lj[:n_prompt][:, None, :]
    ms = jnp.repeat(mod_lj[n_prompt:], per_row_repeat, axis=0)[None]
    sp = [mp[..., i * D_MODEL:(i + 1) * D_MODEL] for i in range(3)]
    ss = [ms[..., i * D_MODEL:(i + 1) * D_MODEL] for i in range(3)]
    return sp, ss


def kernel(x_prompt, x_sample, cache_fox_k, cache_fox_v, cache_fox_logf, state_rwkv_wkv, state_rwkv_shift,
           state_dil_k, state_dil_v, page_table, c_prompt, c_sample, ada_w, ada_b, norm_pre, norm_post,
           mlp_up, mlp_down, ev_w_in, ev_w_out, rw_mu, rw_w0, rw_w2, rw_a0, rw_a2, rw_g2, rw_k_k, rw_k_a,
           rw_r_k, rw_lnx_w, rw_lnx_b, fox_b_f, od_w_in, od_w_out):
    BP, T, D = x_prompt.shape
    BS, TS, _ = x_sample.shape
    depth = ada_w.shape[0]
    past_len = page_table.shape[1] * PAGE_SIZE

    mod = _ada_mod(jnp.concatenate([c_prompt, c_sample], axis=0), ada_w, ada_b)

    idx = np.arange(LANES)
    idx2 = np.arange(2 * LANES)
    bd256 = jnp.asarray((idx2[:, None] // HEAD_DIM == idx2[None, :] // HEAD_DIM).astype(np.float32)).astype(bf16)
    idx5 = np.arange(D_A)
    bd512 = jnp.asarray((idx5[:, None] // HEAD_DIM == idx5[None, :] // HEAD_DIM).astype(np.float32)).astype(bf16)
    e4 = jnp.asarray(np.tile((idx[None, :] % HEAD_DIM == np.arange(HEAD_DIM)[:, None]).astype(np.float32),
                             (PAIRS, 1)))
    TQ = 256
    causal_tab = _log2_block_tables(1, TQ, lambda dl: (dl >= 0).astype(np.float32))
    dil_tab = _log2_block_tables(T // TQ, TQ, _dil_multiplicity)

    xp = x_prompt
    xs = x_sample.reshape(1, BS * TS, D)
    outs = {}
    for l in range(depth):
        i = l // 2
        (sh_p, sc_p, gt_p), (sh_s, sc_s, gt_s) = _mods(mod[2 * l], BP, TS)
        g_pre = norm_pre[l, 0][None]
        g_post = norm_post[l, 0][None]
        if l % 2 == 0:
            w_in = ev_w_in[i]
            zeros = lambda n: jnp.zeros((D, n), f32)
            w_pad = jnp.concatenate([w_in[:, :RW_COLS], zeros(EV_Q_OFF - RW_COLS), w_in[:, RW_COLS:],
                                     zeros(EV_COLS_PAD - EV_F_OFF - H_B)], axis=1).astype(bf16)
            w_out = ev_w_out[i].astype(bf16)
            b_f_pad = jnp.pad(fox_b_f[i], (0, LANES - H_B))[None]
            rw_args = (rw_mu[i][None], rw_w0[i][None], rw_w2[i].astype(bf16), rw_a0[i][None],
                       rw_a2[i].astype(bf16), rw_g2[i].astype(bf16), rw_k_k[i][None], rw_k_a[i][None],
                       rw_r_k[i].reshape(1, D_A), bd512)

            proj_p = _norm_matmul(xp, sh_p, sc_p, g_pre, w_pad, tm=2048, tn=768)
            r, w, k2, v, kk, kka, g, bonus = _rwkv_pre(proj_p, jnp.zeros((BP, 1, RW_COLS), f32), *rw_args, tt=256)
            y, st = _wkv_scan(r, w, k2, v, kk, kka, jnp.zeros((BP, H_A, HEAD_DIM, HEAD_DIM), f32), bd256, e4,
                              chunk=64)
            ya = _rwkv_post(y, g, bonus, rw_lnx_w[i][None], rw_lnx_b[i][None], bd512, tt=512)
            lf_p, cum_p = _fox_gate(proj_p, b_f_pad)
            cum_t = jnp.transpose(cum_p[:, :, :SUBLANES], (0, 2, 1))
            yb, fk, fv = _flash_attention(proj_p, proj_p, proj_p, EV_Q_OFF // D_B, EV_K_OFF // D_B,
                                          EV_V_OFF // D_B, causal_tab, H_B, tq=TQ, tk=TQ, bias=(cum_p, cum_t),
                                          table_everywhere=False, head_group=8, copy_kv=True)
            xp = _matmul_post([ya, yb], [w_out[:D_A], w_out[D_A:]], xp, gt_p, g_post, tm=512)
            outs.setdefault("fk_p", []).append(fk.reshape(BP, T, H_B, HEAD_DIM))
            outs.setdefault("fv_p", []).append(fv.reshape(BP, T, H_B, HEAD_DIM))
            outs.setdefault("fl_p", []).append(lf_p[:, :, :H_B])
            outs.setdefault("wkv_p", []).append(st)
            outs.setdefault("sh_p", []).append(proj_p[:, T - 1, :RW_COLS])

            proj_s = _norm_matmul(xs, sh_s, sc_s, g_pre, w_pad, tm=BS * TS, tn=1280).reshape(BS, TS, EV_COLS_PAD)
            r, w, k2, v, kk, kka, g, bonus = _rwkv_pre(proj_s, state_rwkv_shift[i][:, None, :], *rw_args, tt=TS,
                                                       nb=BS)
            y, st = _wkv_scan(r, w, k2, v, kk, kka, state_rwkv_wkv[i], bd256, e4, chunk=TS)
            flat = lambda t: t.reshape(1, BS * TS, D_A)
            ya = _rwkv_post(flat(y), flat(g), flat(bonus), rw_lnx_w[i][None], rw_lnx_b[i][None], bd512,
                            tt=BS * TS)
            lf_s, cum_s = _fox_gate(proj_s, b_f_pad, nb=BS)
            cum_s_t = jnp.pad(jnp.transpose(cum_s[:, :, :SUBLANES], (0, 2, 1)), ((0, 0), (0, 0), (0, LANES - TS)))
            lf_cache_t = jnp.transpose(cache_fox_logf[i], (0, 2, 1))
            cum_rows = jnp.broadcast_to(
                jnp.transpose(cum_s[:, :, :H_B], (0, 2, 1)).reshape(BS, H_B * TS, 1), (BS, H_B * TS, LANES))
            yb = _fox_decode2(page_table, proj_s, cum_rows, cum_s_t, lf_cache_t,
                              jnp.transpose(cache_fox_k[i], (0, 2, 3, 1)),
                              jnp.transpose(cache_fox_v[i], (0, 2, 3, 1)))
            xs = _matmul_post([ya.reshape(1, BS * TS, D_A), yb.reshape(1, BS * TS, D_B)],
                              [w_out[:D_A], w_out[D_A:]], xs, gt_s, g_post, tm=BS * TS)
            outs.setdefault("fk_s", []).append(proj_s[:, :, EV_K_OFF:EV_K_OFF + D_B].reshape(BS, TS, H_B, HEAD_DIM))
            outs.setdefault("fv_s", []).append(proj_s[:, :, EV_V_OFF:EV_V_OFF + D_B].reshape(BS, TS, H_B, HEAD_DIM))
            outs.setdefault("fl_s", []).append(lf_s[:, :, :H_B])
            outs.setdefault("wkv_s", []).append(st)
            outs.setdefault("sh_s", []).append(proj_s[:, TS - 1, :RW_COLS])
        else:
            w_in = od_w_in[i].astype(bf16)
            w_out = od_w_out[i].astype(bf16)
            rope_p = _rope_tables(jnp.arange(T))
            cos_s, sin_s = _rope_tables(past_len + jnp.arange(TS))
            rope_s = (jnp.tile(cos_s, (BS, 1)), jnp.tile(sin_s, (BS, 1)))

            q_p, k_p, v_p = _norm_matmul(xp, sh_p, sc_p, g_pre, w_in, tm=1024, tn=D_C, rope=rope_p,
                                         n_rope_cols=2 * D_C, split=True)
            o_p = _flash_attention(q_p, k_p, v_p, 0, 0, 0, dil_tab, H_C, tq=TQ, tk=TQ, table_everywhere=True,
                                   head_group=4)
            xp = _matmul_post([o_p], [w_out], xp, gt_p, g_post, tm=512)
            keep = min(DIL_PAIRS[-1][0], T)
            outs.setdefault("dk_p", []).append(k_p[:, T - keep:].reshape(BP, keep, H_C, HEAD_DIM))
            outs.setdefault("dv_p", []).append(v_p[:, T - keep:].reshape(BP, keep, H_C, HEAD_DIM))

            qkv_s = _norm_matmul(xs, sh_s, sc_s, g_pre, w_in, tm=BS * TS, tn=D_C, rope=rope_s,
                                 n_rope_cols=2 * D_C).reshape(BS, TS, 3 * D_C)
            L = state_dil_k.shape[2]
            tq_idx = np.arange(TS)[:, None]
            c_new = np.zeros((TS, LANES), np.float32)
            c_new[:, :TS] = _dil_multiplicity(tq_idx - np.arange(TS)[None, :])
            c_past = _dil_multiplicity(L + tq_idx - np.arange(L)[None, :])
            with np.errstate(divide="ignore"):
                l2_new, l2_past = jnp.asarray(np.log2(c_new)), jnp.asarray(np.log2(c_past))
            o_s = _dil_decode2(qkv_s, l2_new, l2_past,
                               jnp.transpose(state_dil_k[i], (0, 2, 3, 1)),
                               jnp.transpose(state_dil_v[i], (0, 2, 3, 1)))
            xs = _matmul_post([o_s.reshape(1, BS * TS, D_C)], [w_out], xs, gt_s, g_post, tm=BS * TS)
            outs.setdefault("dk_s", []).append(qkv_s[:, :, D_C:2 * D_C].reshape(BS, TS, H_C, HEAD_DIM))
            outs.setdefault("dv_s", []).append(qkv_s[:, :, 2 * D_C:].reshape(BS, TS, H_C, HEAD_DIM))

        (sh_p, sc_p, gt_p), (sh_s, sc_s, gt_s) = _mods(mod[2 * l + 1], BP, TS)
        g_pre = norm_pre[l, 1][None]
        g_post = norm_post[l, 1][None]
        w_up = mlp_up[l]
        w_down = mlp_down[l]
        xp = _mlp(xp, sh_p, sc_p, gt_p, g_pre, g_post, w_up, w_down, tm=1024, tf=1024)
        xs = _mlp(xs, sh_s, sc_s, gt_s, g_pre, g_post, w_up, w_down, tm=BS * TS, tf=1024)

    st = lambda name: jnp.stack(outs[name])
    return (xp, xs.reshape(BS, TS, D), st("fk_p"), st("fv_p"), st("fl_p"), st("fk_s"), st("fv_s"), st("fl_s"),
            st("wkv_p"), st("wkv_s"), st("sh_p"), st("sh_s"), st("dk_p"), st("dv_p"), st("dk_s"), st("dv_s"))
```

```python
import functools

import numpy as np
import jax
import jax.numpy as jnp
from jax import lax
from jax.experimental import pallas as pl
from jax.experimental.pallas import tpu as pltpu

f32 = jnp.float32
bf16 = jnp.bfloat16

D_MODEL = 1024
HEAD_DIM = 64
H_A = 8
H_B = 8
H_C = 16
D_A = H_A * HEAD_DIM
D_B = H_B * HEAD_DIM
D_C = H_C * HEAD_DIM
LORA_W = 64
LORA_A = 64
LORA_G = 128
RW_COLS = 3 * D_A + LORA_W + LORA_A + LORA_G
D_FF = 4 * D_MODEL
PAGE_SIZE = 128
DIL_PAIRS = ((128, 1), (512, 4), (2048, 16))
ROPE_THETA = 10000.0
NORM_EPS = 1e-6
LNX_EPS = 64e-5
ATTN_SCALE = HEAD_DIM ** -0.5
LOG2E = 1.4426950408889634

LANES = 128
SUBLANES = 8
VMEM_LIMIT = 56 * 1024 * 1024

EV_Q_OFF = 2048
EV_K_OFF = 2560
EV_V_OFF = 3072
EV_F_OFF = 3584
EV_COLS_PAD = 3840


def _cparams(sem):
    return pltpu.CompilerParams(dimension_semantics=sem, vmem_limit_bytes=VMEM_LIMIT)


def _split_hi_lo(x):
    hi = x.astype(bf16)
    lo = (x - hi.astype(f32)).astype(bf16)
    return hi, lo


def _seg_sum(x, bd):
    hi, lo = _split_hi_lo(x)
    return (jnp.dot(hi, bd, preferred_element_type=f32) + jnp.dot(lo, bd, preferred_element_type=f32))


def _seg_sum_paired(x, bd2):
    n = x.shape[0] // 2
    xb = x.astype(bf16)
    out = jnp.dot(jnp.concatenate([xb[:n], xb[n:]], axis=1), bd2, preferred_element_type=f32)
    return jnp.concatenate([out[:, :LANES], out[:, LANES:]], axis=0)


def _rmsnorm_rows(x, g):
    ms = jnp.mean(x * x, axis=-1, keepdims=True)
    return x * lax.rsqrt(ms + NORM_EPS) * g


def _softplus(z):
    return jnp.maximum(z, 0.0) + jnp.log(1.0 + jnp.exp(-jnp.abs(z)))


def _ada_kernel(c_ref, w_ref, b_ref, o_ref):
    c = c_ref[...]
    s = (c * jax.nn.sigmoid(c)).astype(bf16)
    o_ref[0] = jnp.dot(s, w_ref[0].astype(bf16), preferred_element_type=f32) + b_ref[0]


def _ada_mod(c_all, ada_w, ada_b):
    n = c_all.shape[0]
    nl = ada_w.shape[0] * ada_w.shape[1]
    w = ada_w.reshape(nl, D_MODEL, 3 * D_MODEL)
    b = ada_b.reshape(nl, 1, 3 * D_MODEL)
    tn = 1024
    return pl.pallas_call(
        _ada_kernel,
        grid=(nl, 3 * D_MODEL // tn),
        in_specs=[pl.BlockSpec((n, D_MODEL), lambda l, j: (0, 0)),
                  pl.BlockSpec((1, D_MODEL, tn), lambda l, j: (l, 0, j)),
                  pl.BlockSpec((1, 1, tn), lambda l, j: (l, 0, j))],
        out_specs=pl.BlockSpec((1, n, tn), lambda l, j: (l, 0, j)),
        out_shape=jax.ShapeDtypeStruct((nl, n, 3 * D_MODEL), f32),
        compiler_params=_cparams(("parallel", "parallel")),
        name="ada_mod",
    )(c_all, w, b)


def _rope_tile(x, cos, sin_signed):
    lane = lax.broadcasted_iota(jnp.int32, (x.shape[0], LANES), 1)
    first = (lane % HEAD_DIM) < (HEAD_DIM // 2)
    outs = []
    for gidx in range(x.shape[1] // LANES):
        xg = x[:, gidx * LANES:(gidx + 1) * LANES]
        partner = jnp.where(first, pltpu.roll(xg, LANES - HEAD_DIM // 2, 1), pltpu.roll(xg, HEAD_DIM // 2, 1))
        outs.append(xg * cos + partner * sin_signed)
    return jnp.concatenate(outs, axis=1) if len(outs) > 1 else outs[0]


def _norm_mm_kernel(x_ref, sh_ref, sc_ref, g_ref, w_ref, *rest, n_rope_tiles, n_split):
    if n_rope_tiles:
        cos_ref, sin_ref = rest[:2]
        rest = rest[2:]
    o_refs = rest[:-1]
    h_scr = rest[-1]
    j = pl.program_id(2)

    @pl.when(j == 0)
    def _():
        h = _rmsnorm_rows(x_ref[0], g_ref[...]) * (1.0 + sc_ref[0]) + sh_ref[0]
        h_scr[...] = h.astype(bf16)

    acc = jnp.dot(h_scr[...], w_ref[...].astype(bf16), preferred_element_type=f32)
    if n_split:
        for jj in range(n_split):
            @pl.when(j == jj)
            def _(jj=jj):
                o_refs[jj][0] = _rope_tile(acc, cos_ref[...], sin_ref[...]) if jj < n_rope_tiles else acc
    elif n_rope_tiles:
        @pl.when(j < n_rope_tiles)
        def _():
            o_refs[0][0] = _rope_tile(acc, cos_ref[...], sin_ref[...])

        @pl.when(j >= n_rope_tiles)
        def _():
            o_refs[0][0] = acc
    else:
        o_refs[0][0] = acc


def _norm_matmul(x, shift, scale, g, w, *, tm, tn, rope=None, n_rope_cols=0, split=False):
    B, T, D = x.shape
    N = w.shape[1]
    per_row = shift.shape[1] != 1
    mod_spec = (pl.BlockSpec((1, tm, D), lambda b, i, j: (b, i, 0)) if per_row
                else pl.BlockSpec((1, 1, D), lambda b, i, j: (b, 0, 0)))
    in_specs = [pl.BlockSpec((1, tm, D), lambda b, i, j: (b, i, 0)), mod_spec, mod_spec,
                pl.BlockSpec((1, D), lambda b, i, j: (0, 0)),
                pl.BlockSpec((D, tn), lambda b, i, j: (0, j))]
    args = [x, shift, scale, g, w]
    n_rope_tiles = 0
    if rope is not None:
        cos, sin = rope
        n_rope_tiles = n_rope_cols // tn
        if cos.shape[0] == T:
            rspec = pl.BlockSpec((tm, LANES), lambda b, i, j: (i, 0))
        else:
            nt = T // tm
            rspec = pl.BlockSpec((tm, LANES), lambda b, i, j: (b * nt + i, 0))
        in_specs += [rspec, rspec]
        args += [cos, sin]
    n_split = N // tn if split else 0
    if split:
        out_specs = [pl.BlockSpec((1, tm, tn), lambda b, i, j: (b, i, 0))] * n_split
        out_shape = [jax.ShapeDtypeStruct((B, T, tn), f32)] * n_split
    else:
        out_specs = pl.BlockSpec((1, tm, tn), lambda b, i, j: (b, i, j))
        out_shape = jax.ShapeDtypeStruct((B, T, N), f32)
    return pl.pallas_call(
        functools.partial(_norm_mm_kernel, n_rope_tiles=n_rope_tiles, n_split=n_split),
        grid=(B, T // tm, N // tn),
        in_specs=in_specs,
        out_specs=out_specs,
        out_shape=out_shape,
        scratch_shapes=[pltpu.VMEM((tm, D), bf16)],
        compiler_params=_cparams(("parallel", "parallel", "arbitrary")),
        name="norm_matmul",
    )(*args)


def _mm_post_kernel(*refs, n_in):
    a_refs = refs[:n_in]
    w_refs = refs[n_in:2 * n_in]
    x_ref, gate_ref, g_ref, o_ref = refs[2 * n_in:]
    y = None
    for a_ref, w_ref in zip(a_refs, w_refs):
        part = jnp.dot(a_ref[0].astype(bf16), w_ref[...], preferred_element_type=f32)
        y = part if y is None else y + part
    o_ref[0] = x_ref[0] + gate_ref[0] * _rmsnorm_rows(y, g_ref[...])


def _matmul_post(a_list, w_list, x, gate, g, *, tm):
    B, T, D = x.shape
    per_row = gate.shape[1] != 1
    gate_spec = (pl.BlockSpec((1, tm, D), lambda b, i: (b, i, 0)) if per_row
                 else pl.BlockSpec((1, 1, D), lambda b, i: (b, 0, 0)))
    in_specs = []
    for a in a_list:
        ka = a.shape[2]
        in_specs.append(pl.BlockSpec((1, tm, ka), lambda b, i: (b, i, 0)))
    for w in w_list:
        in_specs.append(pl.BlockSpec(w.shape, lambda b, i: (0, 0)))
    in_specs += [pl.BlockSpec((1, tm, D), lambda b, i: (b, i, 0)), gate_spec,
                 pl.BlockSpec((1, D), lambda b, i: (0, 0))]
    return pl.pallas_call(
        functools.partial(_mm_post_kernel, n_in=len(a_list)),
        grid=(B, T // tm),
        in_specs=in_specs,
        out_specs=pl.BlockSpec((1, tm, D), lambda b, i: (b, i, 0)),
        out_shape=jax.ShapeDtypeStruct((B, T, D), f32),
        compiler_params=_cparams(("parallel", "parallel")),
        name="matmul_post",
    )(*a_list, *w_list, x, gate, g)


def _mlp_kernel(x_ref, sh_ref, sc_ref, gate_ref, gpre_ref, gpost_ref, wu_ref, wd_ref, o_ref, h_scr, acc_scr):
    k = pl.program_id(2)

    @pl.when(k == 0)
    def _():
        h = _rmsnorm_rows(x_ref[0], gpre_ref[...]) * (1.0 + sc_ref[0]) + sh_ref[0]
        h_scr[...] = h.astype(bf16)
        acc_scr[...] = jnp.zeros_like(acc_scr)

    u = jnp.dot(h_scr[...], wu_ref[...].astype(bf16), preferred_element_type=f32)
    a = jnp.square(jnp.maximum(u, 0.0)).astype(bf16)
    acc_scr[...] += jnp.dot(a, wd_ref[...].astype(bf16), preferred_element_type=f32)

    @pl.when(k == pl.num_programs(2) - 1)
    def _():
        o_ref[0] = x_ref[0] + gate_ref[0] * _rmsnorm_rows(acc_scr[...], gpost_ref[...])


def _mlp(x, shift, scale, gate, g_pre, g_post, w_up, w_down, *, tm, tf):
    B, T, D = x.shape
    per_row = shift.shape[1] != 1
    mod_spec = (pl.BlockSpec((1, tm, D), lambda b, i, k: (b, i, 0)) if per_row
                else pl.BlockSpec((1, 1, D), lambda b, i, k: (b, 0, 0)))
    vec_spec = pl.BlockSpec((1, D), lambda b, i, k: (0, 0))
    return pl.pallas_call(
        _mlp_kernel,
        grid=(B, T // tm, D_FF // tf),
        in_specs=[pl.BlockSpec((1, tm, D), lambda b, i, k: (b, i, 0)), mod_spec, mod_spec, mod_spec,
                  vec_spec, vec_spec,
                  pl.BlockSpec((D, tf), lambda b, i, k: (0, k)),
                  pl.BlockSpec((tf, D), lambda b, i, k: (k, 0))],
        out_specs=pl.BlockSpec((1, tm, D), lambda b, i, k: (b, i, 0)),
        out_shape=jax.ShapeDtypeStruct((B, T, D), f32),
        scratch_shapes=[pltpu.VMEM((tm, D), bf16), pltpu.VMEM((tm, D), f32)],
        compiler_params=_cparams(("parallel", "parallel", "arbitrary")),
        name="mlp",
    )(x, shift, scale, gate, g_pre, g_post, w_up, w_down)


def _rwkv_pre_kernel(p_ref, sh0_ref, mu_ref, w0_ref, w2_ref, a0_ref, a2_ref, g2_ref, kk_w_ref, ka_ref,
                     rk_ref, bd_ref, r_o, w_o, k_o, v_o, kk_o, kka_o, g_o, bon_o, carry):
    ti = pl.program_id(1)

    @pl.when(ti == 0)
    def _():
        carry[...] = sh0_ref[...]

    nb, tt, _ = p_ref.shape
    p3 = p_ref[...]
    p = p3.reshape(nb * tt, RW_COLS)
    first = lax.broadcasted_iota(jnp.int32, (nb * tt, 1), 0) % tt == 0
    last_prev = jnp.broadcast_to(carry[...], (nb, tt, RW_COLS)).reshape(nb * tt, RW_COLS)
    prev = jnp.where(first, last_prev, pltpu.roll(p, 1, 0))
    carry[...] = p3[:, tt - 1:tt, :]
    ps = p + mu_ref[...] * (prev - p)
    r = ps[:, 0:D_A]
    k = ps[:, D_A:2 * D_A]
    v = ps[:, 2 * D_A:3 * D_A]
    o = 3 * D_A
    dw = ps[:, o:o + LORA_W]
    da = ps[:, o + LORA_W:o + LORA_W + LORA_A]
    dg = ps[:, o + LORA_W + LORA_A:o + LORA_W + LORA_A + LORA_G]
    bd = bd_ref[...]
    wl = w0_ref[...] + jnp.dot(jnp.tanh(dw).astype(bf16), w2_ref[...], preferred_element_type=f32)
    w_log = -_softplus(-wl) - 0.5
    decay = jnp.exp(-jnp.exp(w_log))
    a = jax.nn.sigmoid(a0_ref[...] + jnp.dot(da.astype(bf16), a2_ref[...], preferred_element_type=f32))
    g = jnp.dot(jax.nn.sigmoid(dg).astype(bf16), g2_ref[...], preferred_element_type=f32)
    kk = k * kk_w_ref[...]
    nrm = jnp.sqrt(_seg_sum(kk * kk, bd))
    kk = kk / jnp.maximum(nrm, 1e-12)
    k2 = k * (1.0 + (a - 1.0) * ka_ref[...])
    bonus = _seg_sum(r * k2 * rk_ref[...], bd) * v
    for o_ref, val in ((r_o, r), (w_o, decay), (k_o, k2), (v_o, v), (kk_o, kk), (kka_o, kk * a), (g_o, g),
                       (bon_o, bonus)):
        o_ref[...] = val.reshape(nb, tt, D_A)


def _rwkv_pre(proj, shift0, mu, w0, w2, a0, a2, g2, k_k, k_a, r_k, bd512, *, tt, nb=1):
    B, T, _ = proj.shape
    vec = lambda n: pl.BlockSpec((1, n), lambda b, i: (0, 0))
    mat = lambda s: pl.BlockSpec(s, lambda b, i: (0, 0))
    out_spec = pl.BlockSpec((nb, tt, D_A), lambda b, i: (b, i, 0))
    out_sds = jax.ShapeDtypeStruct((B, T, D_A), f32)
    return pl.pallas_call(
        _rwkv_pre_kernel,
        grid=(B // nb, T // tt),
        in_specs=[pl.BlockSpec((nb, tt, RW_COLS), lambda b, i: (b, i, 0)),
                  pl.BlockSpec((nb, 1, RW_COLS), lambda b, i: (b, 0, 0)),
                  vec(RW_COLS), vec(D_A), mat((LORA_W, D_A)), vec(D_A), mat((LORA_A, D_A)),
                  mat((LORA_G, D_A)), vec(D_A), vec(D_A), vec(D_A), mat((D_A, D_A))],
        out_specs=[out_spec] * 8,
        out_shape=[out_sds] * 8,
        scratch_shapes=[pltpu.VMEM((nb, 1, RW_COLS), f32)],
        compiler_params=_cparams(("parallel", "arbitrary")),
        name="rwkv_pre",
    )(proj, shift0, mu, w0, w2, a0, a2, g2, k_k, k_a, r_k, bd512)


SCAN_NB = 8
PAIRS = H_A // 2


def _wkv_scan_kernel(r_ref, w_ref, k_ref, v_ref, kk_ref, kka_ref, s0_ref, bd_ref, e_ref,
                     y_ref, st_ref, s_scr):
    c = pl.program_id(1)
    n_steps = r_ref.shape[1]
    rows_per_b = PAIRS * HEAD_DIM

    @pl.when(c == 0)
    def _():
        for b in range(SCAN_NB):
            for p in range(PAIRS):
                r0 = (b * PAIRS + p) * HEAD_DIM
                s_scr[r0:r0 + HEAD_DIM, :] = jnp.concatenate([s0_ref[b, 2 * p], s0_ref[b, 2 * p + 1]], axis=1)

    bd = bd_ref[...]
    e4 = e_ref[...]

    def step8(t8, carry):
        t0 = pl.multiple_of(t8 * SUBLANES, SUBLANES)
        names = (("r", r_ref), ("w", w_ref), ("k", k_ref), ("v", v_ref), ("kk", kk_ref), ("kka", kka_ref))
        blks = [{name: ref[b, pl.ds(t0, SUBLANES), :] for name, ref in names} for b in range(SCAN_NB)]
        yrows = [[] for _ in range(SCAN_NB)]
        e_all = jnp.concatenate([e4] * SCAN_NB, axis=0)
        for j in range(SUBLANES):
            def rows(name):
                parts = [jnp.broadcast_to(blks[b][name][j:j + 1, LANES * p:LANES * (p + 1)], (HEAD_DIM, LANES))
                         for b in range(SCAN_NB) for p in range(PAIRS)]
                return jnp.concatenate(parts, axis=0)

            s = s_scr[...]
            sk = _seg_sum_paired(s * rows("kk"), bd)
            vcol = _seg_sum_paired(e_all * rows("v"), bd)
            s = s * rows("w") - sk * rows("kka") + vcol * rows("k")
            s_scr[...] = s
            yrep = _seg_sum_paired(s * rows("r"), bd)
            yrow = jnp.sum((e_all * yrep).reshape(SCAN_NB * PAIRS, HEAD_DIM, LANES), axis=1)
            for b in range(SCAN_NB):
                yrows[b].append(jnp.concatenate([yrow[b * PAIRS + p:b * PAIRS + p + 1, :] for p in range(PAIRS)],
                                                axis=1))
        for b in range(SCAN_NB):
            y_ref[b, pl.ds(t0, SUBLANES), :] = jnp.concatenate(yrows[b], axis=0)
        return carry

    lax.fori_loop(0, n_steps // SUBLANES, step8, 0)

    @pl.when(c == pl.num_programs(1) - 1)
    def _():
        for b in range(SCAN_NB):
            for p in range(PAIRS):
                r0 = (b * PAIRS + p) * HEAD_DIM
                tile = s_scr[r0:r0 + HEAD_DIM, :]
                st_ref[b, 2 * p] = tile[:, 0:HEAD_DIM]
                st_ref[b, 2 * p + 1] = tile[:, HEAD_DIM:]


def _wkv_scan(r, w, k, v, kk, kka, s0, bd256, e4, *, chunk):
    NB, T, _ = r.shape
    rows = SCAN_NB * PAIRS * HEAD_DIM
    xspec = pl.BlockSpec((SCAN_NB, chunk, D_A), lambda g, c: (g, c, 0))
    sspec = pl.BlockSpec((SCAN_NB, H_A, HEAD_DIM, HEAD_DIM), lambda g, c: (g, 0, 0, 0))
    return pl.pallas_call(
        _wkv_scan_kernel,
        grid=(NB // SCAN_NB, T // chunk),
        in_specs=[xspec] * 6 + [sspec,
                                pl.BlockSpec(bd256.shape, lambda g, c: (0, 0)),
                                pl.BlockSpec((PAIRS * HEAD_DIM, LANES), lambda g, c: (0, 0))],
        out_specs=[xspec, sspec],
        out_shape=[jax.ShapeDtypeStruct((NB, T, D_A), f32),
                   jax.ShapeDtypeStruct((NB, H_A, HEAD_DIM, HEAD_DIM), f32)],
        scratch_shapes=[pltpu.VMEM((rows, LANES), f32)],
        compiler_params=_cparams(("parallel", "arbitrary")),
        name="wkv_scan",
    )(r, w, k, v, kk, kka, s0, bd256, e4)


def _even_out_kernel(y_ref, g_ref, bon_ref, lw_ref, lb_ref, bd_ref, yb_ref, wa_ref, wb_ref, x_ref, gate_ref,
                     gpost_ref, o_ref):
    y = y_ref[0]
    bd = bd_ref[...]
    mean = _seg_sum(y, bd) * (1.0 / HEAD_DIM)
    d = y - mean
    var = _seg_sum(d * d, bd) * (1.0 / HEAD_DIM)
    ya = (d * lax.rsqrt(var + LNX_EPS) * lw_ref[...] + lb_ref[...] + bon_ref[0]) * g_ref[0]
    out = (jnp.dot(ya.astype(bf16), wa_ref[...], preferred_element_type=f32)
           + jnp.dot(yb_ref[0].astype(bf16), wb_ref[...], preferred_element_type=f32))
    o_ref[0] = x_ref[0] + gate_ref[0] * _rmsnorm_rows(out, gpost_ref[...])


def _even_out(y, g, bonus, lnx_w, lnx_b, bd512, yb, wa, wb, x, gate, g_post, *, tm):
    B, T, D = x.shape
    per_row = gate.shape[1] != 1
    gate_spec = (pl.BlockSpec((1, tm, D), lambda b, i: (b, i, 0)) if per_row
                 else pl.BlockSpec((1, 1, D), lambda b, i: (b, 0, 0)))
    half = pl.BlockSpec((1, tm, D_A), lambda b, i: (b, i, 0))
    vec = pl.BlockSpec((1, D_A), lambda b, i: (0, 0))
    full = lambda a: pl.BlockSpec(a.shape, lambda b, i: (0, 0))
    return pl.pallas_call(
        _even_out_kernel,
        grid=(B, T // tm),
        in_specs=[half, half, half, vec, vec, full(bd512), half, full(wa), full(wb),
                  pl.BlockSpec((1, tm, D), lambda b, i: (b, i, 0)), gate_spec,
                  pl.BlockSpec((1, D), lambda b, i: (0, 0))],
        out_specs=pl.BlockSpec((1, tm, D), lambda b, i: (b, i, 0)),
        out_shape=jax.ShapeDtypeStruct((B, T, D), f32),
        compiler_params=_cparams(("parallel", "parallel")),
        name="even_out",
    )(y, g, bonus, lnx_w, lnx_b, bd512, yb, wa, wb, x, gate, g_post)


def _fox_gate_kernel(f_ref, b_ref, lf_ref, cum_ref):
    nb, T, _ = f_ref.shape
    z = f_ref[...].reshape(nb * T, LANES) + b_ref[...]
    lf = -_softplus(-z)
    lf_ref[...] = lf.reshape(nb, T, LANES)
    t = lax.broadcasted_iota(jnp.int32, (nb * T, 1), 0) % T
    x = lf
    s = 1
    while s < T:
        x = x + jnp.where(t >= s, pltpu.roll(x, s, 0), 0.0)
        s *= 2
    cum_ref[...] = x.reshape(nb, T, LANES)


def _fox_gate(proj, b_f_pad, *, nb=1):
    B, T, _ = proj.shape
    spec = pl.BlockSpec((nb, T, LANES), lambda b: (b, 0, 0))
    sds = jax.ShapeDtypeStruct((B, T, LANES), f32)
    return pl.pallas_call(
        _fox_gate_kernel,
        grid=(B // nb,),
        in_specs=[pl.BlockSpec((nb, T, LANES), lambda b: (b, 0, EV_F_OFF // LANES)),
                  pl.BlockSpec((1, LANES), lambda b: (0, 0))],
        out_specs=[spec, spec],
        out_shape=[sds, sds],
        compiler_params=_cparams(("parallel",)),
        name="fox_gate",
    )(proj, b_f_pad)


def _flash_kernel(q_ref, k_ref, v_ref, l2c_ref, *rest, n_heads, tq, tk, has_bias, table_everywhere, copy_kv,
                  head_group):
    if has_bias:
        cq_ref, ckt_ref = rest[:2]
        rest = rest[2:]
    o_ref = rest[0]
    rest = rest[1:]
    qi = pl.program_id(1)
    if copy_kv:
        ko_ref, vo_ref = rest[:2]
        rest = rest[2:]

        @pl.when(qi == 0)
        def _():
            ko_ref[0] = k_ref[0]
            vo_ref[0] = v_ref[0]
    if has_bias:
        m_scr, acc_scr, qb_scr, cq2_scr = rest
    else:
        m_scr, acc_scr, qb_scr = rest
    m_scr[...] = jnp.full_like(m_scr, -jnp.inf)
    acc_scr[...] = jnp.zeros_like(acc_scr)
    low_q = lax.broadcasted_iota(jnp.int32, (tq, LANES), 1) < HEAD_DIM
    low_k = lax.broadcasted_iota(jnp.int32, (tk, LANES), 1) < HEAD_DIM
    for h in range(n_heads):
        q_pair = q_ref[0, :, (h // 2) * LANES:(h // 2 + 1) * LANES] * (ATTN_SCALE * LOG2E)
        qb_scr[h] = jnp.where(low_q == (h % 2 == 0), q_pair, 0.0).astype(bf16)
        if has_bias:
            cq2_scr[h] = jnp.broadcast_to(cq_ref[0, :, h:h + 1] * LOG2E, (tq, LANES))
    nt = (((1,), (1,)), ((), ()))

    def block(it, use_table):
        k0 = pl.multiple_of((qi - it) * tk, tk)
        for g0 in range(0, n_heads, head_group):
            heads = range(g0, g0 + head_group)
            pairs = range(g0 // 2, (g0 + head_group) // 2)
            parts = []
            for p in pairs:
                k_pair = k_ref[0, pl.ds(k0, tk), p * LANES:(p + 1) * LANES].astype(bf16)
                q2 = qb_scr[2 * p:2 * p + 2].reshape(2 * tq, LANES)
                s2 = lax.dot_general(q2, k_pair, nt, preferred_element_type=f32)
                for n in range(2):
                    s = s2[n * tq:(n + 1) * tq]
                    if has_bias:
                        s = s - ckt_ref[0, 2 * p + n:2 * p + n + 1, pl.ds(k0, tk)] * LOG2E
                    parts.append(s)
            s3 = jnp.stack(parts)
            if use_table:
                s3 = s3 + l2c_ref[it][None]
            m_prev = m_scr[g0:g0 + head_group]
            m_cur = jnp.max(s3, axis=2, keepdims=True)
            if has_bias:
                cq2 = cq2_scr[g0:g0 + head_group]
                m_next = jnp.maximum(m_prev, m_cur + cq2)
                m_sub = m_next - cq2
            else:
                m_next = jnp.maximum(m_prev, m_cur)
                m_sub = m_next
            p3 = jnp.exp2(s3 - jnp.concatenate([m_sub] * (tk // LANES), axis=2)).astype(bf16)
            alpha = jnp.exp2(m_prev - m_next)
            m_scr[g0:g0 + head_group] = m_next
            v_pairs = {p: v_ref[0, pl.ds(k0, tk), p * LANES:(p + 1) * LANES].astype(bf16) for p in pairs}
            for n, h in enumerate(heads):
                v_ext = jnp.where(low_k == (h % 2 == 0), v_pairs[h // 2], 1.0)
                acc_scr[h] = acc_scr[h] * alpha[n] + jnp.dot(p3[n], v_ext, preferred_element_type=f32)

    block(0, True)

    def body(it, carry):
        block(it, table_everywhere)
        return carry

    lax.fori_loop(1, qi + 1, body, 0)
    for p in range(n_heads // 2):
        acc_e, acc_o = acc_scr[2 * p], acc_scr[2 * p + 1]
        out_e = acc_e / pltpu.roll(acc_e, HEAD_DIM, 1)
        out_o = acc_o / pltpu.roll(acc_o, HEAD_DIM, 1)
        o_ref[0, :, p * LANES:(p + 1) * LANES] = jnp.where(low_q, out_e, out_o)


def _flash_attention(q_src, k_src, v_src, q_blk, k_blk, v_blk, l2c, n_heads, *, tq, tk, bias=None,
                     table_everywhere, head_group, copy_kv=False):
    assert tq == tk
    B, T, _ = q_src.shape
    hd = n_heads * HEAD_DIM
    in_specs = [pl.BlockSpec((1, tq, hd), lambda b, i: (b, i, q_blk)),
                pl.BlockSpec((1, T, hd), lambda b, i: (b, 0, k_blk)),
                pl.BlockSpec((1, T, hd), lambda b, i: (b, 0, v_blk)),
                pl.BlockSpec(l2c.shape, lambda b, i: (0, 0, 0))]
    args = [q_src, k_src, v_src, l2c]
    if bias is not None:
        cum, cum_t = bias
        in_specs += [pl.BlockSpec((1, tq, LANES), lambda b, i: (b, i, 0)),
                     pl.BlockSpec((1, cum_t.shape[1], T), lambda b, i: (b, 0, 0))]
        args += [cum, cum_t]
    scratch = [pltpu.VMEM((n_heads, tq, LANES), f32), pltpu.VMEM((n_heads, tq, LANES), f32),
               pltpu.VMEM((n_heads, tq, LANES), bf16)]
    if bias is not None:
        scratch.append(pltpu.VMEM((n_heads, tq, LANES), f32))
    out_specs = pl.BlockSpec((1, tq, hd), lambda b, i: (b, i, 0))
    out_shape = jax.ShapeDtypeStruct((B, T, hd), f32)
    if copy_kv:
        kv_out = pl.BlockSpec((1, T, hd), lambda b, i: (b, 0, 0))
        out_specs = [out_specs, kv_out, kv_out]
        out_shape = [out_shape] * 3
    return pl.pallas_call(
        functools.partial(_flash_kernel, n_heads=n_heads, tq=tq, tk=tk, has_bias=bias is not None,
                          table_everywhere=table_everywhere, copy_kv=copy_kv, head_group=head_group),
        grid=(B, T // tq),
        in_specs=in_specs,
        out_specs=out_specs,
        out_shape=out_shape,
        scratch_shapes=scratch,
        compiler_params=_cparams(("parallel", "arbitrary")),
        name="flash_attention",
    )(*args)


def _qk(q_h, k_h):
    return lax.dot_general(q_h, k_h, (((1,), (1,)), ((), ())), preferred_element_type=f32)


FOX_PP = 32
DIL_KC = 2048


def _block_diag_queries(q2, n_heads):
    nq, hd = q2.shape
    qt = jnp.concatenate([q2] * n_heads, axis=0)
    rh = lax.broadcasted_iota(jnp.int32, (n_heads * nq, hd), 0) // nq
    ch = lax.broadcasted_iota(jnp.int32, (n_heads * nq, hd), 1) // HEAD_DIM
    return jnp.where(rh == ch, qt, 0.0).astype(bf16)


def _rows_per_head(x, nq):
    n_heads, n = x.shape
    return jnp.broadcast_to(x[:, None, :], (n_heads, nq, n)).reshape(n_heads * nq, n)


def _decode_first(s, v_new, m_scr, l_scr, accn_scr, acct_scr):
    m = jnp.max(s, axis=1, keepdims=True)
    p = jnp.exp2(s - m)
    m_scr[...] = jnp.broadcast_to(m, m_scr.shape)
    l_scr[...] = jnp.broadcast_to(jnp.sum(p, axis=1, keepdims=True), l_scr.shape)
    accn_scr[...] = jnp.dot(p.astype(bf16), v_new, preferred_element_type=f32)
    acct_scr[...] = jnp.zeros_like(acct_scr)


def _decode_update(s, vt_all, m_scr, l_scr, accn_scr, acct_scr):
    R = s.shape[0]
    hd = vt_all.shape[0]
    m_prev = m_scr[...]
    m_next = jnp.maximum(m_prev, jnp.max(s, axis=1, keepdims=True))
    p = jnp.exp2(s - m_next[:, 0:1])
    alpha = jnp.exp2(m_prev - m_next)
    l_scr[...] = alpha * l_scr[...] + jnp.sum(p, axis=1, keepdims=True)
    m_scr[...] = m_next
    accn_scr[...] = accn_scr[...] * jnp.concatenate([alpha] * (hd // LANES), axis=1)
    new_t = lax.dot_general(vt_all, p.astype(bf16), (((1,), (1,)), ((), ())), preferred_element_type=f32)
    if R < LANES:
        alpha = jnp.concatenate([alpha, jnp.zeros((LANES - R, LANES), f32)], axis=0)
    alpha_row = jnp.transpose(alpha)[0:1, 0:R]
    acct_scr[:, 0:R] = acct_scr[:, 0:R] * alpha_row + new_t


def _decode_finish(o_ref, n_heads, nq, l_scr, accn_scr, acct_scr):
    for h in range(n_heads):
        blk = acct_scr[h * HEAD_DIM:(h + 1) * HEAD_DIM, :]
        sq = jnp.transpose(jnp.concatenate([blk, jnp.zeros((LANES - HEAD_DIM, LANES), f32)], axis=0))
        rows = slice(h * nq, (h + 1) * nq)
        cols = slice(h * HEAD_DIM, (h + 1) * HEAD_DIM)
        num = sq[rows, 0:HEAD_DIM] + accn_scr[rows, cols]
        o_ref[0, :, cols] = num / l_scr[rows, 0:HEAD_DIM]


def _fox_decode2_kernel(pt_ref, q_ref, kn_ref, vn_ref, cqr_ref, cnt_ref, *rest):
    lf_refs = rest[:FOX_PP]
    k_refs = rest[FOX_PP:2 * FOX_PP]
    v_refs = rest[2 * FOX_PP:3 * FOX_PP]
    o_ref, m_scr, l_scr, accn_scr, acct_scr, tot_scr, qbd_scr = rest[3 * FOX_PP:]
    j = pl.program_id(1)
    nq = q_ref.shape[1]
    R = H_B * nq
    cq2 = cqr_ref[0] * LOG2E

    @pl.when(j == 0)
    def _():
        qbd = _block_diag_queries(q_ref[0] * (ATTN_SCALE * LOG2E), H_B)
        qbd_scr[...] = qbd
        tot_scr[...] = jnp.zeros_like(tot_scr)
        s = _qk(qbd, kn_ref[0].astype(bf16))
        s = s + cq2[:, 0:nq] - _rows_per_head(cnt_ref[0][:, 0:nq] * LOG2E, nq)
        qi = lax.broadcasted_iota(jnp.int32, (R, nq), 0) % nq
        kj = lax.broadcasted_iota(jnp.int32, (R, nq), 1)
        s = jnp.where(kj <= qi, s, -jnp.inf)
        _decode_first(s, vn_ref[0].astype(bf16), m_scr, l_scr, accn_scr, acct_scr)

    lane = lax.broadcasted_iota(jnp.int32, (H_B, PAGE_SIZE), 1)
    qbd = qbd_scr[...]
    tot = tot_scr[...]
    s_parts = []
    v_parts = []
    for i in range(FOX_PP):
        x = lf_refs[i][0]
        sft = 1
        while sft < PAGE_SIZE:
            x = x + jnp.where(lane >= sft, pltpu.roll(x, sft, 1), 0.0)
            sft *= 2
        page_tot = x[:, PAGE_SIZE - 1:PAGE_SIZE]
        bias2 = _rows_per_head((page_tot - x + tot) * LOG2E, nq) + cq2
        tot = tot + page_tot
        kt_all = k_refs[i][0].reshape(D_B, PAGE_SIZE).astype(bf16)
        s_parts.append(jnp.dot(qbd, kt_all, preferred_element_type=f32) + bias2)
        v_parts.append(v_refs[i][0].reshape(D_B, PAGE_SIZE).astype(bf16))
    tot_scr[...] = tot
    _decode_update(jnp.concatenate(s_parts, axis=1), jnp.concatenate(v_parts, axis=1),
                   m_scr, l_scr, accn_scr, acct_scr)

    @pl.when(j == pl.num_programs(1) - 1)
    def _():
        _decode_finish(o_ref, H_B, nq, l_scr, accn_scr, acct_scr)


def _fox_decode2(page_table, proj, cum_rows, cum_new_t, lf_cache_t, k_cache, v_cache):
    NB, nq, _ = proj.shape
    n_pages = page_table.shape[1]
    steps = n_pages // FOX_PP
    R = H_B * nq

    def lf_spec(i):
        return pl.BlockSpec((1, H_B, PAGE_SIZE),
                            lambda b, j, pt: (pt[b, n_pages - 1 - (j * FOX_PP + i)], 0, 0))

    def kv_spec(i):
        return pl.BlockSpec((1, H_B, HEAD_DIM, PAGE_SIZE),
                            lambda b, j, pt: (pt[b, n_pages - 1 - (j * FOX_PP + i)], 0, 0, 0))

    in_specs = [pl.BlockSpec((1, nq, D_B), lambda b, j, pt: (b, 0, EV_Q_OFF // D_B)),
                pl.BlockSpec((1, nq, D_B), lambda b, j, pt: (b, 0, EV_K_OFF // D_B)),
                pl.BlockSpec((1, nq, D_B), lambda b, j, pt: (b, 0, EV_V_OFF // D_B)),
                pl.BlockSpec((1, R, LANES), lambda b, j, pt: (b, 0, 0)),
                pl.BlockSpec((1, SUBLANES, LANES), lambda b, j, pt: (b, 0, 0))]
    args = [proj, proj, proj, cum_rows, cum_new_t]
    for i in range(FOX_PP):
        in_specs.append(lf_spec(i))
        args.append(lf_cache_t)
    for cache in (k_cache, v_cache):
        for i in range(FOX_PP):
            in_specs.append(kv_spec(i))
            args.append(cache)
    grid_spec = pltpu.PrefetchScalarGridSpec(
        num_scalar_prefetch=1,
        grid=(NB, steps),
        in_specs=in_specs,
        out_specs=pl.BlockSpec((1, nq, D_B), lambda b, j, pt: (b, 0, 0)),
        scratch_shapes=[pltpu.VMEM((R, LANES), f32), pltpu.VMEM((R, LANES), f32),
                        pltpu.VMEM((R, D_B), f32), pltpu.VMEM((D_B, LANES), f32),
                        pltpu.VMEM((H_B, LANES), f32), pltpu.VMEM((R, D_B), bf16)],
    )
    return pl.pallas_call(
        _fox_decode2_kernel,
        grid_spec=grid_spec,
        out_shape=jax.ShapeDtypeStruct((NB, nq, D_B), f32),
        compiler_params=_cparams(("parallel", "arbitrary")),
        name="fox_decode",
    )(page_table, *args)


def _dil_decode2_kernel(q_ref, kn_ref, vn_ref, l2new_ref, l2past_ref, k_ref, v_ref, o_ref,
                        m_scr, l_scr, accn_scr, acct_scr, qbd_scr):
    j = pl.program_id(1)
    nq = q_ref.shape[1]

    @pl.when(j == 0)
    def _():
        qbd = _block_diag_queries(q_ref[0] * (ATTN_SCALE * LOG2E), H_C)
        qbd_scr[...] = qbd
        s = _qk(qbd, kn_ref[0].astype(bf16)) + jnp.concatenate([l2new_ref[:, 0:nq]] * H_C, axis=0)
        _decode_first(s, vn_ref[0].astype(bf16), m_scr, l_scr, accn_scr, acct_scr)

    kt_all = k_ref[0].reshape(D_C, DIL_KC).astype(bf16)
    s = jnp.dot(qbd_scr[...], kt_all, preferred_element_type=f32)
    s = s + jnp.concatenate([l2past_ref[...]] * H_C, axis=0)
    _decode_update(s, v_ref[0].reshape(D_C, DIL_KC).astype(bf16), m_scr, l_scr, accn_scr, acct_scr)

    @pl.when(j == pl.num_programs(1) - 1)
    def _():
        _decode_finish(o_ref, H_C, nq, l_scr, accn_scr, acct_scr)


def _dil_decode2(qkv, l2_new, l2_past, k_state, v_state):
    NB, nq, _ = qkv.shape
    L = k_state.shape[3]
    R = H_C * nq
    kv_spec = pl.BlockSpec((1, H_C, HEAD_DIM, DIL_KC), lambda b, j: (b, 0, 0, j))
    return pl.pallas_call(
        _dil_decode2_kernel,
        grid=(NB, L // DIL_KC),
        in_specs=[pl.BlockSpec((1, nq, D_C), lambda b, j: (b, 0, 0)),
                  pl.BlockSpec((1, nq, D_C), lambda b, j: (b, 0, 1)),
                  pl.BlockSpec((1, nq, D_C), lambda b, j: (b, 0, 2)),
                  pl.BlockSpec(l2_new.shape, lambda b, j: (0, 0)),
                  pl.BlockSpec((nq, DIL_KC), lambda b, j: (0, j)),
                  kv_spec, kv_spec],
        out_specs=pl.BlockSpec((1, nq, D_C), lambda b, j: (b, 0, 0)),
        out_shape=jax.ShapeDtypeStruct((NB, nq, D_C), f32),
        scratch_shapes=[pltpu.VMEM((R, LANES), f32), pltpu.VMEM((R, LANES), f32),
                        pltpu.VMEM((R, D_C), f32), pltpu.VMEM((D_C, LANES), f32),
                        pltpu.VMEM((R, D_C), bf16)],
        compiler_params=_cparams(("parallel", "arbitrary")),
        name="dil_decode",
    )(qkv, qkv, qkv, l2_new, l2_past, k_state, v_state)


def _dil_multiplicity(delta):
    delta = np.asarray(delta)
    c = np.zeros(delta.shape, np.float32)
    for w, d in DIL_PAIRS:
        c += ((delta >= 0) & (delta % d == 0) & (delta <= w)).astype(np.float32)
    return c


def _log2_block_tables(n_blocks, t, mult_fn):
    i = np.arange(t)[:, None]
    j = np.arange(t)[None, :]
    with np.errstate(divide="ignore"):
        tabs = [np.log2(mult_fn(i - j + d * t)).astype(np.float32) for d in range(n_blocks)]
    return jnp.asarray(np.stack(tabs))


def _rope_tables(pos):
    half = HEAD_DIM // 2
    inv = ROPE_THETA ** (-jnp.arange(half, dtype=f32) / half)
    ang = pos.astype(f32)[:, None] * inv[None, :]
    cos = jnp.cos(ang)
    sin = jnp.sin(ang)
    return jnp.tile(cos, (1, 4)), jnp.concatenate([-sin, sin, -sin, sin], axis=1)


def _mods(mod_lj, n_prompt, per_row_repeat):
    mp = mod_lj[:n_prompt][:, None, :]
    ms = jnp.repeat(mod_lj[n_prompt:], per_row_repeat, axis=0)[None]
    sp = [mp[..., i * D_MODEL:(i + 1) * D_MODEL] for i in range(3)]
    ss = [ms[..., i * D_MODEL:(i + 1) * D_MODEL] for i in range(3)]
    return sp, ss


def kernel(x_prompt, x_sample, cache_fox_k, cache_fox_v, cache_fox_logf, state_rwkv_wkv, state_rwkv_shift,
           state_dil_k, state_dil_v, page_table, c_prompt, c_sample, ada_w, ada_b, norm_pre, norm_post,
           mlp_up, mlp_down, ev_w_in, ev_w_out, rw_mu, rw_w0, rw_w2, rw_a0, rw_a2, rw_g2, rw_k_k, rw_k_a,
           rw_r_k, rw_lnx_w, rw_lnx_b, fox_b_f, od_w_in, od_w_out):
    BP, T, D = x_prompt.shape
    BS, TS, _ = x_sample.shape
    depth = ada_w.shape[0]
    past_len = page_table.shape[1] * PAGE_SIZE

    mod = _ada_mod(jnp.concatenate([c_prompt, c_sample], axis=0), ada_w, ada_b)

    idx = np.arange(LANES)
    idx2 = np.arange(2 * LANES)
    bd256 = jnp.asarray((idx2[:, None] // HEAD_DIM == idx2[None, :] // HEAD_DIM).astype(np.float32)).astype(bf16)
    idx5 = np.arange(D_A)
    bd512 = jnp.asarray((idx5[:, None] // HEAD_DIM == idx5[None, :] // HEAD_DIM).astype(np.float32)).astype(bf16)
    e4 = jnp.asarray(np.tile((idx[None, :] % HEAD_DIM == np.arange(HEAD_DIM)[:, None]).astype(np.float32),
                             (PAIRS, 1)))
    TQ = 256
    causal_tab = _log2_block_tables(1, TQ, lambda dl: (dl >= 0).astype(np.float32))
    dil_tab = _log2_block_tables(T // TQ, TQ, _dil_multiplicity)

    xp = x_prompt
    xs = x_sample.reshape(1, BS * TS, D)
    outs = {}
    for l in range(depth):
        i = l // 2
        (sh_p, sc_p, gt_p), (sh_s, sc_s, gt_s) = _mods(mod[2 * l], BP, TS)
        g_pre = norm_pre[l, 0][None]
        g_post = norm_post[l, 0][None]
        if l % 2 == 0:
            w_in = ev_w_in[i]
            zeros = lambda n: jnp.zeros((D, n), f32)
            w_pad = jnp.concatenate([w_in[:, :RW_COLS], zeros(EV_Q_OFF - RW_COLS), w_in[:, RW_COLS:],
                                     zeros(EV_COLS_PAD - EV_F_OFF - H_B)], axis=1).astype(bf16)
            w_out = ev_w_out[i].astype(bf16)
            b_f_pad = jnp.pad(fox_b_f[i], (0, LANES - H_B))[None]
            rw_args = (rw_mu[i][None], rw_w0[i][None], rw_w2[i].astype(bf16), rw_a0[i][None],
                       rw_a2[i].astype(bf16), rw_g2[i].astype(bf16), rw_k_k[i][None], rw_k_a[i][None],
                       rw_r_k[i].reshape(1, D_A), bd512)

            proj_p = _norm_matmul(xp, sh_p, sc_p, g_pre, w_pad, tm=2048, tn=768)
            r, w, k2, v, kk, kka, g, bonus = _rwkv_pre(proj_p, jnp.zeros((BP, 1, RW_COLS), f32), *rw_args, tt=256)
            y, st = _wkv_scan(r, w, k2, v, kk, kka, jnp.zeros((BP, H_A, HEAD_DIM, HEAD_DIM), f32), bd256, e4,
                              chunk=64)
            lf_p, cum_p = _fox_gate(proj_p, b_f_pad)
            cum_t = jnp.transpose(cum_p[:, :, :SUBLANES], (0, 2, 1))
            yb, fk, fv = _flash_attention(proj_p, proj_p, proj_p, EV_Q_OFF // D_B, EV_K_OFF // D_B,
                                          EV_V_OFF // D_B, causal_tab, H_B, tq=TQ, tk=TQ, bias=(cum_p, cum_t),
                                          table_everywhere=False, head_group=8, copy_kv=True)
            xp = _even_out(y, g, bonus, rw_lnx_w[i][None], rw_lnx_b[i][None], bd512, yb, w_out[:D_A], w_out[D_A:],
                           xp, gt_p, g_post, tm=512)
            outs.setdefault("fk_p", []).append(fk.reshape(BP, T, H_B, HEAD_DIM))
            outs.setdefault("fv_p", []).append(fv.reshape(BP, T, H_B, HEAD_DIM))
            outs.setdefault("fl_p", []).append(lf_p[:, :, :H_B])
            outs.setdefault("wkv_p", []).append(st)
            outs.setdefault("sh_p", []).append(proj_p[:, T - 1, :RW_COLS])

            proj_s = _norm_matmul(xs, sh_s, sc_s, g_pre, w_pad, tm=BS * TS, tn=1280).reshape(BS, TS, EV_COLS_PAD)
            r, w, k2, v, kk, kka, g, bonus = _rwkv_pre(proj_s, state_rwkv_shift[i][:, None, :], *rw_args, tt=TS,
                                                       nb=BS)
            y, st = _wkv_scan(r, w, k2, v, kk, kka, state_rwkv_wkv[i], bd256, e4, chunk=TS)
            flat = lambda t: t.reshape(1, BS * TS, D_A)
            lf_s, cum_s = _fox_gate(proj_s, b_f_pad, nb=BS)
            cum_s_t = jnp.pad(jnp.transpose(cum_s[:, :, :SUBLANES], (0, 2, 1)), ((0, 0), (0, 0), (0, LANES - TS)))
            lf_cache_t = jnp.transpose(cache_fox_logf[i], (0, 2, 1))
            cum_rows = jnp.broadcast_to(
                jnp.transpose(cum_s[:, :, :H_B], (0, 2, 1)).reshape(BS, H_B * TS, 1), (BS, H_B * TS, LANES))
            yb = _fox_decode2(page_table, proj_s, cum_rows, cum_s_t, lf_cache_t,
                              jnp.transpose(cache_fox_k[i], (0, 2, 3, 1)),
                              jnp.transpose(cache_fox_v[i], (0, 2, 3, 1)))
            xs = _even_out(flat(y), flat(g), flat(bonus), rw_lnx_w[i][None], rw_lnx_b[i][None], bd512, flat(yb),
                           w_out[:D_A], w_out[D_A:], xs, gt_s, g_post, tm=BS * TS)
            outs.setdefault("fk_s", []).append(proj_s[:, :, EV_K_OFF:EV_K_OFF + D_B].reshape(BS, TS, H_B, HEAD_DIM))
            outs.setdefault("fv_s", []).append(proj_s[:, :, EV_V_OFF:EV_V_OFF + D_B].reshape(BS, TS, H_B, HEAD_DIM))
            outs.setdefault("fl_s", []).append(lf_s[:, :, :H_B])
            outs.setdefault("wkv_s", []).append(st)
            outs.setdefault("sh_s", []).append(proj_s[:, TS - 1, :RW_COLS])
        else:
            w_in = od_w_in[i].astype(bf16)
            w_out = od_w_out[i].astype(bf16)
            rope_p = _rope_tables(jnp.arange(T))
            cos_s, sin_s = _rope_tables(past_len + jnp.arange(TS))
            rope_s = (jnp.tile(cos_s, (BS, 1)), jnp.tile(sin_s, (BS, 1)))

            q_p, k_p, v_p = _norm_matmul(xp, sh_p, sc_p, g_pre, w_in, tm=1024, tn=D_C, rope=rope_p,
                                         n_rope_cols=2 * D_C, split=True)
            o_p = _flash_attention(q_p, k_p, v_p, 0, 0, 0, dil_tab, H_C, tq=TQ, tk=TQ, table_everywhere=True,
                                   head_group=4)
            xp = _matmul_post([o_p], [w_out], xp, gt_p, g_post, tm=512)
            keep = min(DIL_PAIRS[-1][0], T)
            outs.setdefault("dk_p", []).append(k_p[:, T - keep:].reshape(BP, keep, H_C, HEAD_DIM))
            outs.setdefault("dv_p", []).append(v_p[:, T - keep:].reshape(BP, keep, H_C, HEAD_DIM))

            qkv_s = _norm_matmul(xs, sh_s, sc_s, g_pre, w_in, tm=BS * TS, tn=D_C, rope=rope_s,
                                 n_rope_cols=2 * D_C).reshape(BS, TS, 3 * D_C)
            L = state_dil_k.shape[2]
            tq_idx = np.arange(TS)[:, None]
            c_new = np.zeros((TS, LANES), np.float32)
            c_new[:, :TS] = _dil_multiplicity(tq_idx - np.arange(TS)[None, :])
            c_past = _dil_multiplicity(L + tq_idx - np.arange(L)[None, :])
            with np.errstate(divide="ignore"):
                l2_new, l2_past = jnp.asarray(np.log2(c_new)), jnp.asarray(np.log2(c_past))
            o_s = _dil_decode2(qkv_s, l2_new, l2_past,
                               jnp.transpose(state_dil_k[i], (0, 2, 3, 1)),
                               jnp.transpose(state_dil_v[i], (0, 2, 3, 1)))
            xs = _matmul_post([o_s.reshape(1, BS * TS, D_C)], [w_out], xs, gt_s, g_post, tm=BS * TS)
            outs.setdefault("dk_s", []).append(qkv_s[:, :, D_C:2 * D_C].reshape(BS, TS, H_C, HEAD_DIM))
            outs.setdefault("dv_s", []).append(qkv_s[:, :, 2 * D_C:].reshape(BS, TS, H_C, HEAD_DIM))

        (sh_p, sc_p, gt_p), (sh_s, sc_s, gt_s) = _mods(mod[2 * l + 1], BP, TS)
        g_pre = norm_pre[l, 1][None]
        g_post = norm_post[l, 1][None]
        w_up = mlp_up[l]
        w_down = mlp_down[l]
        xp = _mlp(xp, sh_p, sc_p, gt_p, g_pre, g_post, w_up, w_down, tm=1024, tf=1024)
        xs = _mlp(xs, sh_s, sc_s, gt_s, g_pre, g_post, w_up, w_down, tm=BS * TS, tf=1024)

    st = lambda name: jnp.stack(outs[name])
    return (xp, xs.reshape(BS, TS, D), st("fk_p"), st("fv_p"), st("fl_p"), st("fk_s"), st("fv_s"), st("fl_s"),
            st("wkv_p"), st("wkv_s"), st("sh_p"), st("sh_s"), st("dk_p"), st("dv_p"), st("dk_s"), st("dv_s"))
```
